```python
import jax, jax.numpy as jnp
from jax import lax
import numpy as np

D_MODEL = 1024
BATCH = 32
SEQ = 2048
DEPTH = 1
DEC_BATCH = 8
DEC_SEQ = 16
PAST_LEN = 1024

CHUNK = 64
LEFT_CHUNKS = 8
BAND = (LEFT_CHUNKS + 1) * CHUNK
ATT_HEADS = 8
ATT_HEAD_DIM = 64
ATT_WIDTH = ATT_HEADS * ATT_HEAD_DIM
REL_MAX = 128
CONV_WIDTH = 512
CONV_K = 3
MEM_TOKENS = 256
MEM_HEADS = 4
MEM_HEAD_DIM = 128
MEM_WIDTH = MEM_HEADS * MEM_HEAD_DIM
IN_COLS = 3 * ATT_WIDTH + 3 * CONV_WIDTH + MEM_WIDTH
SPLITS = (ATT_WIDTH, 2 * ATT_WIDTH, 3 * ATT_WIDTH,
          3 * ATT_WIDTH + CONV_WIDTH, 3 * ATT_WIDTH + 2 * CONV_WIDTH,
          3 * ATT_WIDTH + 3 * CONV_WIDTH)
N_EXPERTS = 32
TOP_K = 4
D_FF = D_MODEL
SWIGLU_LIMIT = 7.0
SWIGLU_ALPHA = 1.702
DN_ALPHA = (2.0 * DEPTH) ** 0.25
DN_BETA = (8.0 * DEPTH) ** -0.25
LN_EPS = 1e-5
NEG_INF = -1e30

kernel_name = 'hybrid_stream_encoder_step'


def layer_norm(x, g, b):
    xf = x.astype(jnp.float32)
    mu = jnp.mean(xf, axis=-1, keepdims=True)
    var = jnp.mean(jnp.square(xf - mu), axis=-1, keepdims=True)
    y = (xf - mu) * lax.rsqrt(var + LN_EPS) * g.astype(jnp.float32) + b.astype(jnp.float32)
    return y.astype(x.dtype)


def split_mixer_inputs(x, w_in):
    lead = x.shape[:-1]
    q, k, v, bg, cg, hc, qm = jnp.split(x @ w_in, SPLITS, axis=-1)
    att = lead + (ATT_HEADS, ATT_HEAD_DIM)
    return (q.reshape(att), k.reshape(att), v.reshape(att), bg, cg, hc,
            qm.reshape(lead + (MEM_HEADS, MEM_HEAD_DIM)))


def rel_bias_and_mask(rel_table, q_pos, k_pos):
    dist = q_pos[..., :, None] - k_pos[..., None, :]
    bias = rel_table[:, jnp.clip(dist, -REL_MAX, REL_MAX) + REL_MAX]
    q_chunk = q_pos[..., :, None] // CHUNK
    k_chunk = k_pos[..., None, :] // CHUNK
    valid = ((k_pos[..., None, :] >= 0) & (k_chunk <= q_chunk)
             & (k_chunk >= q_chunk - LEFT_CHUNKS))
    return bias, valid


def chunk_band_attention(q, k, v, rel_table, q_pos, k_pos):
    bias, valid = rel_bias_and_mask(rel_table, q_pos, k_pos)
    s = jnp.einsum('bcqhd,bckhd->bhcqk', q, k).astype(jnp.float32) * (ATT_HEAD_DIM ** -0.5)
    s = jnp.where(valid, s + bias.astype(jnp.float32), NEG_INF)
    p = jax.nn.softmax(s, axis=-1).astype(v.dtype)
    return jnp.einsum('bhcqk,bckhd->bcqhd', p, v)


def prompt_attention(q, k, v, rel_table):
    b, s = q.shape[:2]
    n_chunks = s // CHUNK
    pad = LEFT_CHUNKS * CHUNK
    band_idx = jnp.arange(n_chunks)[:, None] * CHUNK + jnp.arange(BAND)[None, :]
    widths = ((0, 0), (pad, 0), (0, 0), (0, 0))
    kb = jnp.pad(k, widths)[:, band_idx]
    vb = jnp.pad(v, widths)[:, band_idx]
    q_pos = jnp.arange(s).reshape(n_chunks, CHUNK)
    k_pos = band_idx - pad
    qc = q.reshape(b, n_chunks, CHUNK, ATT_HEADS, ATT_HEAD_DIM)
    o = chunk_band_attention(qc, kb, vb, rel_table, q_pos, k_pos)
    return o.reshape(b, s, ATT_WIDTH)


def sample_attention(q, k_new, v_new, k_cache, v_cache, rel_table):
    b, s = q.shape[:2]
    w = k_cache.shape[1]
    k = jnp.concatenate([k_cache, k_new], axis=1)
    v = jnp.concatenate([v_cache, v_new], axis=1)
    q_pos = PAST_LEN + jnp.arange(s)
    k_pos = jnp.concatenate([PAST_LEN - w + jnp.arange(w), q_pos])
    o = chunk_band_attention(q[:, None], k[:, None], v[:, None], rel_table,
                             q_pos[None], k_pos[None])
    return o.reshape(b, s, ATT_WIDTH)


def gated_short_conv(u_padded, gate_b, conv_w):
    n = gate_b.shape[1]
    y = sum(conv_w[j] * u_padded[:, j:j + n] for j in range(CONV_K))
    return gate_b * y


def memory_kv(mem, w_mem_kv):
    lead = mem.shape[:-1]
    mk, mv = jnp.split(mem @ w_mem_kv, 2, axis=-1)
    shp = lead + (MEM_HEADS, MEM_HEAD_DIM)
    return mk.reshape(shp), mv.reshape(shp)


def memory_attention(qm, mk, mv):
    s = jnp.einsum('bshd,bmhd->bhsm', qm, mk).astype(jnp.float32) * (MEM_HEAD_DIM ** -0.5)
    p = jax.nn.softmax(s, axis=-1).astype(mv.dtype)
    o = jnp.einsum('bhsm,bmhd->bshd', p, mv)
    return o.reshape(o.shape[:2] + (MEM_WIDTH,))


def merge_branches(x, a, c, m, w_gate, w_br_attn, w_br_conv, w_br_mem, w_out):
    ga, gc, gm = jnp.split(jax.nn.sigmoid(x @ w_gate), 3, axis=-1)
    return (ga * (a @ w_br_attn) + gc * (c @ w_br_conv) + gm * (m @ w_br_mem)) @ w_out


def moe_ffn(x, w_router, b_router, w_gate_up, b_gate_up, w_down, b_down):
    shp = x.shape
    t = x.reshape(-1, D_MODEL)
    logits = (t @ w_router + b_router).astype(jnp.float32)
    top_val, top_idx = lax.top_k(logits, TOP_K)
    gates = jax.nn.softmax(top_val, axis=-1)
    combine = jnp.einsum('tk,tke->te', gates,
                         jax.nn.one_hot(top_idx, N_EXPERTS, dtype=jnp.float32)).astype(x.dtype)
    out = jnp.zeros_like(t)
    for e in range(N_EXPERTS):
        hgu = t @ w_gate_up[e] + b_gate_up[e]
        glu = jnp.minimum(hgu[:, :D_FF], SWIGLU_LIMIT)
        lin = jnp.clip(hgu[:, D_FF:], -SWIGLU_LIMIT, SWIGLU_LIMIT)
        act = glu * jax.nn.sigmoid(SWIGLU_ALPHA * glu) * (lin + 1.0)
        out = out + combine[:, e:e + 1] * (act @ w_down[e] + b_down[e])
    return out.reshape(shp)


def setup_inputs(seed: int = 0) -> dict:
    key = jax.random.key(seed)
    ks = jax.random.split(key, 27)
    f32 = jnp.float32

    def nrm(k, shape, scale):
        return jax.random.normal(k, shape, f32) * scale

    att_cache = min(LEFT_CHUNKS * CHUNK, PAST_LEN)
    L = DEPTH
    D = D_MODEL
    E = N_EXPERTS
    return {
        'x_prompt': nrm(ks[0], (BATCH, SEQ, D), 1.0),
        'x_sample': nrm(ks[1], (DEC_BATCH, DEC_SEQ, D), 1.0),
        'mem_prompt': nrm(ks[2], (BATCH, MEM_TOKENS, D), 1.0),
        'cache_attn_k': nrm(ks[3], (L, DEC_BATCH, att_cache, ATT_HEADS, ATT_HEAD_DIM), 1.0),
        'cache_attn_v': nrm(ks[4], (L, DEC_BATCH, att_cache, ATT_HEADS, ATT_HEAD_DIM), 1.0),
        'cache_conv': nrm(ks[5], (L, DEC_BATCH, CONV_K - 1, CONV_WIDTH), 1.0),
        'cache_mem_k': nrm(ks[6], (L, DEC_BATCH, MEM_TOKENS, MEM_HEADS, MEM_HEAD_DIM), 1.0),
        'cache_mem_v': nrm(ks[7], (L, DEC_BATCH, MEM_TOKENS, MEM_HEADS, MEM_HEAD_DIM), 1.0),
        'w_in': nrm(ks[8], (L, D, IN_COLS), D ** -0.5),
        'rel_bias': nrm(ks[9], (L, ATT_HEADS, 2 * REL_MAX + 1), 0.5),
        'conv_w': nrm(ks[10], (L, CONV_K, CONV_WIDTH), CONV_K ** -0.5),
        'w_mem_kv': nrm(ks[11], (L, D, 2 * MEM_WIDTH), D ** -0.5),
        'w_gate': nrm(ks[12], (L, D, 3 * D), D ** -0.5),
        'w_br_attn': nrm(ks[13], (L, ATT_WIDTH, D), ATT_WIDTH ** -0.5),
        'w_br_conv': nrm(ks[14], (L, CONV_WIDTH, D), CONV_WIDTH ** -0.5),
        'w_br_mem': nrm(ks[15], (L, MEM_WIDTH, D), MEM_WIDTH ** -0.5),
        'w_out': nrm(ks[16], (L, D, D), D ** -0.5 * DN_BETA),
        'ln1_g': 1.0 + nrm(ks[17], (L, D), 0.02),
        'ln1_b': nrm(ks[18], (L, D), 0.02),
        'w_router': nrm(ks[19], (L, D, E), D ** -0.5),
        'b_router': nrm(ks[20], (L, E), 0.01),
        'w_gate_up': nrm(ks[21], (L, E, D, 2 * D_FF), D ** -0.5),
        'b_gate_up': nrm(ks[22], (L, E, 2 * D_FF), 0.01),
        'w_down': nrm(ks[23], (L, E, D_FF, D), D_FF ** -0.5 * DN_BETA),
        'b_down': nrm(ks[24], (L, E, D), 0.01),
        'ln2_g': 1.0 + nrm(ks[25], (L, D), 0.02),
        'ln2_b': nrm(ks[26], (L, D), 0.02),
    }


def reference(x_prompt, x_sample, mem_prompt, cache_attn_k, cache_attn_v, cache_conv,
              cache_mem_k, cache_mem_v, w_in, rel_bias, conv_w, w_mem_kv, w_gate,
              w_br_attn, w_br_conv, w_br_mem, w_out, ln1_g, ln1_b, w_router, b_router,
              w_gate_up, b_gate_up, w_down, b_down, ln2_g, ln2_b):
    xp = x_prompt
    xs = x_sample
    ak_p, av_p, cv_p, mk_p, mv_p = [], [], [], [], []
    ak_s, av_s, cv_s = [], [], []
    for l in range(DEPTH):
        merge_w = (w_gate[l], w_br_attn[l], w_br_conv[l], w_br_mem[l], w_out[l])
        moe_w = (w_router[l], b_router[l], w_gate_up[l], b_gate_up[l], w_down[l], b_down[l])

        q, k, v, bg, cg, hc, qm = split_mixer_inputs(xp, w_in[l])
        a = prompt_attention(q, k, v, rel_bias[l])
        u = jnp.pad(cg * hc, ((0, 0), (CONV_K - 1, 0), (0, 0)))
        c = gated_short_conv(u, bg, conv_w[l])
        mk, mv = memory_kv(mem_prompt, w_mem_kv[l])
        m = memory_attention(qm, mk, mv)
        xp = layer_norm(DN_ALPHA * xp + merge_branches(xp, a, c, m, *merge_w), ln1_g[l], ln1_b[l])
        xp = layer_norm(DN_ALPHA * xp + moe_ffn(xp, *moe_w), ln2_g[l], ln2_b[l])
        keep = min(LEFT_CHUNKS * CHUNK, k.shape[1])
        ak_p.append(k[:, -keep:])
        av_p.append(v[:, -keep:])
        cv_p.append(u[:, -(CONV_K - 1):])
        mk_p.append(mk)
        mv_p.append(mv)

        q, k, v, bg, cg, hc, qm = split_mixer_inputs(xs, w_in[l])
        a = sample_attention(q, k, v, cache_attn_k[l], cache_attn_v[l], rel_bias[l])
        u = jnp.concatenate([cache_conv[l], cg * hc], axis=1)
        c = gated_short_conv(u, bg, conv_w[l])
        m = memory_attention(qm, cache_mem_k[l], cache_mem_v[l])
        xs = layer_norm(DN_ALPHA * xs + merge_branches(xs, a, c, m, *merge_w), ln1_g[l], ln1_b[l])
        xs = layer_norm(DN_ALPHA * xs + moe_ffn(xs, *moe_w), ln2_g[l], ln2_b[l])
        ak_s.append(k)
        av_s.append(v)
        cv_s.append(u[:, -(CONV_K - 1):])

    return (xp, xs, jnp.stack(ak_p), jnp.stack(av_p), jnp.stack(cv_p), jnp.stack(mk_p),
            jnp.stack(mv_p), jnp.stack(ak_s), jnp.stack(av_s), jnp.stack(cv_s))
```

```python
import functools

import jax
import jax.numpy as jnp
from jax import lax
from jax.experimental import pallas as pl
from jax.experimental.pallas import tpu as pltpu

F32 = jnp.float32
BF16 = jnp.bfloat16
I32 = jnp.int32

D_MODEL = 1024
CHUNK = 64
LEFT_CHUNKS = 8
HIST = LEFT_CHUNKS * CHUNK
BAND = HIST + CHUNK
ATT_HEADS = 8
ATT_DIM = 64
REL_MAX = 128
WIDTH = 512
MEM_TOKENS = 256
MEM_HEADS = 4
MEM_DIM = 128
N_EXPERTS = 32
TOP_K = 4
D_FF = 1024
SWIGLU_LIMIT = 7.0
SWIGLU_ALPHA = 1.702
DN_ALPHA = 2.0 ** 0.25
LN_EPS = 1e-5
NEG_INF = -1e30

V7X_LANES = 128
V7X_VMEM_BYTES = 64 * 1024 * 1024
VMEM_LIMIT = V7X_VMEM_BYTES - 8 * 1024 * 1024

ROW_TILE = 512
WINDOW = BAND + CHUNK
EXPERT_TILE = 512
DMA_WINDOW = 64


def _params(*sem):
    return pltpu.CompilerParams(dimension_semantics=sem, vmem_limit_bytes=VMEM_LIMIT)


def _resident(shape):
    nd = len(shape)
    return pl.BlockSpec(shape, lambda *_: (0,) * nd, pipeline_mode=pl.Buffered(1))


def _layer_norm(r, g, b):
    mu = jnp.mean(r, axis=-1, keepdims=True)
    d = r - mu
    var = jnp.mean(d * d, axis=-1, keepdims=True)
    return d * lax.rsqrt(var + LN_EPS) * g + b


def _memkv_kernel(x_ref, w_ref, kf_ref, vf_ref, kb_ref, vb_ref):
    y = jnp.dot(x_ref[...].astype(BF16), w_ref[...], preferred_element_type=F32)
    k = y[:, :WIDTH]
    v = y[:, WIDTH:]
    kf_ref[...] = k
    vf_ref[...] = v
    kb_ref[...] = k.astype(BF16)
    vb_ref[...] = v.astype(BF16)


def _memory_kv(mem2d, w_b):
    rows = mem2d.shape[0]
    tile = pl.BlockSpec((ROW_TILE, D_MODEL), lambda i: (i, 0))
    half = pl.BlockSpec((ROW_TILE, WIDTH), lambda i: (i, 0))
    return pl.pallas_call(
        _memkv_kernel,
        grid=(rows // ROW_TILE,),
        in_specs=[tile, _resident((D_MODEL, 2 * WIDTH))],
        out_specs=[half, half, half, half],
        out_shape=[jax.ShapeDtypeStruct((rows, WIDTH), F32)] * 2
        + [jax.ShapeDtypeStruct((rows, WIDTH), BF16)] * 2,
        compiler_params=_params("arbitrary"),
        name="memory_kv",
    )(mem2d, w_b)


def _proj_in_kernel(x_ref, w_ref, cw_ref, cinit_ref,
                    q_ref, k_ref, v_ref, c_ref, qm_ref, kt_ref, vt_ref, ut_ref,
                    carry_ref, *, pad_steps):
    s = pl.program_id(1)
    ns = pl.num_programs(1)
    rows = x_ref.shape[0]
    first = pad_steps
    last = ns - 1 - pad_steps

    if pad_steps:
        @pl.when((s < first) | (s > last))
        def _():
            k_ref[...] = jnp.zeros(k_ref.shape, k_ref.dtype)
            v_ref[...] = jnp.zeros(v_ref.shape, v_ref.dtype)

    @pl.when((s >= first) & (s <= last))
    def _():
        @pl.when(s == first)
        def _():
            carry_ref[...] = cinit_ref[0]

        xb = x_ref[...].astype(BF16)

        def proj(g):
            return jnp.dot(xb, w_ref[:, g * WIDTH:(g + 1) * WIDTH], preferred_element_type=F32)

        q_ref[...] = (proj(0) * (ATT_DIM ** -0.5)).astype(BF16)
        k = proj(1)
        v = proj(2)
        k_ref[0] = k.astype(BF16)
        v_ref[0] = v.astype(BF16)

        @pl.when(s == last)
        def _():
            kt_ref[0] = k
            vt_ref[0] = v

        bg = proj(3)
        u = proj(4) * proj(5)
        prev = carry_ref[...]
        row = lax.broadcasted_iota(I32, u.shape, 0)
        u1 = jnp.where(row == 0, prev[7:8], pltpu.roll(u, 1, 0))
        u2 = jnp.where(row == 0, prev[6:7], jnp.where(row == 1, prev[7:8], pltpu.roll(u, 2, 0)))
        cw = cw_ref[...]
        c_ref[...] = (bg * (cw[0:1] * u2 + cw[1:2] * u1 + cw[2:3] * u)).astype(BF16)
        tail = u[rows - 8:]
        carry_ref[...] = tail

        @pl.when(s == last)
        def _():
            ut_ref[0] = tail

        qm_ref[...] = proj(6).astype(BF16)


def _proj_in(x2d, w_b, conv_w8, conv_init, *, batch, seq, rows, pad_steps):
    n_data = seq // rows
    n_steps = n_data + 2 * pad_steps

    def data_idx(b, s):
        return (b * n_data + jnp.clip(s - pad_steps, 0, n_data - 1), 0)

    wide = pl.BlockSpec((rows, D_MODEL), data_idx)
    narrow = pl.BlockSpec((rows, WIDTH), data_idx)
    seq_blk = pl.BlockSpec((1, rows, WIDTH), lambda b, s: (b, s, 0))
    tail_blk = pl.BlockSpec((1, rows, WIDTH), lambda b, s: (b, 0, 0))
    tail8 = pl.BlockSpec((1, 8, WIDTH), lambda b, s: (b, 0, 0))
    tok = jax.ShapeDtypeStruct((batch * seq, WIDTH), BF16)
    kv = jax.ShapeDtypeStruct((batch, n_steps * rows, WIDTH), BF16)
    tail = jax.ShapeDtypeStruct((batch, rows, WIDTH), F32)
    return pl.pallas_call(
        functools.partial(_proj_in_kernel, pad_steps=pad_steps),
        grid=(batch, n_steps),
        in_specs=[wide, _resident(w_b.shape), _resident(conv_w8.shape), tail8],
        out_specs=[narrow, seq_blk, seq_blk, narrow, narrow, tail_blk, tail_blk, tail8],
        out_shape=[tok, kv, kv, tok, tok, tail, tail,
                   jax.ShapeDtypeStruct((batch, 8, WIDTH), F32)],
        scratch_shapes=[pltpu.VMEM((8, WIDTH), F32)],
        compiler_params=_params("arbitrary", "arbitrary"),
        name="proj_in",
    )(x2d, w_b, conv_w8, conv_init)


def _bias_kernel(tab_ref, out_ref):
    shape = (CHUNK, WINDOW)
    i = lax.broadcasted_iota(I32, shape, 0)
    j = lax.broadcasted_iota(I32, shape, 1)
    idx = jnp.clip(HIST + i - j, -REL_MAX, REL_MAX) + REL_MAX
    for h in range(ATT_HEADS):
        def body(d, acc, h=h):
            return jnp.where(idx == d, tab_ref[h, d], acc)
        out_ref[h] = lax.fori_loop(0, 2 * REL_MAX + 1, body, jnp.zeros(shape, F32))


def _band_bias(rel_table):
    return pl.pallas_call(
        _bias_kernel,
        in_specs=[pl.BlockSpec(memory_space=pltpu.SMEM)],
        out_shape=jax.ShapeDtypeStruct((ATT_HEADS, CHUNK, WINDOW), F32),
        name="band_bias",
    )(rel_table)


def _softmax_pv(sc, v):
    m = jnp.max(sc, axis=-1, keepdims=True)
    p = jnp.exp(sc - m)
    l = jnp.sum(p, axis=-1, keepdims=True)
    return jnp.dot(p.astype(BF16), v, preferred_element_type=F32) / l


def _attention_kernel(q_ref, k_ref, v_ref, bias_ref, qm_ref, mk_ref, mv_ref,
                      a_ref, m_ref, *, chunk_rows, n_chunks, valid_keys, first_chunk, mem_rows):
    s = pl.program_id(1)
    shape = (chunk_rows, WINDOW)
    col = lax.broadcasted_iota(I32, shape, 1)
    pair_lane = lax.broadcasted_iota(I32, (chunk_rows, 2 * ATT_DIM), 1)

    def chunk(c, carry):
        g = first_chunk + s * n_chunks + c
        r0 = pl.multiple_of(c * chunk_rows, chunk_rows)
        w0 = pl.multiple_of((s * n_chunks + c) * CHUNK, CHUNK)
        valid = (col >= HIST - CHUNK * g) & (col < valid_keys)
        qc = q_ref[pl.ds(r0, chunk_rows), :].astype(F32)
        kw = k_ref[0, pl.ds(w0, WINDOW), :]
        vw = v_ref[0, pl.ds(w0, WINDOW), :]
        outs = []
        for pair in range(ATT_HEADS // 2):
            lanes = slice(pair * 2 * ATT_DIM, (pair + 1) * 2 * ATT_DIM)
            qp, kp, vp = qc[:, lanes], kw[:, lanes], vw[:, lanes]
            o_pair = None
            for half in range(2):
                h = 2 * pair + half
                mine = (pair_lane >= half * ATT_DIM) & (pair_lane < (half + 1) * ATT_DIM)
                qh = jnp.where(mine, qp, 0.0).astype(BF16)
                sc = lax.dot_general(qh, kp, (((1,), (1,)), ((), ())), preferred_element_type=F32)
                sc = jnp.where(valid, sc + bias_ref[h], NEG_INF)
                o = _softmax_pv(sc, vp)
                o_pair = o if o_pair is None else jnp.where(mine, o, o_pair)
            outs.append(o_pair)
        a_ref[pl.ds(r0, chunk_rows), :] = jnp.concatenate(outs, axis=1).astype(BF16)
        return carry

    lax.fori_loop(0, n_chunks, chunk, 0)

    rows = qm_ref.shape[0]
    for rb in range(rows // mem_rows):
        rs = slice(rb * mem_rows, (rb + 1) * mem_rows)
        for h in range(MEM_HEADS):
            lanes = slice(h * MEM_DIM, (h + 1) * MEM_DIM)
            sc = lax.dot_general(qm_ref[rs, lanes], mk_ref[0, :, lanes],
                                 (((1,), (1,)), ((), ())), preferred_element_type=F32)
            o = _softmax_pv(sc * (MEM_DIM ** -0.5), mv_ref[0, :, lanes])
            m_ref[rs, lanes] = o.astype(BF16)


def _attention(q, k_pad, v_pad, bias, qm, mk, mv, *, batch, seq, rows, chunk_rows,
               valid_keys, first_chunk, mem_rows):
    n_steps = seq // rows
    tok = pl.BlockSpec((rows, WIDTH), lambda b, s: (b * n_steps + s, 0))
    whole_seq = pl.BlockSpec((1, k_pad.shape[1], WIDTH), lambda b, s: (b, 0, 0))
    mem = pl.BlockSpec((1, MEM_TOKENS, WIDTH), lambda b, s: (b, 0, 0))
    kern = functools.partial(
        _attention_kernel, chunk_rows=chunk_rows, n_chunks=rows // chunk_rows,
        valid_keys=valid_keys, first_chunk=first_chunk, mem_rows=mem_rows)
    out = jax.ShapeDtypeStruct((batch * seq, WIDTH), BF16)
    return pl.pallas_call(
        kern,
        grid=(batch, n_steps),
        in_specs=[tok, whole_seq, whole_seq, _resident(bias.shape), tok, mem, mem],
        out_specs=[tok, tok],
        out_shape=[out, out],
        compiler_params=_params("arbitrary", "arbitrary"),
        name="attention",
    )(q, k_pad, v_pad, bias, qm, mk, mv)


def _merge_kernel(x_ref, a_ref, c_ref, m_ref, wg_ref, wa_ref, wc_ref, wm_ref, wo_ref,
                  g_ref, b_ref, wr2_ref, wrh_ref, br_ref,
                  y_ref, sel_ref, topi_ref, topg_ref):
    x = x_ref[...]
    xb = x.astype(BF16)
    comb = None
    for i, (br, wbr) in enumerate(((a_ref, wa_ref), (c_ref, wc_ref), (m_ref, wm_ref))):
        gate = jax.nn.sigmoid(jnp.dot(xb, wg_ref[:, i * D_MODEL:(i + 1) * D_MODEL],
                                      preferred_element_type=F32))
        term = gate * jnp.dot(br[...], wbr[...], preferred_element_type=F32)
        comb = term if comb is None else comb + term
    out = jnp.dot(comb.astype(BF16), wo_ref[...], preferred_element_type=F32)
    y = _layer_norm(DN_ALPHA * x + out, g_ref[...], b_ref[...])
    y_ref[...] = y

    y_hi = y.astype(BF16)
    y_lo = (y - y_hi.astype(F32)).astype(BF16)
    both = jnp.dot(y_hi, wr2_ref[...], preferred_element_type=F32)
    logits = (both[:, :N_EXPERTS] + both[:, N_EXPERTS:]
              + jnp.dot(y_lo, wrh_ref[...], preferred_element_type=F32) + br_ref[...])

    rows = logits.shape[0]
    lane = lax.broadcasted_iota(I32, (rows, N_EXPERTS), 1).astype(F32)
    lane_out = lax.broadcasted_iota(I32, (rows, V7X_LANES), 1)
    work = logits
    sel = jnp.zeros((rows, N_EXPERTS), F32)
    topi = jnp.zeros((rows, V7X_LANES), I32)
    vals = []
    for k in range(TOP_K):
        top = jnp.max(work, axis=-1, keepdims=True)
        idx = jnp.min(jnp.where(work == top, lane, float(N_EXPERTS)), axis=-1, keepdims=True)
        hit = lane == idx
        sel = jnp.where(hit, 1.0, sel)
        work = jnp.where(hit, -jnp.inf, work)
        topi = jnp.where(lane_out == k, idx.astype(I32), topi)
        vals.append(top)
    exps = [jnp.exp(v - vals[0]) for v in vals]
    denom = exps[0] + exps[1] + exps[2] + exps[3]
    topg = jnp.zeros((rows, V7X_LANES), F32)
    for k in range(TOP_K):
        topg = jnp.where(lane_out == k, exps[k] / denom, topg)
    sel_ref[...] = sel.astype(BF16)
    topi_ref[...] = topi
    topg_ref[...] = topg


def _merge(x2d, a, c, m, weights, *, rows):
    n = x2d.shape[0]
    wide = pl.BlockSpec((rows, D_MODEL), lambda i: (i, 0))
    narrow = pl.BlockSpec((rows, WIDTH), lambda i: (i, 0))
    lanes = pl.BlockSpec((rows, V7X_LANES), lambda i: (i, 0))
    return pl.pallas_call(
        _merge_kernel,
        grid=(n // rows,),
        in_specs=[wide, narrow, narrow, narrow] + [_resident(w.shape) for w in weights],
        out_specs=[wide, pl.BlockSpec((rows, N_EXPERTS), lambda i: (i, 0)), lanes, lanes],
        out_shape=[jax.ShapeDtypeStruct((n, D_MODEL), F32),
                   jax.ShapeDtypeStruct((n, N_EXPERTS), BF16),
                   jax.ShapeDtypeStruct((n, V7X_LANES), I32),
                   jax.ShapeDtypeStruct((n, V7X_LANES), F32)],
        compiler_params=_params("arbitrary"),
        name="merge_router",
    )(x2d, a, c, m, *weights)


def _rank_kernel(sel_ref, topi_ref, init_ref, rank_ref, cnt_ref, carry_ref):
    @pl.when(pl.program_id(0) == 0)
    def _():
        carry_ref[...] = init_ref[...]

    sel = sel_ref[...]
    rows = sel.shape[0]
    r = lax.broadcasted_iota(I32, (rows, rows), 0)
    c = lax.broadcasted_iota(I32, (rows, rows), 1)
    before = jnp.where(c < r, 1.0, 0.0).astype(BF16)
    counts = jnp.dot(before, sel, preferred_element_type=F32) + carry_ref[0:1]
    lane = lax.broadcasted_iota(I32, (rows, N_EXPERTS), 1)
    lane_out = lax.broadcasted_iota(I32, (rows, V7X_LANES), 1)
    topi = topi_ref[...]
    rank = jnp.zeros((rows, V7X_LANES), I32)
    for k in range(TOP_K):
        mine = jnp.sum(jnp.where(lane == topi[:, k:k + 1], counts, 0.0), axis=-1, keepdims=True)
        rank = jnp.where(lane_out == k, mine.astype(I32), rank)
    rank_ref[...] = rank
    total = carry_ref[...] + jnp.sum(sel.astype(F32), axis=0, keepdims=True)
    carry_ref[...] = total
    cnt_ref[...] = total


def _ranks(sel, topi, init, *, rows):
    n = sel.shape[0]
    small = pl.BlockSpec((8, N_EXPERTS), lambda i: (0, 0))
    lanes = pl.BlockSpec((rows, V7X_LANES), lambda i: (i, 0))
    return pl.pallas_call(
        _rank_kernel,
        grid=(n // rows,),
        in_specs=[pl.BlockSpec((rows, N_EXPERTS), lambda i: (i, 0)), lanes, small],
        out_specs=[lanes, small],
        out_shape=[jax.ShapeDtypeStruct((n, V7X_LANES), I32),
                   jax.ShapeDtypeStruct((8, N_EXPERTS), F32)],
        scratch_shapes=[pltpu.VMEM((8, N_EXPERTS), F32)],
        compiler_params=_params("arbitrary"),
        name="expert_ranks",
    )(sel, topi, init)


def _row_copy(src_ref, src_row, dst_ref, dst_row, sem):
    return pltpu.make_async_copy(src_ref.at[pl.ds(src_row, 1)], dst_ref.at[pl.ds(dst_row, 1)], sem)


def _scatter_kernel(dest_ref, src_ref, alias_ref, dst_ref, sem, *, rows):
    del alias_ref
    base = pl.program_id(0) * rows

    def wait_token():
        for _ in range(TOP_K):
            _row_copy(src_ref, 0, dst_ref, 0, sem).wait()

    def step(t, carry):
        @pl.when(t >= DMA_WINDOW)
        def _():
            wait_token()
        for k in range(TOP_K):
            _row_copy(src_ref, base + t, dst_ref, dest_ref[0, 0, t * TOP_K + k], sem).start()
        return carry

    lax.fori_loop(0, rows, step, 0)

    def drain(t, carry):
        wait_token()
        return carry

    lax.fori_loop(0, min(rows, DMA_WINDOW), drain, 0)


def _scatter_rows(dest, src, sorted_rows, *, rows):
    n = src.shape[0]
    any_spec = pl.BlockSpec(memory_space=pl.ANY)
    return pl.pallas_call(
        functools.partial(_scatter_kernel, rows=rows),
        grid=(n // rows,),
        in_specs=[pl.BlockSpec((1, 1, rows * TOP_K), lambda i: (i, 0, 0), memory_space=pltpu.SMEM),
                  any_spec, any_spec],
        out_specs=any_spec,
        out_shape=jax.ShapeDtypeStruct(sorted_rows.shape, sorted_rows.dtype),
        scratch_shapes=[pltpu.SemaphoreType.DMA],
        input_output_aliases={2: 0},
        compiler_params=_params("arbitrary"),
        name="scatter_rows",
    )(dest, src, sorted_rows)


def _expert_kernel(te_ref, nu_ref, x_ref, wgu_ref, bgu_ref, wd_ref, bd_ref, o_ref,
                   wgu_b, wd_b):
    i = pl.program_id(0)
    e = te_ref[i]
    changed = (i == 0) | (e != te_ref[jnp.maximum(i - 1, 0)])

    @pl.when(changed)
    def _():
        wgu_b[...] = wgu_ref[0].astype(BF16)
        wd_b[...] = wd_ref[0].astype(BF16)

    @pl.when(i >= nu_ref[0])
    def _():
        o_ref[...] = jnp.zeros(o_ref.shape, o_ref.dtype)

    @pl.when(i < nu_ref[0])
    def _():
        xb = x_ref[...].astype(BF16)
        bgu = bgu_ref[0]
        acc = None
        n_col = 256
        for j in range(D_FF // n_col):
            cg = slice(j * n_col, (j + 1) * n_col)
            cl = slice(D_FF + j * n_col, D_FF + (j + 1) * n_col)
            hg = jnp.dot(xb, wgu_b[:, cg], preferred_element_type=F32) + bgu[:, cg]
            hl = jnp.dot(xb, wgu_b[:, cl], preferred_element_type=F32) + bgu[:, cl]
            glu = jnp.minimum(hg, SWIGLU_LIMIT)
            lin = jnp.clip(hl, -SWIGLU_LIMIT, SWIGLU_LIMIT)
            act = glu * jax.nn.sigmoid(SWIGLU_ALPHA * glu) * (lin + 1.0)
            part = jnp.dot(act.astype(BF16), wd_b[cg, :], preferred_element_type=F32)
            acc = part if acc is None else acc + part
        o_ref[...] = acc + bd_ref[0]


def _experts(tile_expert, n_used, xs, w_gate_up, b_gate_up, w_down, b_down):
    n_tiles = xs.shape[0] // EXPERT_TILE
    grid_spec = pltpu.PrefetchScalarGridSpec(
        num_scalar_prefetch=2,
        grid=(n_tiles,),
        in_specs=[
            pl.BlockSpec((EXPERT_TILE, D_MODEL), lambda i, te, nu: (i, 0)),
            pl.BlockSpec((1, D_MODEL, 2 * D_FF), lambda i, te, nu: (te[i], 0, 0)),
            pl.BlockSpec((1, 1, 2 * D_FF), lambda i, te, nu: (te[i], 0, 0)),
            pl.BlockSpec((1, D_FF, D_MODEL), lambda i, te, nu: (te[i], 0, 0)),
            pl.BlockSpec((1, 1, D_MODEL), lambda i, te, nu: (te[i], 0, 0)),
        ],
        out_specs=pl.BlockSpec((EXPERT_TILE, D_MODEL), lambda i, te, nu: (i, 0)),
        scratch_shapes=[pltpu.VMEM((D_MODEL, 2 * D_FF), BF16), pltpu.VMEM((D_FF, D_MODEL), BF16)],
    )
    return pl.pallas_call(
        _expert_kernel,
        grid_spec=grid_spec,
        out_shape=jax.ShapeDtypeStruct(xs.shape, F32),
        compiler_params=_params("arbitrary"),
        name="expert_ffn",
    )(tile_expert, n_used, xs, w_gate_up, b_gate_up, w_down, b_down)


def _combine_kernel(dest_ref, gate_ref, y_ref, g_ref, b_ref, eo_ref, out_ref, buf, sem, *, rows):
    def copy(t, k, src_row):
        return pltpu.make_async_copy(eo_ref.at[pl.ds(src_row, 1)], buf.at[k, pl.ds(t, 1)], sem)

    def issue(t, carry):
        for k in range(TOP_K):
            copy(t, k, dest_ref[0, 0, t * TOP_K + k]).start()
        return carry

    lax.fori_loop(0, rows, issue, 0)

    def drain(t, carry):
        for k in range(TOP_K):
            copy(t, k, 0).wait()
        return carry

    lax.fori_loop(0, rows, drain, 0)

    gate = gate_ref[...]
    moe = gate[:, 0:1] * buf[0]
    for k in range(1, TOP_K):
        moe = moe + gate[:, k:k + 1] * buf[k]
    out_ref[...] = _layer_norm(DN_ALPHA * y_ref[...] + moe, g_ref[...], b_ref[...])


def _combine(dest, gates, y1, ln_g, ln_b, expert_out, *, rows):
    n = y1.shape[0]
    wide = pl.BlockSpec((rows, D_MODEL), lambda i: (i, 0))
    return pl.pallas_call(
        functools.partial(_combine_kernel, rows=rows),
        grid=(n // rows,),
        in_specs=[pl.BlockSpec((1, 1, rows * TOP_K), lambda i: (i, 0, 0), memory_space=pltpu.SMEM),
                  pl.BlockSpec((rows, V7X_LANES), lambda i: (i, 0)), wide,
                  _resident(ln_g.shape), _resident(ln_b.shape),
                  pl.BlockSpec(memory_space=pl.ANY)],
        out_specs=wide,
        out_shape=jax.ShapeDtypeStruct((n, D_MODEL), F32),
        scratch_shapes=[pltpu.VMEM((TOP_K, rows, D_MODEL), F32), pltpu.SemaphoreType.DMA],
        compiler_params=_params("arbitrary"),
        name="combine_norm",
    )(dest, gates, y1, ln_g, ln_b, expert_out)


def _dest_blocks(dest, rows):
    return dest.reshape(dest.shape[0] // rows, 1, rows * TOP_K)


def kernel(x_prompt, x_sample, mem_prompt, cache_attn_k, cache_attn_v, cache_conv, cache_mem_k,
           cache_mem_v, w_in, rel_bias, conv_w, w_mem_kv, w_gate, w_br_attn, w_br_conv, w_br_mem,
           w_out, ln1_g, ln1_b, w_router, b_router, w_gate_up, b_gate_up, w_down, b_down,
           ln2_g, ln2_b):
    depth = w_in.shape[0]
    assert depth == 1, "one layer only"
    batch, seq, _ = x_prompt.shape
    dec_batch, dec_seq, _ = x_sample.shape
    n_prompt = batch * seq
    n_sample = dec_batch * dec_seq
    assert seq % ROW_TILE == 0 and HIST == ROW_TILE and cache_attn_k.shape[2] == HIST
    assert dec_seq % 16 == 0 and dec_seq <= CHUNK and cache_conv.shape[2] == 2

    w_in_b = w_in[0].astype(BF16)
    w_mem_b = w_mem_kv[0].astype(BF16)
    wr = w_router[0]
    wr_hi = wr.astype(BF16)
    wr_lo = (wr - wr_hi.astype(F32)).astype(BF16)
    merge_w = (w_gate[0].astype(BF16), w_br_attn[0].astype(BF16), w_br_conv[0].astype(BF16),
               w_br_mem[0].astype(BF16), w_out[0].astype(BF16), ln1_g, ln1_b,
               jnp.concatenate([wr_hi, wr_lo], axis=1), wr_hi, b_router)
    conv_w8 = jnp.pad(conv_w[0], ((0, 5), (0, 0)))
    bias = _band_bias(rel_bias[0])

    xp = x_prompt.reshape(n_prompt, D_MODEL)
    mk_f, mv_f, mk_b, mv_b = _memory_kv(mem_prompt.reshape(batch * MEM_TOKENS, D_MODEL), w_mem_b)
    q, k_pad, v_pad, c, qm, k_tail, v_tail, u_tail = _proj_in(
        xp, w_in_b, conv_w8, jnp.zeros((batch, 8, WIDTH), F32),
        batch=batch, seq=seq, rows=ROW_TILE, pad_steps=1)
    a, m = _attention(
        q, k_pad, v_pad, bias, qm, mk_b.reshape(batch, MEM_TOKENS, WIDTH),
        mv_b.reshape(batch, MEM_TOKENS, WIDTH), batch=batch, seq=seq, rows=ROW_TILE,
        chunk_rows=CHUNK, valid_keys=BAND, first_chunk=0, mem_rows=128)
    y1_p, sel_p, topi_p, topg_p = _merge(xp, a, c, m, merge_w, rows=ROW_TILE)

    xs = x_sample.reshape(n_sample, D_MODEL)
    conv_init = jnp.pad(cache_conv[0], ((0, 0), (6, 0), (0, 0)))
    q_s, k_s, v_s, c_s, qm_s, k_new, v_new, u_tail_s = _proj_in(
        xs, w_in_b, conv_w8, conv_init, batch=dec_batch, seq=dec_seq, rows=dec_seq, pad_steps=0)
    pad_rows = WINDOW - HIST - dec_seq

    def window(cache, new):
        cache = cache[0].reshape(dec_batch, HIST, WIDTH).astype(BF16)
        return jnp.pad(jnp.concatenate([cache, new], axis=1), ((0, 0), (0, pad_rows), (0, 0)))

    a_s, m_s = _attention(
        q_s, window(cache_attn_k, k_s), window(cache_attn_v, v_s), bias[:, :dec_seq], qm_s,
        cache_mem_k[0].reshape(dec_batch, MEM_TOKENS, WIDTH).astype(BF16),
        cache_mem_v[0].reshape(dec_batch, MEM_TOKENS, WIDTH).astype(BF16),
        batch=dec_batch, seq=dec_seq, rows=dec_seq, chunk_rows=dec_seq,
        valid_keys=HIST + dec_seq, first_chunk=LEFT_CHUNKS, mem_rows=dec_seq)
    y1_s, sel_s, topi_s, topg_s = _merge(xs, a_s, c_s, m_s, merge_w, rows=n_sample)

    rank_p, cnt_p = _ranks(sel_p, topi_p, jnp.zeros((8, N_EXPERTS), F32), rows=ROW_TILE)
    rank_s, cnt = _ranks(sel_s, topi_s, cnt_p, rows=n_sample)
    counts = cnt[0].astype(I32)
    padded = (counts + EXPERT_TILE - 1) // EXPERT_TILE * EXPERT_TILE
    ends = jnp.cumsum(padded)
    starts = ends - padded
    n_tiles = (n_prompt + n_sample) * TOP_K // EXPERT_TILE + N_EXPERTS
    tile_expert = jnp.minimum(
        jnp.searchsorted(ends, jnp.arange(n_tiles, dtype=I32) * EXPERT_TILE, side="right"),
        N_EXPERTS - 1).astype(I32)
    n_used = (ends[-1:] // EXPERT_TILE).astype(I32)
    dest_p = starts[topi_p[:, :TOP_K]] + rank_p[:, :TOP_K]
    dest_s = starts[topi_s[:, :TOP_K]] + rank_s[:, :TOP_K]

    sorted_rows = jnp.zeros((n_tiles * EXPERT_TILE, D_MODEL), F32)
    sorted_rows = _scatter_rows(_dest_blocks(dest_p, ROW_TILE), y1_p, sorted_rows, rows=ROW_TILE)
    sorted_rows = _scatter_rows(_dest_blocks(dest_s, n_sample), y1_s, sorted_rows, rows=n_sample)
    expert_out = _experts(tile_expert, n_used, sorted_rows, w_gate_up[0],
                          b_gate_up[0].reshape(N_EXPERTS, 1, 2 * D_FF), w_down[0],
                          b_down[0].reshape(N_EXPERTS, 1, D_MODEL))
    y_p = _combine(_dest_blocks(dest_p, 256), topg_p, y1_p, ln2_g, ln2_b, expert_out, rows=256)
    y_s = _combine(_dest_blocks(dest_s, n_sample), topg_s, y1_s, ln2_g, ln2_b, expert_out,
                   rows=n_sample)

    def heads(t, n_b, n_rows, n_heads, dim):
        return t.reshape(1, n_b, n_rows, n_heads, dim)

    return (y_p.reshape(batch, seq, D_MODEL),
            y_s.reshape(dec_batch, dec_seq, D_MODEL),
            heads(k_tail, batch, HIST, ATT_HEADS, ATT_DIM),
            heads(v_tail, batch, HIST, ATT_HEADS, ATT_DIM),
            u_tail[:, 6:8].reshape(1, batch, 2, WIDTH),
            heads(mk_f, batch, MEM_TOKENS, MEM_HEADS, MEM_DIM),
            heads(mv_f, batch, MEM_TOKENS, MEM_HEADS, MEM_DIM),
            heads(k_new, dec_batch, dec_seq, ATT_HEADS, ATT_DIM),
            heads(v_new, dec_batch, dec_seq, ATT_HEADS, ATT_DIM),
            u_tail_s[:, 6:8].reshape(1, dec_batch, 2, WIDTH))
```

```python
import functools

import jax
import jax.numpy as jnp
from jax import lax
from jax.experimental import pallas as pl
from jax.experimental.pallas import tpu as pltpu

F32 = jnp.float32
BF16 = jnp.bfloat16
I32 = jnp.int32

D_MODEL = 1024
CHUNK = 64
LEFT_CHUNKS = 8
HIST = LEFT_CHUNKS * CHUNK
BAND = HIST + CHUNK
ATT_HEADS = 8
ATT_DIM = 64
REL_MAX = 128
WIDTH = 512
MEM_TOKENS = 256
MEM_HEADS = 4
MEM_DIM = 128
N_EXPERTS = 32
TOP_K = 4
D_FF = 1024
SWIGLU_LIMIT = 7.0
SWIGLU_ALPHA = 1.702
DN_ALPHA = 2.0 ** 0.25
LN_EPS = 1e-5
NEG_INF = -1e30

V7X_LANES = 128
V7X_VMEM_BYTES = 64 * 1024 * 1024
VMEM_LIMIT = V7X_VMEM_BYTES - 8 * 1024 * 1024

ROW_TILE = 512
WINDOW = BAND + CHUNK
EXPERT_TILE = 512


def _params(*sem):
    return pltpu.CompilerParams(dimension_semantics=sem, vmem_limit_bytes=VMEM_LIMIT)


def _resident(shape):
    nd = len(shape)
    return pl.BlockSpec(shape, lambda *_: (0,) * nd, pipeline_mode=pl.Buffered(1))


def _layer_norm(r, g, b):
    mu = jnp.mean(r, axis=-1, keepdims=True)
    d = r - mu
    var = jnp.mean(d * d, axis=-1, keepdims=True)
    return d * lax.rsqrt(var + LN_EPS) * g + b


def _memkv_kernel(x_ref, w_ref, kf_ref, vf_ref, kb_ref, vb_ref):
    y = jnp.dot(x_ref[...].astype(BF16), w_ref[...], preferred_element_type=F32)
    k = y[:, :WIDTH]
    v = y[:, WIDTH:]
    kf_ref[...] = k
    vf_ref[...] = v
    kb_ref[...] = k.astype(BF16)
    vb_ref[...] = v.astype(BF16)


def _memory_kv(mem2d, w_b):
    rows = mem2d.shape[0]
    tile = pl.BlockSpec((ROW_TILE, D_MODEL), lambda i: (i, 0))
    half = pl.BlockSpec((ROW_TILE, WIDTH), lambda i: (i, 0))
    return pl.pallas_call(
        _memkv_kernel,
        grid=(rows // ROW_TILE,),
        in_specs=[tile, _resident((D_MODEL, 2 * WIDTH))],
        out_specs=[half, half, half, half],
        out_shape=[jax.ShapeDtypeStruct((rows, WIDTH), F32)] * 2
        + [jax.ShapeDtypeStruct((rows, WIDTH), BF16)] * 2,
        compiler_params=_params("arbitrary"),
        name="memory_kv",
    )(mem2d, w_b)


def _proj_in_kernel(x_ref, w_ref, cw_ref, cinit_ref,
                    q_ref, k_ref, v_ref, c_ref, qm_ref, kt_ref, vt_ref, ut_ref,
                    carry_ref, *, pad_steps):
    s = pl.program_id(1)
    ns = pl.num_programs(1)
    rows = x_ref.shape[0]
    first = pad_steps
    last = ns - 1 - pad_steps

    if pad_steps:
        @pl.when((s < first) | (s > last))
        def _():
            k_ref[...] = jnp.zeros(k_ref.shape, k_ref.dtype)
            v_ref[...] = jnp.zeros(v_ref.shape, v_ref.dtype)

    @pl.when((s >= first) & (s <= last))
    def _():
        @pl.when(s == first)
        def _():
            carry_ref[...] = cinit_ref[0]

        xb = x_ref[...].astype(BF16)

        def proj(g):
            return jnp.dot(xb, w_ref[:, g * WIDTH:(g + 1) * WIDTH], preferred_element_type=F32)

        q_ref[...] = (proj(0) * (ATT_DIM ** -0.5)).astype(BF16)
        k = proj(1)
        v = proj(2)
        k_ref[0] = k.astype(BF16)
        v_ref[0] = v.astype(BF16)

        @pl.when(s == last)
        def _():
            kt_ref[0] = k
            vt_ref[0] = v

        bg = proj(3)
        u = proj(4) * proj(5)
        prev = carry_ref[...]
        row = lax.broadcasted_iota(I32, u.shape, 0)
        u1 = jnp.where(row == 0, prev[7:8], pltpu.roll(u, 1, 0))
        u2 = jnp.where(row == 0, prev[6:7], jnp.where(row == 1, prev[7:8], pltpu.roll(u, 2, 0)))
        cw = cw_ref[...]
        c_ref[...] = (bg * (cw[0:1] * u2 + cw[1:2] * u1 + cw[2:3] * u)).astype(BF16)
        tail = u[rows - 8:]
        carry_ref[...] = tail

        @pl.when(s == last)
        def _():
            ut_ref[0] = tail

        qm_ref[...] = proj(6).astype(BF16)


def _proj_in(x2d, w_b, conv_w8, conv_init, *, batch, seq, rows, pad_steps):
    n_data = seq // rows
    n_steps = n_data + 2 * pad_steps

    def data_idx(b, s):
        return (b * n_data + jnp.clip(s - pad_steps, 0, n_data - 1), 0)

    wide = pl.BlockSpec((rows, D_MODEL), data_idx)
    narrow = pl.BlockSpec((rows, WIDTH), data_idx)
    seq_blk = pl.BlockSpec((1, rows, WIDTH), lambda b, s: (b, s, 0))
    tail_blk = pl.BlockSpec((1, rows, WIDTH), lambda b, s: (b, 0, 0))
    tail8 = pl.BlockSpec((1, 8, WIDTH), lambda b, s: (b, 0, 0))
    tok = jax.ShapeDtypeStruct((batch * seq, WIDTH), BF16)
    kv = jax.ShapeDtypeStruct((batch, n_steps * rows, WIDTH), BF16)
    tail = jax.ShapeDtypeStruct((batch, rows, WIDTH), F32)
    return pl.pallas_call(
        functools.partial(_proj_in_kernel, pad_steps=pad_steps),
        grid=(batch, n_steps),
        in_specs=[wide, _resident(w_b.shape), _resident(conv_w8.shape), tail8],
        out_specs=[narrow, seq_blk, seq_blk, narrow, narrow, tail_blk, tail_blk, tail8],
        out_shape=[tok, kv, kv, tok, tok, tail, tail,
                   jax.ShapeDtypeStruct((batch, 8, WIDTH), F32)],
        scratch_shapes=[pltpu.VMEM((8, WIDTH), F32)],
        compiler_params=_params("arbitrary", "arbitrary"),
        name="proj_in",
    )(x2d, w_b, conv_w8, conv_init)


def _bias_kernel(tab_ref, out_ref):
    shape = (CHUNK, WINDOW)
    i = lax.broadcasted_iota(I32, shape, 0)
    j = lax.broadcasted_iota(I32, shape, 1)
    idx = jnp.clip(HIST + i - j, -REL_MAX, REL_MAX) + REL_MAX
    for h in range(ATT_HEADS):
        def body(d, acc, h=h):
            return jnp.where(idx == d, tab_ref[h, d], acc)
        out_ref[h] = lax.fori_loop(0, 2 * REL_MAX + 1, body, jnp.zeros(shape, F32))


def _band_bias(rel_table):
    return pl.pallas_call(
        _bias_kernel,
        in_specs=[pl.BlockSpec(memory_space=pltpu.SMEM)],
        out_shape=jax.ShapeDtypeStruct((ATT_HEADS, CHUNK, WINDOW), F32),
        name="band_bias",
    )(rel_table)


def _softmax_pv(sc, v):
    m = jnp.max(sc, axis=-1, keepdims=True)
    p = jnp.exp(sc - m)
    l = jnp.sum(p, axis=-1, keepdims=True)
    return jnp.dot(p.astype(BF16), v, preferred_element_type=F32) / l


def _attention_kernel(q_ref, k_ref, v_ref, bias_ref, qm_ref, mk_ref, mv_ref,
                      a_ref, m_ref, *, chunk_rows, n_chunks, valid_keys, first_chunk, mem_rows):
    s = pl.program_id(1)
    nr = chunk_rows
    pair_w = 2 * ATT_DIM
    col = lax.broadcasted_iota(I32, (ATT_HEADS * nr, WINDOW), 1)
    lane2 = lax.broadcasted_iota(I32, (2 * nr, pair_w), 1)
    row2 = lax.broadcasted_iota(I32, (2 * nr, pair_w), 0)
    own = (lane2 >= ATT_DIM) == (row2 >= nr)
    low_lanes = lax.broadcasted_iota(I32, (nr, pair_w), 1) < ATT_DIM
    nt = (((1,), (1,)), ((), ()))

    def chunk(c, carry):
        g = first_chunk + s * n_chunks + c
        r0 = pl.multiple_of(c * nr, nr)
        w0 = pl.multiple_of((s * n_chunks + c) * CHUNK, CHUNK)
        qc = q_ref[pl.ds(r0, nr), :].astype(F32)
        kw = k_ref[0, pl.ds(w0, WINDOW), :]
        vw = v_ref[0, pl.ds(w0, WINDOW), :]
        scores = []
        for pair in range(ATT_HEADS // 2):
            lanes = slice(pair * pair_w, (pair + 1) * pair_w)
            qp = qc[:, lanes]
            q2 = jnp.where(own, jnp.concatenate([qp, qp], axis=0), 0.0).astype(BF16)
            scores.append(lax.dot_general(q2, kw[:, lanes], nt, preferred_element_type=F32))
        sc = jnp.concatenate(scores, axis=0) + bias_ref[...]
        sc = jnp.where((col >= HIST - CHUNK * g) & (col < valid_keys), sc, NEG_INF)
        m = jnp.max(sc, axis=-1, keepdims=True)
        p = jnp.exp(sc - m)
        inv = 1.0 / jnp.sum(p, axis=-1, keepdims=True)
        pb = p.astype(BF16)
        outs = []
        for pair in range(ATT_HEADS // 2):
            lanes = slice(pair * pair_w, (pair + 1) * pair_w)
            rows = slice(pair * 2 * nr, (pair + 1) * 2 * nr)
            o = jnp.dot(pb[rows], vw[:, lanes], preferred_element_type=F32) * inv[rows]
            outs.append(jnp.where(low_lanes, o[:nr], o[nr:]))
        a_ref[pl.ds(r0, nr), :] = jnp.concatenate(outs, axis=1).astype(BF16)
        return carry

    lax.fori_loop(0, n_chunks, chunk, 0)

    rows = qm_ref.shape[0]
    for rb in range(rows // mem_rows):
        rs = slice(rb * mem_rows, (rb + 1) * mem_rows)
        for h in range(MEM_HEADS):
            lanes = slice(h * MEM_DIM, (h + 1) * MEM_DIM)
            sc = lax.dot_general(qm_ref[rs, lanes], mk_ref[0, :, lanes],
                                 (((1,), (1,)), ((), ())), preferred_element_type=F32)
            o = _softmax_pv(sc * (MEM_DIM ** -0.5), mv_ref[0, :, lanes])
            m_ref[rs, lanes] = o.astype(BF16)


def _attention(q, k_pad, v_pad, bias, qm, mk, mv, *, batch, seq, rows, chunk_rows,
               valid_keys, first_chunk, mem_rows):
    n_steps = seq // rows
    tok = pl.BlockSpec((rows, WIDTH), lambda b, s: (b * n_steps + s, 0))
    whole_seq = pl.BlockSpec((1, k_pad.shape[1], WIDTH), lambda b, s: (b, 0, 0))
    mem = pl.BlockSpec((1, MEM_TOKENS, WIDTH), lambda b, s: (b, 0, 0))
    kern = functools.partial(
        _attention_kernel, chunk_rows=chunk_rows, n_chunks=rows // chunk_rows,
        valid_keys=valid_keys, first_chunk=first_chunk, mem_rows=mem_rows)
    out = jax.ShapeDtypeStruct((batch * seq, WIDTH), BF16)
    return pl.pallas_call(
        kern,
        grid=(batch, n_steps),
        in_specs=[tok, whole_seq, whole_seq, _resident(bias.shape), tok, mem, mem],
        out_specs=[tok, tok],
        out_shape=[out, out],
        compiler_params=_params("arbitrary", "arbitrary"),
        name="attention",
    )(q, k_pad, v_pad, bias, qm, mk, mv)


def _merge_kernel(x_ref, a_ref, c_ref, m_ref, wg_ref, wa_ref, wc_ref, wm_ref, wo_ref,
                  g_ref, b_ref, wr2_ref, wrh_ref, br_ref,
                  y_ref, sel_ref, topi_ref, topg_ref):
    x = x_ref[...]
    xb = x.astype(BF16)
    comb = None
    for i, (br, wbr) in enumerate(((a_ref, wa_ref), (c_ref, wc_ref), (m_ref, wm_ref))):
        gate = jax.nn.sigmoid(jnp.dot(xb, wg_ref[:, i * D_MODEL:(i + 1) * D_MODEL],
                                      preferred_element_type=F32))
        term = gate * jnp.dot(br[...], wbr[...], preferred_element_type=F32)
        comb = term if comb is None else comb + term
    out = jnp.dot(comb.astype(BF16), wo_ref[...], preferred_element_type=F32)
    y = _layer_norm(DN_ALPHA * x + out, g_ref[...], b_ref[...])
    y_ref[...] = y

    y_hi = y.astype(BF16)
    y_lo = (y - y_hi.astype(F32)).astype(BF16)
    both = jnp.dot(y_hi, wr2_ref[...], preferred_element_type=F32)
    logits = (both[:, :N_EXPERTS] + both[:, N_EXPERTS:]
              + jnp.dot(y_lo, wrh_ref[...], preferred_element_type=F32) + br_ref[...])

    rows = logits.shape[0]
    lane = lax.broadcasted_iota(I32, (rows, N_EXPERTS), 1).astype(F32)
    lane_out = lax.broadcasted_iota(I32, (rows, V7X_LANES), 1)
    work = logits
    sel = jnp.zeros((rows, N_EXPERTS), F32)
    topi = jnp.zeros((rows, V7X_LANES), I32)
    vals = []
    for k in range(TOP_K):
        top = jnp.max(work, axis=-1, keepdims=True)
        idx = jnp.min(jnp.where(work == top, lane, float(N_EXPERTS)), axis=-1, keepdims=True)
        hit = lane == idx
        sel = jnp.where(hit, 1.0, sel)
        work = jnp.where(hit, -jnp.inf, work)
        topi = jnp.where(lane_out == k, idx.astype(I32), topi)
        vals.append(top)
    exps = [jnp.exp(v - vals[0]) for v in vals]
    denom = exps[0] + exps[1] + exps[2] + exps[3]
    topg = jnp.zeros((rows, V7X_LANES), F32)
    for k in range(TOP_K):
        topg = jnp.where(lane_out == k, exps[k] / denom, topg)
    sel_ref[...] = sel.astype(BF16)
    topi_ref[...] = topi
    topg_ref[...] = topg


def _merge(x2d, a, c, m, weights, *, rows):
    n = x2d.shape[0]
    wide = pl.BlockSpec((rows, D_MODEL), lambda i: (i, 0))
    narrow = pl.BlockSpec((rows, WIDTH), lambda i: (i, 0))
    lanes = pl.BlockSpec((rows, V7X_LANES), lambda i: (i, 0))
    return pl.pallas_call(
        _merge_kernel,
        grid=(n // rows,),
        in_specs=[wide, narrow, narrow, narrow] + [_resident(w.shape) for w in weights],
        out_specs=[wide, pl.BlockSpec((rows, N_EXPERTS), lambda i: (i, 0)), lanes, lanes],
        out_shape=[jax.ShapeDtypeStruct((n, D_MODEL), F32),
                   jax.ShapeDtypeStruct((n, N_EXPERTS), BF16),
                   jax.ShapeDtypeStruct((n, V7X_LANES), I32),
                   jax.ShapeDtypeStruct((n, V7X_LANES), F32)],
        compiler_params=_params("arbitrary"),
        name="merge_router",
    )(x2d, a, c, m, *weights)


def _rank_kernel(sel_ref, topi_ref, init_ref, rank_ref, cnt_ref, carry_ref):
    @pl.when(pl.program_id(0) == 0)
    def _():
        carry_ref[...] = init_ref[...]

    sel = sel_ref[...]
    rows = sel.shape[0]
    r = lax.broadcasted_iota(I32, (rows, rows), 0)
    c = lax.broadcasted_iota(I32, (rows, rows), 1)
    before = jnp.where(c < r, 1.0, 0.0).astype(BF16)
    counts = jnp.dot(before, sel, preferred_element_type=F32) + carry_ref[0:1]
    lane = lax.broadcasted_iota(I32, (rows, N_EXPERTS), 1)
    lane_out = lax.broadcasted_iota(I32, (rows, V7X_LANES), 1)
    topi = topi_ref[...]
    rank = jnp.zeros((rows, V7X_LANES), I32)
    for k in range(TOP_K):
        mine = jnp.sum(jnp.where(lane == topi[:, k:k + 1], counts, 0.0), axis=-1, keepdims=True)
        rank = jnp.where(lane_out == k, mine.astype(I32), rank)
    rank_ref[...] = rank
    total = carry_ref[...] + jnp.sum(sel.astype(F32), axis=0, keepdims=True)
    carry_ref[...] = total
    cnt_ref[...] = total


def _ranks(sel, topi, init, *, rows):
    n = sel.shape[0]
    small = pl.BlockSpec((8, N_EXPERTS), lambda i: (0, 0))
    lanes = pl.BlockSpec((rows, V7X_LANES), lambda i: (i, 0))
    return pl.pallas_call(
        _rank_kernel,
        grid=(n // rows,),
        in_specs=[pl.BlockSpec((rows, N_EXPERTS), lambda i: (i, 0)), lanes, small],
        out_specs=[lanes, small],
        out_shape=[jax.ShapeDtypeStruct((n, V7X_LANES), I32),
                   jax.ShapeDtypeStruct((8, N_EXPERTS), F32)],
        scratch_shapes=[pltpu.VMEM((8, N_EXPERTS), F32)],
        compiler_params=_params("arbitrary"),
        name="expert_ranks",
    )(sel, topi, init)


def _scatter_kernel(dest_ref, src_ref, *rest, rows):
    dst_ref, sem = rest[-2:]

    def copy(t, dst_row):
        return pltpu.make_async_copy(src_ref.at[pl.ds(t, 1)], dst_ref.at[pl.ds(dst_row, 1)], sem)

    def issue(t, carry):
        for k in range(TOP_K):
            copy(t, dest_ref[0, 0, t * TOP_K + k]).start(priority=k % 2)
        return carry

    lax.fori_loop(0, rows, issue, 0)

    def drain(t, carry):
        for _ in range(TOP_K):
            copy(t, 0).wait()
        return carry

    lax.fori_loop(0, rows, drain, 0)


def _scatter_rows(dest, src, sorted_rows, *, rows, n_sorted):
    n = src.shape[0]
    any_spec = pl.BlockSpec(memory_space=pl.ANY)
    in_specs = [pl.BlockSpec((1, 1, rows * TOP_K), lambda i: (i, 0, 0), memory_space=pltpu.SMEM),
                pl.BlockSpec((rows, D_MODEL), lambda i: (i, 0))]
    args = [dest, src]
    aliases = {}
    if sorted_rows is not None:
        in_specs.append(any_spec)
        args.append(sorted_rows)
        aliases = {2: 0}
    return pl.pallas_call(
        functools.partial(_scatter_kernel, rows=rows),
        grid=(n // rows,),
        in_specs=in_specs,
        out_specs=any_spec,
        out_shape=jax.ShapeDtypeStruct((n_sorted, D_MODEL), F32),
        scratch_shapes=[pltpu.SemaphoreType.DMA],
        input_output_aliases=aliases,
        compiler_params=_params("arbitrary"),
        name="scatter_rows",
    )(*args)


def _expert_kernel(te_ref, tr_ref, x_ref, wgu_ref, bgu_ref, wd_ref, bd_ref, o_ref,
                   wgu_b, wd_b):
    i = pl.program_id(0)
    e = te_ref[i]
    changed = (i == 0) | (e != te_ref[jnp.maximum(i - 1, 0)])

    @pl.when(changed)
    def _():
        wgu_b[...] = wgu_ref[0].astype(BF16)
        wd_b[...] = wd_ref[0].astype(BF16)

    @pl.when(tr_ref[i] == 0)
    def _():
        o_ref[...] = jnp.zeros(o_ref.shape, o_ref.dtype)

    @pl.when(tr_ref[i] > 0)
    def _():
        row = lax.broadcasted_iota(I32, x_ref.shape, 0)
        xb = jnp.where(row < tr_ref[i], x_ref[...], 0.0).astype(BF16)
        bgu = bgu_ref[0]
        acc = None
        n_col = 256
        for j in range(D_FF // n_col):
            cg = slice(j * n_col, (j + 1) * n_col)
            cl = slice(D_FF + j * n_col, D_FF + (j + 1) * n_col)
            hg = jnp.dot(xb, wgu_b[:, cg], preferred_element_type=F32) + bgu[:, cg]
            hl = jnp.dot(xb, wgu_b[:, cl], preferred_element_type=F32) + bgu[:, cl]
            glu = jnp.minimum(hg, SWIGLU_LIMIT)
            lin = jnp.clip(hl, -SWIGLU_LIMIT, SWIGLU_LIMIT)
            act = glu * jax.nn.sigmoid(SWIGLU_ALPHA * glu) * (lin + 1.0)
            part = jnp.dot(act.astype(BF16), wd_b[cg, :], preferred_element_type=F32)
            acc = part if acc is None else acc + part
        o_ref[...] = acc + bd_ref[0]


def _experts(tile_expert, tile_rows, xs, w_gate_up, b_gate_up, w_down, b_down):
    n_tiles = xs.shape[0] // EXPERT_TILE
    grid_spec = pltpu.PrefetchScalarGridSpec(
        num_scalar_prefetch=2,
        grid=(n_tiles,),
        in_specs=[
            pl.BlockSpec((EXPERT_TILE, D_MODEL), lambda i, te, tr: (i, 0)),
            pl.BlockSpec((1, D_MODEL, 2 * D_FF), lambda i, te, tr: (te[i], 0, 0)),
            pl.BlockSpec((1, 1, 2 * D_FF), lambda i, te, tr: (te[i], 0, 0)),
            pl.BlockSpec((1, D_FF, D_MODEL), lambda i, te, tr: (te[i], 0, 0)),
            pl.BlockSpec((1, 1, D_MODEL), lambda i, te, tr: (te[i], 0, 0)),
        ],
        out_specs=pl.BlockSpec((EXPERT_TILE, D_MODEL), lambda i, te, tr: (i, 0)),
        scratch_shapes=[pltpu.VMEM((D_MODEL, 2 * D_FF), BF16), pltpu.VMEM((D_FF, D_MODEL), BF16)],
    )
    return pl.pallas_call(
        _expert_kernel,
        grid_spec=grid_spec,
        out_shape=jax.ShapeDtypeStruct(xs.shape, F32),
        compiler_params=_params("arbitrary"),
        name="expert_ffn",
    )(tile_expert, tile_rows, xs, w_gate_up, b_gate_up, w_down, b_down)


def _combine_kernel(dest_ref, gate_ref, y_ref, g_ref, b_ref, eo_ref, out_ref, buf, sem, *, rows):
    def copy(t, k, src_row):
        return pltpu.make_async_copy(eo_ref.at[pl.ds(src_row, 1)], buf.at[k, pl.ds(t, 1)], sem)

    def issue(t, carry):
        for k in range(TOP_K):
            copy(t, k, dest_ref[0, 0, t * TOP_K + k]).start(priority=k % 2)
        return carry

    lax.fori_loop(0, rows, issue, 0)

    def drain(t, carry):
        for k in range(TOP_K):
            copy(t, k, 0).wait()
        return carry

    lax.fori_loop(0, rows, drain, 0)

    gate = gate_ref[...]
    moe = gate[:, 0:1] * buf[0]
    for k in range(1, TOP_K):
        moe = moe + gate[:, k:k + 1] * buf[k]
    out_ref[...] = _layer_norm(DN_ALPHA * y_ref[...] + moe, g_ref[...], b_ref[...])


def _combine(dest, gates, y1, ln_g, ln_b, expert_out, *, rows):
    n = y1.shape[0]
    wide = pl.BlockSpec((rows, D_MODEL), lambda i: (i, 0))
    return pl.pallas_call(
        functools.partial(_combine_kernel, rows=rows),
        grid=(n // rows,),
        in_specs=[pl.BlockSpec((1, 1, rows * TOP_K), lambda i: (i, 0, 0), memory_space=pltpu.SMEM),
                  pl.BlockSpec((rows, V7X_LANES), lambda i: (i, 0)), wide,
                  _resident(ln_g.shape), _resident(ln_b.shape),
                  pl.BlockSpec(memory_space=pl.ANY)],
        out_specs=wide,
        out_shape=jax.ShapeDtypeStruct((n, D_MODEL), F32),
        scratch_shapes=[pltpu.VMEM((TOP_K, rows, D_MODEL), F32), pltpu.SemaphoreType.DMA],
        compiler_params=_params("arbitrary"),
        name="combine_norm",
    )(dest, gates, y1, ln_g, ln_b, expert_out)


def _dest_blocks(dest, rows):
    return dest.reshape(dest.shape[0] // rows, 1, rows * TOP_K)


def kernel(x_prompt, x_sample, mem_prompt, cache_attn_k, cache_attn_v, cache_conv, cache_mem_k,
           cache_mem_v, w_in, rel_bias, conv_w, w_mem_kv, w_gate, w_br_attn, w_br_conv, w_br_mem,
           w_out, ln1_g, ln1_b, w_router, b_router, w_gate_up, b_gate_up, w_down, b_down,
           ln2_g, ln2_b):
    depth = w_in.shape[0]
    assert depth == 1, "one layer only"
    batch, seq, _ = x_prompt.shape
    dec_batch, dec_seq, _ = x_sample.shape
    n_prompt = batch * seq
    n_sample = dec_batch * dec_seq
    assert seq % ROW_TILE == 0 and HIST == ROW_TILE and cache_attn_k.shape[2] == HIST
    assert dec_seq % 16 == 0 and dec_seq <= CHUNK and cache_conv.shape[2] == 2

    w_in_b = w_in[0].astype(BF16)
    w_mem_b = w_mem_kv[0].astype(BF16)
    wr = w_router[0]
    wr_hi = wr.astype(BF16)
    wr_lo = (wr - wr_hi.astype(F32)).astype(BF16)
    merge_w = (w_gate[0].astype(BF16), w_br_attn[0].astype(BF16), w_br_conv[0].astype(BF16),
               w_br_mem[0].astype(BF16), w_out[0].astype(BF16), ln1_g, ln1_b,
               jnp.concatenate([wr_hi, wr_lo], axis=1), wr_hi, b_router)
    conv_w8 = jnp.pad(conv_w[0], ((0, 5), (0, 0)))
    bias = _band_bias(rel_bias[0])

    xp = x_prompt.reshape(n_prompt, D_MODEL)
    mk_f, mv_f, mk_b, mv_b = _memory_kv(mem_prompt.reshape(batch * MEM_TOKENS, D_MODEL), w_mem_b)
    q, k_pad, v_pad, c, qm, k_tail, v_tail, u_tail = _proj_in(
        xp, w_in_b, conv_w8, jnp.zeros((batch, 8, WIDTH), F32),
        batch=batch, seq=seq, rows=ROW_TILE, pad_steps=1)
    a, m = _attention(
        q, k_pad, v_pad, bias.reshape(ATT_HEADS * CHUNK, WINDOW), qm,
        mk_b.reshape(batch, MEM_TOKENS, WIDTH),
        mv_b.reshape(batch, MEM_TOKENS, WIDTH), batch=batch, seq=seq, rows=ROW_TILE,
        chunk_rows=CHUNK, valid_keys=BAND, first_chunk=0, mem_rows=128)
    y1_p, sel_p, topi_p, topg_p = _merge(xp, a, c, m, merge_w, rows=ROW_TILE)

    xs = x_sample.reshape(n_sample, D_MODEL)
    conv_init = jnp.pad(cache_conv[0], ((0, 0), (6, 0), (0, 0)))
    q_s, k_s, v_s, c_s, qm_s, k_new, v_new, u_tail_s = _proj_in(
        xs, w_in_b, conv_w8, conv_init, batch=dec_batch, seq=dec_seq, rows=dec_seq, pad_steps=0)
    pad_rows = WINDOW - HIST - dec_seq

    def window(cache, new):
        cache = cache[0].reshape(dec_batch, HIST, WIDTH).astype(BF16)
        return jnp.pad(jnp.concatenate([cache, new], axis=1), ((0, 0), (0, pad_rows), (0, 0)))

    a_s, m_s = _attention(
        q_s, window(cache_attn_k, k_s), window(cache_attn_v, v_s),
        bias[:, :dec_seq].reshape(ATT_HEADS * dec_seq, WINDOW), qm_s,
        cache_mem_k[0].reshape(dec_batch, MEM_TOKENS, WIDTH).astype(BF16),
        cache_mem_v[0].reshape(dec_batch, MEM_TOKENS, WIDTH).astype(BF16),
        batch=dec_batch, seq=dec_seq, rows=dec_seq, chunk_rows=dec_seq,
        valid_keys=HIST + dec_seq, first_chunk=LEFT_CHUNKS, mem_rows=dec_seq)
    y1_s, sel_s, topi_s, topg_s = _merge(xs, a_s, c_s, m_s, merge_w, rows=n_sample)

    rank_p, cnt_p = _ranks(sel_p, topi_p, jnp.zeros((8, N_EXPERTS), F32), rows=ROW_TILE)
    rank_s, cnt = _ranks(sel_s, topi_s, cnt_p, rows=n_sample)
    counts = cnt[0].astype(I32)
    padded = (counts + EXPERT_TILE - 1) // EXPERT_TILE * EXPERT_TILE
    ends = jnp.cumsum(padded)
    starts = ends - padded
    n_tiles = (n_prompt + n_sample) * TOP_K // EXPERT_TILE + N_EXPERTS
    tile_start = jnp.arange(n_tiles, dtype=I32) * EXPERT_TILE
    tile_expert = jnp.minimum(
        jnp.sum((ends[None, :] <= tile_start[:, None]).astype(I32), axis=1), N_EXPERTS - 1)
    tile_rows = jnp.clip(starts[tile_expert] + counts[tile_expert] - tile_start, 0, EXPERT_TILE)
    dest_p = starts[topi_p[:, :TOP_K]] + rank_p[:, :TOP_K]
    dest_s = starts[topi_s[:, :TOP_K]] + rank_s[:, :TOP_K]

    n_sorted = n_tiles * EXPERT_TILE
    sorted_rows = _scatter_rows(_dest_blocks(dest_p, ROW_TILE), y1_p, None, rows=ROW_TILE,
                                n_sorted=n_sorted)
    sorted_rows = _scatter_rows(_dest_blocks(dest_s, n_sample), y1_s, sorted_rows, rows=n_sample,
                                n_sorted=n_sorted)
    expert_out = _experts(tile_expert, tile_rows, sorted_rows, w_gate_up[0],
                          b_gate_up[0].reshape(N_EXPERTS, 1, 2 * D_FF), w_down[0],
                          b_down[0].reshape(N_EXPERTS, 1, D_MODEL))
    y_p = _combine(_dest_blocks(dest_p, 256), topg_p, y1_p, ln2_g, ln2_b, expert_out, rows=256)
    y_s = _combine(_dest_blocks(dest_s, n_sample), topg_s, y1_s, ln2_g, ln2_b, expert_out,
                   rows=n_sample)

    def heads(t, n_b, n_rows, n_heads, dim):
        return t.reshape(1, n_b, n_rows, n_heads, dim)

    return (y_p.reshape(batch, seq, D_MODEL),
            y_s.reshape(dec_batch, dec_seq, D_MODEL),
            heads(k_tail, batch, HIST, ATT_HEADS, ATT_DIM),
            heads(v_tail, batch, HIST, ATT_HEADS, ATT_DIM),
            u_tail[:, 6:8].reshape(1, batch, 2, WIDTH),
            heads(mk_f, batch, MEM_TOKENS, MEM_HEADS, MEM_DIM),
            heads(mv_f, batch, MEM_TOKENS, MEM_HEADS, MEM_DIM),
            heads(k_new, dec_batch, dec_seq, ATT_HEADS, ATT_DIM),
            heads(v_new, dec_batch, dec_seq, ATT_HEADS, ATT_DIM),
            u_tail_s[:, 6:8].reshape(1, dec_batch, 2, WIDTH))
```

```python
import functools

import jax
import jax.numpy as jnp
from jax import lax
from jax.experimental import pallas as pl
from jax.experimental.pallas import tpu as pltpu

F32 = jnp.float32
BF16 = jnp.bfloat16
I32 = jnp.int32

D_MODEL = 1024
CHUNK = 64
LEFT_CHUNKS = 8
HIST = LEFT_CHUNKS * CHUNK
BAND = HIST + CHUNK
ATT_HEADS = 8
ATT_DIM = 64
REL_MAX = 128
WIDTH = 512
MEM_TOKENS = 256
MEM_HEADS = 4
MEM_DIM = 128
N_EXPERTS = 32
TOP_K = 4
D_FF = 1024
SWIGLU_LIMIT = 7.0
SWIGLU_ALPHA = 1.702
DN_ALPHA = 2.0 ** 0.25
LN_EPS = 1e-5
NEG_INF = -1e30

V7X_LANES = 128
V7X_VMEM_BYTES = 64 * 1024 * 1024
VMEM_LIMIT = V7X_VMEM_BYTES - 8 * 1024 * 1024

ROW_TILE = 512
WINDOW = BAND + CHUNK
EXPERT_TILE = 512
RUN_CHUNK = 8
LOCAL_BLOCK = 256


def _params(*sem):
    return pltpu.CompilerParams(dimension_semantics=sem, vmem_limit_bytes=VMEM_LIMIT)


def _resident(shape):
    nd = len(shape)
    return pl.BlockSpec(shape, lambda *_: (0,) * nd, pipeline_mode=pl.Buffered(1))


def _layer_norm(r, g, b):
    mu = jnp.mean(r, axis=-1, keepdims=True)
    d = r - mu
    var = jnp.mean(d * d, axis=-1, keepdims=True)
    return d * lax.rsqrt(var + LN_EPS) * g + b


def _memkv_kernel(x_ref, w_ref, kf_ref, vf_ref, kb_ref, vb_ref):
    y = jnp.dot(x_ref[...].astype(BF16), w_ref[...], preferred_element_type=F32)
    k = y[:, :WIDTH]
    v = y[:, WIDTH:]
    kf_ref[...] = k
    vf_ref[...] = v
    kb_ref[...] = k.astype(BF16)
    vb_ref[...] = v.astype(BF16)


def _memory_kv(mem2d, w_b):
    rows = mem2d.shape[0]
    tile = pl.BlockSpec((ROW_TILE, D_MODEL), lambda i: (i, 0))
    half = pl.BlockSpec((ROW_TILE, WIDTH), lambda i: (i, 0))
    return pl.pallas_call(
        _memkv_kernel,
        grid=(rows // ROW_TILE,),
        in_specs=[tile, _resident((D_MODEL, 2 * WIDTH))],
        out_specs=[half, half, half, half],
        out_shape=[jax.ShapeDtypeStruct((rows, WIDTH), F32)] * 2
        + [jax.ShapeDtypeStruct((rows, WIDTH), BF16)] * 2,
        compiler_params=_params("arbitrary"),
        name="memory_kv",
    )(mem2d, w_b)


def _proj_in_kernel(x_ref, w_ref, cw_ref, cinit_ref,
                    q_ref, k_ref, v_ref, c_ref, qm_ref, kt_ref, vt_ref, ut_ref,
                    carry_ref, *, pad_steps):
    s = pl.program_id(1)
    ns = pl.num_programs(1)
    rows = x_ref.shape[0]
    first = pad_steps
    last = ns - 1 - pad_steps

    if pad_steps:
        @pl.when((s < first) | (s > last))
        def _():
            k_ref[...] = jnp.zeros(k_ref.shape, k_ref.dtype)
            v_ref[...] = jnp.zeros(v_ref.shape, v_ref.dtype)

    @pl.when((s >= first) & (s <= last))
    def _():
        @pl.when(s == first)
        def _():
            carry_ref[...] = cinit_ref[0]

        xb = x_ref[...].astype(BF16)

        def proj(g):
            return jnp.dot(xb, w_ref[:, g * WIDTH:(g + 1) * WIDTH], preferred_element_type=F32)

        q_ref[...] = (proj(0) * (ATT_DIM ** -0.5)).astype(BF16)
        k = proj(1)
        v = proj(2)
        k_ref[0] = k.astype(BF16)
        v_ref[0] = v.astype(BF16)

        @pl.when(s == last)
        def _():
            kt_ref[0] = k
            vt_ref[0] = v

        bg = proj(3)
        u = proj(4) * proj(5)
        prev = carry_ref[...]
        row = lax.broadcasted_iota(I32, u.shape, 0)
        u1 = jnp.where(row == 0, prev[7:8], pltpu.roll(u, 1, 0))
        u2 = jnp.where(row == 0, prev[6:7], jnp.where(row == 1, prev[7:8], pltpu.roll(u, 2, 0)))
        cw = cw_ref[...]
        c_ref[...] = (bg * (cw[0:1] * u2 + cw[1:2] * u1 + cw[2:3] * u)).astype(BF16)
        tail = u[rows - 8:]
        carry_ref[...] = tail

        @pl.when(s == last)
        def _():
            ut_ref[0] = tail

        qm_ref[...] = proj(6).astype(BF16)


def _proj_in(x2d, w_b, conv_w8, conv_init, *, batch, seq, rows, pad_steps):
    n_data = seq // rows
    n_steps = n_data + 2 * pad_steps

    def data_idx(b, s):
        return (b * n_data + jnp.clip(s - pad_steps, 0, n_data - 1), 0)

    wide = pl.BlockSpec((rows, D_MODEL), data_idx)
    narrow = pl.BlockSpec((rows, WIDTH), data_idx)
    seq_blk = pl.BlockSpec((1, rows, WIDTH), lambda b, s: (b, s, 0))
    tail_blk = pl.BlockSpec((1, rows, WIDTH), lambda b, s: (b, 0, 0))
    tail8 = pl.BlockSpec((1, 8, WIDTH), lambda b, s: (b, 0, 0))
    tok = jax.ShapeDtypeStruct((batch * seq, WIDTH), BF16)
    kv = jax.ShapeDtypeStruct((batch, n_steps * rows, WIDTH), BF16)
    tail = jax.ShapeDtypeStruct((batch, rows, WIDTH), F32)
    return pl.pallas_call(
        functools.partial(_proj_in_kernel, pad_steps=pad_steps),
        grid=(batch, n_steps),
        in_specs=[wide, _resident(w_b.shape), _resident(conv_w8.shape), tail8],
        out_specs=[narrow, seq_blk, seq_blk, narrow, narrow, tail_blk, tail_blk, tail8],
        out_shape=[tok, kv, kv, tok, tok, tail, tail,
                   jax.ShapeDtypeStruct((batch, 8, WIDTH), F32)],
        scratch_shapes=[pltpu.VMEM((8, WIDTH), F32)],
        compiler_params=_params("arbitrary", "arbitrary"),
        name="proj_in",
    )(x2d, w_b, conv_w8, conv_init)


def _bias_kernel(tab_ref, out_ref):
    shape = (CHUNK, WINDOW)
    i = lax.broadcasted_iota(I32, shape, 0)
    j = lax.broadcasted_iota(I32, shape, 1)
    idx = jnp.clip(HIST + i - j, -REL_MAX, REL_MAX) + REL_MAX
    for h in range(ATT_HEADS):
        def body(d, acc, h=h):
            return jnp.where(idx == d, tab_ref[h, d], acc)
        out_ref[h] = lax.fori_loop(0, 2 * REL_MAX + 1, body, jnp.zeros(shape, F32))


def _band_bias(rel_table):
    return pl.pallas_call(
        _bias_kernel,
        in_specs=[pl.BlockSpec(memory_space=pltpu.SMEM)],
        out_shape=jax.ShapeDtypeStruct((ATT_HEADS, CHUNK, WINDOW), F32),
        name="band_bias",
    )(rel_table)


def _softmax_pv(sc, v):
    m = jnp.max(sc, axis=-1, keepdims=True)
    p = jnp.exp(sc - m)
    l = jnp.sum(p, axis=-1, keepdims=True)
    return jnp.dot(p.astype(BF16), v, preferred_element_type=F32) / l


def _attention_kernel(q_ref, k_ref, v_ref, bias_ref, qm_ref, mk_ref, mv_ref,
                      a_ref, m_ref, *, chunk_rows, n_chunks, valid_keys, first_chunk, mem_rows):
    s = pl.program_id(1)
    nr = chunk_rows
    pair_w = 2 * ATT_DIM
    col = lax.broadcasted_iota(I32, (ATT_HEADS * nr, WINDOW), 1)
    lane2 = lax.broadcasted_iota(I32, (2 * nr, pair_w), 1)
    row2 = lax.broadcasted_iota(I32, (2 * nr, pair_w), 0)
    own = (lane2 >= ATT_DIM) == (row2 >= nr)
    low_lanes = lax.broadcasted_iota(I32, (nr, pair_w), 1) < ATT_DIM
    nt = (((1,), (1,)), ((), ()))

    def chunk(c, carry):
        g = first_chunk + s * n_chunks + c
        r0 = pl.multiple_of(c * nr, nr)
        w0 = pl.multiple_of((s * n_chunks + c) * CHUNK, CHUNK)
        qc = q_ref[pl.ds(r0, nr), :].astype(F32)
        kw = k_ref[0, pl.ds(w0, WINDOW), :]
        vw = v_ref[0, pl.ds(w0, WINDOW), :]
        scores = []
        for pair in range(ATT_HEADS // 2):
            lanes = slice(pair * pair_w, (pair + 1) * pair_w)
            qp = qc[:, lanes]
            q2 = jnp.where(own, jnp.concatenate([qp, qp], axis=0), 0.0).astype(BF16)
            scores.append(lax.dot_general(q2, kw[:, lanes], nt, preferred_element_type=F32))
        sc = jnp.concatenate(scores, axis=0) + bias_ref[...]
        sc = jnp.where((col >= HIST - CHUNK * g) & (col < valid_keys), sc, NEG_INF)
        m = jnp.max(sc, axis=-1, keepdims=True)
        p = jnp.exp(sc - m)
        inv = 1.0 / jnp.sum(p, axis=-1, keepdims=True)
        pb = p.astype(BF16)
        outs = []
        for pair in range(ATT_HEADS // 2):
            lanes = slice(pair * pair_w, (pair + 1) * pair_w)
            rows = slice(pair * 2 * nr, (pair + 1) * 2 * nr)
            o = jnp.dot(pb[rows], vw[:, lanes], preferred_element_type=F32) * inv[rows]
            outs.append(jnp.where(low_lanes, o[:nr], o[nr:]))
        a_ref[pl.ds(r0, nr), :] = jnp.concatenate(outs, axis=1).astype(BF16)
        return carry

    lax.fori_loop(0, n_chunks, chunk, 0)

    rows = qm_ref.shape[0]
    for rb in range(rows // mem_rows):
        rs = slice(rb * mem_rows, (rb + 1) * mem_rows)
        for h in range(MEM_HEADS):
            lanes = slice(h * MEM_DIM, (h + 1) * MEM_DIM)
            sc = lax.dot_general(qm_ref[rs, lanes], mk_ref[0, :, lanes],
                                 (((1,), (1,)), ((), ())), preferred_element_type=F32)
            o = _softmax_pv(sc * (MEM_DIM ** -0.5), mv_ref[0, :, lanes])
            m_ref[rs, lanes] = o.astype(BF16)


def _attention(q, k_pad, v_pad, bias, qm, mk, mv, *, batch, seq, rows, chunk_rows,
               valid_keys, first_chunk, mem_rows):
    n_steps = seq // rows
    tok = pl.BlockSpec((rows, WIDTH), lambda b, s: (b * n_steps + s, 0))
    whole_seq = pl.BlockSpec((1, k_pad.shape[1], WIDTH), lambda b, s: (b, 0, 0))
    mem = pl.BlockSpec((1, MEM_TOKENS, WIDTH), lambda b, s: (b, 0, 0))
    kern = functools.partial(
        _attention_kernel, chunk_rows=chunk_rows, n_chunks=rows // chunk_rows,
        valid_keys=valid_keys, first_chunk=first_chunk, mem_rows=mem_rows)
    out = jax.ShapeDtypeStruct((batch * seq, WIDTH), BF16)
    return pl.pallas_call(
        kern,
        grid=(batch, n_steps),
        in_specs=[tok, whole_seq, whole_seq, _resident(bias.shape), tok, mem, mem],
        out_specs=[tok, tok],
        out_shape=[out, out],
        compiler_params=_params("arbitrary", "arbitrary"),
        name="attention",
    )(q, k_pad, v_pad, bias, qm, mk, mv)


def _merge_kernel(x_ref, a_ref, c_ref, m_ref, wg_ref, wa_ref, wc_ref, wm_ref, wo_ref,
                  g_ref, b_ref, wr2_ref, wrh_ref, br_ref,
                  y_ref, sel_ref, topi_ref, topg_ref):
    x = x_ref[...]
    xb = x.astype(BF16)
    comb = None
    for i, (br, wbr) in enumerate(((a_ref, wa_ref), (c_ref, wc_ref), (m_ref, wm_ref))):
        gate = jax.nn.sigmoid(jnp.dot(xb, wg_ref[:, i * D_MODEL:(i + 1) * D_MODEL],
                                      preferred_element_type=F32))
        term = gate * jnp.dot(br[...], wbr[...], preferred_element_type=F32)
        comb = term if comb is None else comb + term
    out = jnp.dot(comb.astype(BF16), wo_ref[...], preferred_element_type=F32)
    y = _layer_norm(DN_ALPHA * x + out, g_ref[...], b_ref[...])
    y_ref[...] = y

    y_hi = y.astype(BF16)
    y_lo = (y - y_hi.astype(F32)).astype(BF16)
    both = jnp.dot(y_hi, wr2_ref[...], preferred_element_type=F32)
    logits = (both[:, :N_EXPERTS] + both[:, N_EXPERTS:]
              + jnp.dot(y_lo, wrh_ref[...], preferred_element_type=F32) + br_ref[...])

    rows = logits.shape[0]
    lane = lax.broadcasted_iota(I32, (rows, N_EXPERTS), 1).astype(F32)
    lane_out = lax.broadcasted_iota(I32, (rows, V7X_LANES), 1)
    work = logits
    sel = jnp.zeros((rows, N_EXPERTS), F32)
    topi = jnp.zeros((rows, V7X_LANES), I32)
    vals = []
    for k in range(TOP_K):
        top = jnp.max(work, axis=-1, keepdims=True)
        idx = jnp.min(jnp.where(work == top, lane, float(N_EXPERTS)), axis=-1, keepdims=True)
        hit = lane == idx
        sel = jnp.where(hit, 1.0, sel)
        work = jnp.where(hit, -jnp.inf, work)
        topi = jnp.where(lane_out == k, idx.astype(I32), topi)
        vals.append(top)
    exps = [jnp.exp(v - vals[0]) for v in vals]
    denom = exps[0] + exps[1] + exps[2] + exps[3]
    topg = jnp.zeros((rows, V7X_LANES), F32)
    for k in range(TOP_K):
        topg = jnp.where(lane_out == k, exps[k] / denom, topg)
    sel_ref[...] = sel.astype(BF16)
    topi_ref[...] = topi
    topg_ref[...] = topg


def _merge(x2d, a, c, m, weights, *, rows):
    n = x2d.shape[0]
    wide = pl.BlockSpec((rows, D_MODEL), lambda i: (i, 0))
    narrow = pl.BlockSpec((rows, WIDTH), lambda i: (i, 0))
    lanes = pl.BlockSpec((rows, V7X_LANES), lambda i: (i, 0))
    return pl.pallas_call(
        _merge_kernel,
        grid=(n // rows,),
        in_specs=[wide, narrow, narrow, narrow] + [_resident(w.shape) for w in weights],
        out_specs=[wide, pl.BlockSpec((rows, N_EXPERTS), lambda i: (i, 0)), lanes, lanes],
        out_shape=[jax.ShapeDtypeStruct((n, D_MODEL), F32),
                   jax.ShapeDtypeStruct((n, N_EXPERTS), BF16),
                   jax.ShapeDtypeStruct((n, V7X_LANES), I32),
                   jax.ShapeDtypeStruct((n, V7X_LANES), F32)],
        compiler_params=_params("arbitrary"),
        name="merge_router",
    )(x2d, a, c, m, *weights)


def _rank_kernel(sel_ref, topi_ref, init_ref, rank_ref, before_ref, cnt_ref, carry_ref):
    @pl.when(pl.program_id(0) == 0)
    def _():
        carry_ref[...] = init_ref[...]

    before_ref[0] = carry_ref[...]
    sel = sel_ref[...]
    rows = sel.shape[0]
    r = lax.broadcasted_iota(I32, (rows, rows), 0)
    c = lax.broadcasted_iota(I32, (rows, rows), 1)
    before = jnp.where(c < r, 1.0, 0.0).astype(BF16)
    counts = jnp.dot(before, sel, preferred_element_type=F32) + carry_ref[0:1]
    lane = lax.broadcasted_iota(I32, (rows, N_EXPERTS), 1)
    lane_out = lax.broadcasted_iota(I32, (rows, V7X_LANES), 1)
    topi = topi_ref[...]
    rank = jnp.zeros((rows, V7X_LANES), I32)
    for k in range(TOP_K):
        mine = jnp.sum(jnp.where(lane == topi[:, k:k + 1], counts, 0.0), axis=-1, keepdims=True)
        rank = jnp.where(lane_out == k, mine.astype(I32), rank)
    rank_ref[...] = rank
    total = carry_ref[...] + jnp.sum(sel.astype(F32), axis=0, keepdims=True)
    carry_ref[...] = total
    cnt_ref[...] = total


def _ranks(sel, topi, init, *, rows):
    n = sel.shape[0]
    small = pl.BlockSpec((8, N_EXPERTS), lambda i: (0, 0))
    lanes = pl.BlockSpec((rows, V7X_LANES), lambda i: (i, 0))
    return pl.pallas_call(
        _rank_kernel,
        grid=(n // rows,),
        in_specs=[pl.BlockSpec((rows, N_EXPERTS), lambda i: (i, 0)), lanes, small],
        out_specs=[lanes, pl.BlockSpec((1, 8, N_EXPERTS), lambda i: (i, 0, 0)), small],
        out_shape=[jax.ShapeDtypeStruct((n, V7X_LANES), I32),
                   jax.ShapeDtypeStruct((n // rows, 8, N_EXPERTS), F32),
                   jax.ShapeDtypeStruct((8, N_EXPERTS), F32)],
        scratch_shapes=[pltpu.VMEM((8, N_EXPERTS), F32)],
        compiler_params=_params("arbitrary"),
        name="expert_ranks",
    )(sel, topi, init)


def _run_copies(scal_ref, local_ref, sorted_ref, sem, *, to_sorted):
    def chunk_copy(local_row, sorted_row):
        if not isinstance(local_row, int):
            local_row = pl.multiple_of(local_row, RUN_CHUNK)
            sorted_row = pl.multiple_of(sorted_row, RUN_CHUNK)
        loc = local_ref.at[pl.ds(local_row, RUN_CHUNK)]
        srt = sorted_ref.at[pl.ds(sorted_row, RUN_CHUNK)]
        return pltpu.make_async_copy(loc, srt, sem) if to_sorted else pltpu.make_async_copy(srt, loc, sem)

    def expert(e, carry):
        sorted0 = scal_ref[0, 0, e]
        local0 = scal_ref[0, 0, 2 * N_EXPERTS + e]

        def chunk(c, inner):
            chunk_copy(local0 + c * RUN_CHUNK, sorted0 + c * RUN_CHUNK).start()
            return inner

        lax.fori_loop(0, scal_ref[0, 0, N_EXPERTS + e], chunk, 0)
        return carry

    lax.fori_loop(0, N_EXPERTS, expert, 0)

    def drain(c, carry):
        chunk_copy(0, 0).wait()
        return carry

    lax.fori_loop(0, scal_ref[0, 0, 3 * N_EXPERTS], drain, 0)


def _dispatch_kernel(scal_ref, lpos_ref, y_ref, *rest, local_rows):
    sorted_ref, stage, sem = rest[-3:]
    rows = y_ref.shape[0]
    yb = y_ref[...].astype(BF16)
    lpos = lpos_ref[0]
    blk = LOCAL_BLOCK
    for rb in range(local_rows // blk):
        j = lax.broadcasted_iota(I32, (blk, rows), 0) + rb * blk
        hit = (j == lpos[0:1]) | (j == lpos[1:2]) | (j == lpos[2:3]) | (j == lpos[3:4])
        onehot = jnp.where(hit, 1.0, 0.0).astype(BF16)
        stage[rb * blk:(rb + 1) * blk, :] = jnp.dot(onehot, yb, preferred_element_type=F32)
    _run_copies(scal_ref, stage, sorted_ref, sem, to_sorted=True)


def _local_rows(rows):
    worst = rows * TOP_K + N_EXPERTS * (RUN_CHUNK - 1)
    return (worst + LOCAL_BLOCK - 1) // LOCAL_BLOCK * LOCAL_BLOCK


def _dispatch(scal, lpos_t, y1, sorted_rows, *, rows, n_sorted):
    n = y1.shape[0]
    local_rows = _local_rows(rows)
    any_spec = pl.BlockSpec(memory_space=pl.ANY)
    in_specs = [pl.BlockSpec((1, 1, V7X_LANES), lambda i: (i, 0, 0), memory_space=pltpu.SMEM),
                pl.BlockSpec((1, 8, rows), lambda i: (i, 0, 0)),
                pl.BlockSpec((rows, D_MODEL), lambda i: (i, 0))]
    args = [scal, lpos_t, y1]
    aliases = {}
    if sorted_rows is not None:
        in_specs.append(any_spec)
        args.append(sorted_rows)
        aliases = {3: 0}
    return pl.pallas_call(
        functools.partial(_dispatch_kernel, local_rows=local_rows),
        grid=(n // rows,),
        in_specs=in_specs,
        out_specs=any_spec,
        out_shape=jax.ShapeDtypeStruct((n_sorted, D_MODEL), F32),
        scratch_shapes=[pltpu.VMEM((local_rows, D_MODEL), F32), pltpu.SemaphoreType.DMA],
        input_output_aliases=aliases,
        compiler_params=_params("arbitrary"),
        name="dispatch_rows",
    )(*args)


def _expert_kernel(te_ref, tr_ref, x_ref, wgu_ref, bgu_ref, wd_ref, bd_ref, o_ref,
                   wgu_b, wd_b):
    i = pl.program_id(0)
    e = te_ref[i]
    changed = (i == 0) | (e != te_ref[jnp.maximum(i - 1, 0)])

    @pl.when(changed)
    def _():
        wgu_b[...] = wgu_ref[0].astype(BF16)
        wd_b[...] = wd_ref[0].astype(BF16)

    @pl.when(tr_ref[i] == 0)
    def _():
        o_ref[...] = jnp.zeros(o_ref.shape, o_ref.dtype)

    @pl.when(tr_ref[i] > 0)
    def _():
        row = lax.broadcasted_iota(I32, x_ref.shape, 0)
        xb = jnp.where(row < tr_ref[i], x_ref[...], 0.0).astype(BF16)
        bgu = bgu_ref[0]
        acc = None
        n_col = 256
        for j in range(D_FF // n_col):
            cg = slice(j * n_col, (j + 1) * n_col)
            cl = slice(D_FF + j * n_col, D_FF + (j + 1) * n_col)
            hg = jnp.dot(xb, wgu_b[:, cg], preferred_element_type=F32) + bgu[:, cg]
            hl = jnp.dot(xb, wgu_b[:, cl], preferred_element_type=F32) + bgu[:, cl]
            glu = jnp.minimum(hg, SWIGLU_LIMIT)
            lin = jnp.clip(hl, -SWIGLU_LIMIT, SWIGLU_LIMIT)
            act = glu * jax.nn.sigmoid(SWIGLU_ALPHA * glu) * (lin + 1.0)
            part = jnp.dot(act.astype(BF16), wd_b[cg, :], preferred_element_type=F32)
            acc = part if acc is None else acc + part
        o_ref[...] = acc + bd_ref[0]


def _experts(tile_expert, tile_rows, xs, w_gate_up, b_gate_up, w_down, b_down):
    n_tiles = xs.shape[0] // EXPERT_TILE
    grid_spec = pltpu.PrefetchScalarGridSpec(
        num_scalar_prefetch=2,
        grid=(n_tiles,),
        in_specs=[
            pl.BlockSpec((EXPERT_TILE, D_MODEL), lambda i, te, tr: (i, 0)),
            pl.BlockSpec((1, D_MODEL, 2 * D_FF), lambda i, te, tr: (te[i], 0, 0)),
            pl.BlockSpec((1, 1, 2 * D_FF), lambda i, te, tr: (te[i], 0, 0)),
            pl.BlockSpec((1, D_FF, D_MODEL), lambda i, te, tr: (te[i], 0, 0)),
            pl.BlockSpec((1, 1, D_MODEL), lambda i, te, tr: (te[i], 0, 0)),
        ],
        out_specs=pl.BlockSpec((EXPERT_TILE, D_MODEL), lambda i, te, tr: (i, 0)),
        scratch_shapes=[pltpu.VMEM((D_MODEL, 2 * D_FF), BF16), pltpu.VMEM((D_FF, D_MODEL), BF16)],
    )
    return pl.pallas_call(
        _expert_kernel,
        grid_spec=grid_spec,
        out_shape=jax.ShapeDtypeStruct(xs.shape, F32),
        compiler_params=_params("arbitrary"),
        name="expert_ffn",
    )(tile_expert, tile_rows, xs, w_gate_up, b_gate_up, w_down, b_down)


def _combine_kernel(scal_ref, lpos_ref, gate_ref, y_ref, g_ref, b_ref, eo_ref, out_ref, buf, sem,
                    *, local_rows):
    @pl.when(pl.program_id(0) == 0)
    def _():
        buf[...] = jnp.zeros(buf.shape, buf.dtype)

    _run_copies(scal_ref, buf, eo_ref, sem, to_sorted=False)
    rows = y_ref.shape[0]
    lpos = lpos_ref[...]
    gate = gate_ref[...]
    blk = LOCAL_BLOCK
    moe = None
    for kb in range(local_rows // blk):
        lane = lax.broadcasted_iota(I32, (rows, blk), 1) + kb * blk
        w = jnp.zeros((rows, blk), F32)
        for k in range(TOP_K):
            w = jnp.where(lane == lpos[:, k:k + 1], gate[:, k:k + 1], w)
        part = jnp.dot(w.astype(BF16), buf[kb * blk:(kb + 1) * blk, :].astype(BF16),
                       preferred_element_type=F32)
        moe = part if moe is None else moe + part
    out_ref[...] = _layer_norm(DN_ALPHA * y_ref[...] + moe, g_ref[...], b_ref[...])


def _combine(scal, lpos, gates, y1, ln_g, ln_b, expert_out, *, rows):
    n = y1.shape[0]
    local_rows = _local_rows(rows)
    wide = pl.BlockSpec((rows, D_MODEL), lambda i: (i, 0))
    lanes = pl.BlockSpec((rows, V7X_LANES), lambda i: (i, 0))
    return pl.pallas_call(
        functools.partial(_combine_kernel, local_rows=local_rows),
        grid=(n // rows,),
        in_specs=[pl.BlockSpec((1, 1, V7X_LANES), lambda i: (i, 0, 0), memory_space=pltpu.SMEM),
                  lanes, lanes, wide, _resident(ln_g.shape), _resident(ln_b.shape),
                  pl.BlockSpec(memory_space=pl.ANY)],
        out_specs=wide,
        out_shape=jax.ShapeDtypeStruct((n, D_MODEL), F32),
        scratch_shapes=[pltpu.VMEM((local_rows, D_MODEL), F32), pltpu.SemaphoreType.DMA],
        compiler_params=_params("arbitrary"),
        name="combine_norm",
    )(scal, lpos, gates, y1, ln_g, ln_b, expert_out)


def _tile_tables(topi, rank, before, n_chunk, sorted0, rows):
    n_tiles = topi.shape[0] // rows
    local0 = (jnp.cumsum(n_chunk, axis=1) - n_chunk) * RUN_CHUNK
    pad = jnp.zeros((n_tiles, V7X_LANES - 3 * N_EXPERTS - 1), I32)
    scal = jnp.concatenate([sorted0, n_chunk, local0,
                            jnp.sum(n_chunk, axis=1, keepdims=True), pad], axis=1)
    slot_expert = topi[:, :TOP_K].reshape(n_tiles, rows * TOP_K)
    lpos = (jnp.take_along_axis(local0 - before, slot_expert, axis=1)
            + rank[:, :TOP_K].reshape(n_tiles, rows * TOP_K)).reshape(n_tiles, rows, TOP_K)
    lpos_t = jnp.pad(jnp.transpose(lpos, (0, 2, 1)), ((0, 0), (0, 8 - TOP_K), (0, 0)),
                     constant_values=-1)
    lpos_l = jnp.pad(lpos.reshape(n_tiles * rows, TOP_K), ((0, 0), (0, V7X_LANES - TOP_K)),
                     constant_values=-1)
    return scal.reshape(n_tiles, 1, V7X_LANES), lpos_t, lpos_l


def kernel(x_prompt, x_sample, mem_prompt, cache_attn_k, cache_attn_v, cache_conv, cache_mem_k,
           cache_mem_v, w_in, rel_bias, conv_w, w_mem_kv, w_gate, w_br_attn, w_br_conv, w_br_mem,
           w_out, ln1_g, ln1_b, w_router, b_router, w_gate_up, b_gate_up, w_down, b_down,
           ln2_g, ln2_b):
    depth = w_in.shape[0]
    assert depth == 1, "one layer only"
    batch, seq, _ = x_prompt.shape
    dec_batch, dec_seq, _ = x_sample.shape
    n_prompt = batch * seq
    n_sample = dec_batch * dec_seq
    assert seq % ROW_TILE == 0 and HIST == ROW_TILE and cache_attn_k.shape[2] == HIST
    assert dec_seq % 16 == 0 and dec_seq <= CHUNK and cache_conv.shape[2] == 2

    w_in_b = w_in[0].astype(BF16)
    w_mem_b = w_mem_kv[0].astype(BF16)
    wr = w_router[0]
    wr_hi = wr.astype(BF16)
    wr_lo = (wr - wr_hi.astype(F32)).astype(BF16)
    merge_w = (w_gate[0].astype(BF16), w_br_attn[0].astype(BF16), w_br_conv[0].astype(BF16),
               w_br_mem[0].astype(BF16), w_out[0].astype(BF16), ln1_g, ln1_b,
               jnp.concatenate([wr_hi, wr_lo], axis=1), wr_hi, b_router)
    conv_w8 = jnp.pad(conv_w[0], ((0, 5), (0, 0)))
    bias = _band_bias(rel_bias[0])

    xp = x_prompt.reshape(n_prompt, D_MODEL)
    mk_f, mv_f, mk_b, mv_b = _memory_kv(mem_prompt.reshape(batch * MEM_TOKENS, D_MODEL), w_mem_b)
    q, k_pad, v_pad, c, qm, k_tail, v_tail, u_tail = _proj_in(
        xp, w_in_b, conv_w8, jnp.zeros((batch, 8, WIDTH), F32),
        batch=batch, seq=seq, rows=ROW_TILE, pad_steps=1)
    a, m = _attention(
        q, k_pad, v_pad, bias.reshape(ATT_HEADS * CHUNK, WINDOW), qm,
        mk_b.reshape(batch, MEM_TOKENS, WIDTH),
        mv_b.reshape(batch, MEM_TOKENS, WIDTH), batch=batch, seq=seq, rows=ROW_TILE,
        chunk_rows=CHUNK, valid_keys=BAND, first_chunk=0, mem_rows=128)
    y1_p, sel_p, topi_p, topg_p = _merge(xp, a, c, m, merge_w, rows=ROW_TILE)

    xs = x_sample.reshape(n_sample, D_MODEL)
    conv_init = jnp.pad(cache_conv[0], ((0, 0), (6, 0), (0, 0)))
    q_s, k_s, v_s, c_s, qm_s, k_new, v_new, u_tail_s = _proj_in(
        xs, w_in_b, conv_w8, conv_init, batch=dec_batch, seq=dec_seq, rows=dec_seq, pad_steps=0)
    pad_rows = WINDOW - HIST - dec_seq

    def window(cache, new):
        cache = cache[0].reshape(dec_batch, HIST, WIDTH).astype(BF16)
        return jnp.pad(jnp.concatenate([cache, new], axis=1), ((0, 0), (0, pad_rows), (0, 0)))

    a_s, m_s = _attention(
        q_s, window(cache_attn_k, k_s), window(cache_attn_v, v_s),
        bias[:, :dec_seq].reshape(ATT_HEADS * dec_seq, WINDOW), qm_s,
        cache_mem_k[0].reshape(dec_batch, MEM_TOKENS, WIDTH).astype(BF16),
        cache_mem_v[0].reshape(dec_batch, MEM_TOKENS, WIDTH).astype(BF16),
        batch=dec_batch, seq=dec_seq, rows=dec_seq, chunk_rows=dec_seq,
        valid_keys=HIST + dec_seq, first_chunk=LEFT_CHUNKS, mem_rows=dec_seq)
    y1_s, sel_s, topi_s, topg_s = _merge(xs, a_s, c_s, m_s, merge_w, rows=n_sample)

    rank_p, before_p, cnt_p = _ranks(sel_p, topi_p, jnp.zeros((8, N_EXPERTS), F32), rows=ROW_TILE)
    rank_s, before_s, cnt = _ranks(sel_s, topi_s, cnt_p, rows=n_sample)
    before = jnp.concatenate([before_p[:, 0], before_s[:, 0]], axis=0).astype(I32)
    n_run = jnp.concatenate([before[1:], cnt[:1].astype(I32)], axis=0) - before
    n_chunk = (n_run + RUN_CHUNK - 1) // RUN_CHUNK
    run_rows = n_chunk * RUN_CHUNK
    counts = jnp.sum(run_rows, axis=0)
    padded = (counts + EXPERT_TILE - 1) // EXPERT_TILE * EXPERT_TILE
    ends = jnp.cumsum(padded)
    starts = ends - padded
    sorted0 = starts[None, :] + jnp.cumsum(run_rows, axis=0) - run_rows
    n_token_tiles = n_prompt // ROW_TILE + 1
    n_tiles = ((n_prompt + n_sample) * TOP_K
               + n_token_tiles * N_EXPERTS * (RUN_CHUNK - 1)) // EXPERT_TILE + N_EXPERTS
    tile_start = jnp.arange(n_tiles, dtype=I32) * EXPERT_TILE
    tile_expert = jnp.minimum(
        jnp.sum((ends[None, :] <= tile_start[:, None]).astype(I32), axis=1), N_EXPERTS - 1)
    tile_rows = jnp.clip(starts[tile_expert] + counts[tile_expert] - tile_start, 0, EXPERT_TILE)
    scal_p, lpos_tp, lpos_lp = _tile_tables(topi_p, rank_p, before[:-1], n_chunk[:-1], sorted0[:-1],
                                            ROW_TILE)
    scal_s, lpos_ts, lpos_ls = _tile_tables(topi_s, rank_s, before[-1:], n_chunk[-1:], sorted0[-1:],
                                            n_sample)

    n_sorted = n_tiles * EXPERT_TILE
    sorted_rows = _dispatch(scal_p, lpos_tp, y1_p, None, rows=ROW_TILE, n_sorted=n_sorted)
    sorted_rows = _dispatch(scal_s, lpos_ts, y1_s, sorted_rows, rows=n_sample, n_sorted=n_sorted)
    expert_out = _experts(tile_expert, tile_rows, sorted_rows, w_gate_up[0],
                          b_gate_up[0].reshape(N_EXPERTS, 1, 2 * D_FF), w_down[0],
                          b_down[0].reshape(N_EXPERTS, 1, D_MODEL))
    y_p = _combine(scal_p, lpos_lp, topg_p, y1_p, ln2_g, ln2_b, expert_out, rows=ROW_TILE)
    y_s = _combine(scal_s, lpos_ls, topg_s, y1_s, ln2_g, ln2_b, expert_out, rows=n_sample)

    def heads(t, n_b, n_rows, n_heads, dim):
        return t.reshape(1, n_b, n_rows, n_heads, dim)

    return (y_p.reshape(batch, seq, D_MODEL),
            y_s.reshape(dec_batch, dec_seq, D_MODEL),
            heads(k_tail, batch, HIST, ATT_HEADS, ATT_DIM),
            heads(v_tail, batch, HIST, ATT_HEADS, ATT_DIM),
            u_tail[:, 6:8].reshape(1, batch, 2, WIDTH),
            heads(mk_f, batch, MEM_TOKENS, MEM_HEADS, MEM_DIM),
            heads(mv_f, batch, MEM_TOKENS, MEM_HEADS, MEM_DIM),
            heads(k_new, dec_batch, dec_seq, ATT_HEADS, ATT_DIM),
            heads(v_new, dec_batch, dec_seq, ATT_HEADS, ATT_DIM),
            u_tail_s[:, 6:8].reshape(1, dec_batch, 2, WIDTH))
```

```python
import functools

import jax
import jax.numpy as jnp
from jax import lax
from jax.experimental import pallas as pl
from jax.experimental.pallas import tpu as pltpu

F32 = jnp.float32
BF16 = jnp.bfloat16
I32 = jnp.int32

D_MODEL = 1024
CHUNK = 64
LEFT_CHUNKS = 8
HIST = LEFT_CHUNKS * CHUNK
BAND = HIST + CHUNK
ATT_HEADS = 8
ATT_DIM = 64
REL_MAX = 128
WIDTH = 512
MEM_TOKENS = 256
MEM_HEADS = 4
MEM_DIM = 128
N_EXPERTS = 32
TOP_K = 4
D_FF = 1024
SWIGLU_LIMIT = 7.0
SWIGLU_ALPHA = 1.702
DN_ALPHA = 2.0 ** 0.25
LN_EPS = 1e-5
NEG_INF = -1e30

V7X_LANES = 128
V7X_VMEM_BYTES = 64 * 1024 * 1024
VMEM_LIMIT = V7X_VMEM_BYTES - 8 * 1024 * 1024

ROW_TILE = 512
WINDOW = BAND + CHUNK
EXPERT_TILE = 512
RUN_CHUNK = 8
LOCAL_BLOCK = 256


def _params(*sem):
    return pltpu.CompilerParams(dimension_semantics=sem, vmem_limit_bytes=VMEM_LIMIT)


def _resident(shape):
    nd = len(shape)
    return pl.BlockSpec(shape, lambda *_: (0,) * nd, pipeline_mode=pl.Buffered(1))


def _layer_norm(r, g, b):
    mu = jnp.mean(r, axis=-1, keepdims=True)
    d = r - mu
    var = jnp.mean(d * d, axis=-1, keepdims=True)
    return d * lax.rsqrt(var + LN_EPS) * g + b


def _memkv_kernel(x_ref, w_ref, kf_ref, vf_ref, kb_ref, vb_ref):
    y = jnp.dot(x_ref[...].astype(BF16), w_ref[...], preferred_element_type=F32)
    k = y[:, :WIDTH]
    v = y[:, WIDTH:]
    kf_ref[...] = k
    vf_ref[...] = v
    kb_ref[...] = k.astype(BF16)
    vb_ref[...] = v.astype(BF16)


def _memory_kv(mem2d, w_b):
    rows = mem2d.shape[0]
    tile = pl.BlockSpec((ROW_TILE, D_MODEL), lambda i: (i, 0))
    half = pl.BlockSpec((ROW_TILE, WIDTH), lambda i: (i, 0))
    return pl.pallas_call(
        _memkv_kernel,
        grid=(rows // ROW_TILE,),
        in_specs=[tile, _resident((D_MODEL, 2 * WIDTH))],
        out_specs=[half, half, half, half],
        out_shape=[jax.ShapeDtypeStruct((rows, WIDTH), F32)] * 2
        + [jax.ShapeDtypeStruct((rows, WIDTH), BF16)] * 2,
        compiler_params=_params("arbitrary"),
        name="memory_kv",
    )(mem2d, w_b)


def _proj_in_kernel(x_ref, w_ref, cw_ref, cinit_ref,
                    q_ref, k_ref, v_ref, c_ref, qm_ref, kt_ref, vt_ref, ut_ref,
                    carry_ref, *, pad_steps):
    s = pl.program_id(1)
    ns = pl.num_programs(1)
    rows = x_ref.shape[0]
    first = pad_steps
    last = ns - 1 - pad_steps

    if pad_steps:
        @pl.when((s < first) | (s > last))
        def _():
            k_ref[...] = jnp.zeros(k_ref.shape, k_ref.dtype)
            v_ref[...] = jnp.zeros(v_ref.shape, v_ref.dtype)

    @pl.when((s >= first) & (s <= last))
    def _():
        @pl.when(s == first)
        def _():
            carry_ref[...] = cinit_ref[0]

        xb = x_ref[...].astype(BF16)

        def proj(g):
            return jnp.dot(xb, w_ref[:, g * WIDTH:(g + 1) * WIDTH], preferred_element_type=F32)

        q_ref[...] = (proj(0) * (ATT_DIM ** -0.5)).astype(BF16)
        k = proj(1)
        v = proj(2)
        k_ref[0] = k.astype(BF16)
        v_ref[0] = v.astype(BF16)

        @pl.when(s == last)
        def _():
            kt_ref[0] = k
            vt_ref[0] = v

        bg = proj(3)
        u = proj(4) * proj(5)
        prev = carry_ref[...]
        row = lax.broadcasted_iota(I32, u.shape, 0)
        u1 = jnp.where(row == 0, prev[7:8], pltpu.roll(u, 1, 0))
        u2 = jnp.where(row == 0, prev[6:7], jnp.where(row == 1, prev[7:8], pltpu.roll(u, 2, 0)))
        cw = cw_ref[...]
        c_ref[...] = (bg * (cw[0:1] * u2 + cw[1:2] * u1 + cw[2:3] * u)).astype(BF16)
        tail = u[rows - 8:]
        carry_ref[...] = tail

        @pl.when(s == last)
        def _():
            ut_ref[0] = tail

        qm_ref[...] = proj(6).astype(BF16)


def _proj_in(x2d, w_b, conv_w8, conv_init, *, batch, seq, rows, pad_steps):
    n_data = seq // rows
    n_steps = n_data + 2 * pad_steps

    def data_idx(b, s):
        return (b * n_data + jnp.clip(s - pad_steps, 0, n_data - 1), 0)

    wide = pl.BlockSpec((rows, D_MODEL), data_idx)
    narrow = pl.BlockSpec((rows, WIDTH), data_idx)
    seq_blk = pl.BlockSpec((1, rows, WIDTH), lambda b, s: (b, s, 0))
    tail_blk = pl.BlockSpec((1, rows, WIDTH), lambda b, s: (b, 0, 0))
    tail8 = pl.BlockSpec((1, 8, WIDTH), lambda b, s: (b, 0, 0))
    tok = jax.ShapeDtypeStruct((batch * seq, WIDTH), BF16)
    kv = jax.ShapeDtypeStruct((batch, n_steps * rows, WIDTH), BF16)
    tail = jax.ShapeDtypeStruct((batch, rows, WIDTH), F32)
    return pl.pallas_call(
        functools.partial(_proj_in_kernel, pad_steps=pad_steps),
        grid=(batch, n_steps),
        in_specs=[wide, _resident(w_b.shape), _resident(conv_w8.shape), tail8],
        out_specs=[narrow, seq_blk, seq_blk, narrow, narrow, tail_blk, tail_blk, tail8],
        out_shape=[tok, kv, kv, tok, tok, tail, tail,
                   jax.ShapeDtypeStruct((batch, 8, WIDTH), F32)],
        scratch_shapes=[pltpu.VMEM((8, WIDTH), F32)],
        compiler_params=_params("arbitrary", "arbitrary"),
        name="proj_in",
    )(x2d, w_b, conv_w8, conv_init)


def _bias_kernel(tab_ref, out_ref):
    shape = (CHUNK, WINDOW)
    i = lax.broadcasted_iota(I32, shape, 0)
    j = lax.broadcasted_iota(I32, shape, 1)
    idx = jnp.clip(HIST + i - j, -REL_MAX, REL_MAX) + REL_MAX
    for h in range(ATT_HEADS):
        def body(d, acc, h=h):
            return jnp.where(idx == d, tab_ref[h, d], acc)
        out_ref[h] = lax.fori_loop(0, 2 * REL_MAX + 1, body, jnp.zeros(shape, F32))


def _band_bias(rel_table):
    return pl.pallas_call(
        _bias_kernel,
        in_specs=[pl.BlockSpec(memory_space=pltpu.SMEM)],
        out_shape=jax.ShapeDtypeStruct((ATT_HEADS, CHUNK, WINDOW), F32),
        name="band_bias",
    )(rel_table)


def _softmax_pv(sc, v):
    m = jnp.max(sc, axis=-1, keepdims=True)
    p = jnp.exp(sc - m)
    l = jnp.sum(p, axis=-1, keepdims=True)
    return jnp.dot(p.astype(BF16), v, preferred_element_type=F32) / l


def _attention_kernel(q_ref, k_ref, v_ref, bias_ref, qm_ref, mk_ref, mv_ref,
                      a_ref, m_ref, *, chunk_rows, n_chunks, valid_keys, first_chunk, mem_rows):
    s = pl.program_id(1)
    nr = chunk_rows
    pair_w = 2 * ATT_DIM
    col = lax.broadcasted_iota(I32, (ATT_HEADS * nr, WINDOW), 1)
    lane2 = lax.broadcasted_iota(I32, (2 * nr, pair_w), 1)
    row2 = lax.broadcasted_iota(I32, (2 * nr, pair_w), 0)
    own = (lane2 >= ATT_DIM) == (row2 >= nr)
    low_lanes = lax.broadcasted_iota(I32, (nr, pair_w), 1) < ATT_DIM
    nt = (((1,), (1,)), ((), ()))

    def chunk(c, carry):
        g = first_chunk + s * n_chunks + c
        r0 = pl.multiple_of(c * nr, nr)
        w0 = pl.multiple_of((s * n_chunks + c) * CHUNK, CHUNK)
        qc = q_ref[pl.ds(r0, nr), :].astype(F32)
        kw = k_ref[0, pl.ds(w0, WINDOW), :]
        vw = v_ref[0, pl.ds(w0, WINDOW), :]
        scores = []
        for pair in range(ATT_HEADS // 2):
            lanes = slice(pair * pair_w, (pair + 1) * pair_w)
            qp = qc[:, lanes]
            q2 = jnp.where(own, jnp.concatenate([qp, qp], axis=0), 0.0).astype(BF16)
            scores.append(lax.dot_general(q2, kw[:, lanes], nt, preferred_element_type=F32))
        sc = jnp.concatenate(scores, axis=0) + bias_ref[...]
        sc = jnp.where((col >= HIST - CHUNK * g) & (col < valid_keys), sc, NEG_INF)
        m = jnp.max(sc, axis=-1, keepdims=True)
        p = jnp.exp(sc - m)
        inv = 1.0 / jnp.sum(p, axis=-1, keepdims=True)
        pb = p.astype(BF16)
        outs = []
        for pair in range(ATT_HEADS // 2):
            lanes = slice(pair * pair_w, (pair + 1) * pair_w)
            rows = slice(pair * 2 * nr, (pair + 1) * 2 * nr)
            o = jnp.dot(pb[rows], vw[:, lanes], preferred_element_type=F32) * inv[rows]
            outs.append(jnp.where(low_lanes, o[:nr], o[nr:]))
        a_ref[pl.ds(r0, nr), :] = jnp.concatenate(outs, axis=1).astype(BF16)
        return carry

    lax.fori_loop(0, n_chunks, chunk, 0)

    rows = qm_ref.shape[0]
    for rb in range(rows // mem_rows):
        rs = slice(rb * mem_rows, (rb + 1) * mem_rows)
        for h in range(MEM_HEADS):
            lanes = slice(h * MEM_DIM, (h + 1) * MEM_DIM)
            sc = lax.dot_general(qm_ref[rs, lanes], mk_ref[0, :, lanes],
                                 (((1,), (1,)), ((), ())), preferred_element_type=F32)
            o = _softmax_pv(sc * (MEM_DIM ** -0.5), mv_ref[0, :, lanes])
            m_ref[rs, lanes] = o.astype(BF16)


def _attention(q, k_pad, v_pad, bias, qm, mk, mv, *, batch, seq, rows, chunk_rows,
               valid_keys, first_chunk, mem_rows):
    n_steps = seq // rows
    tok = pl.BlockSpec((rows, WIDTH), lambda b, s: (b * n_steps + s, 0))
    whole_seq = pl.BlockSpec((1, k_pad.shape[1], WIDTH), lambda b, s: (b, 0, 0))
    mem = pl.BlockSpec((1, MEM_TOKENS, WIDTH), lambda b, s: (b, 0, 0))
    kern = functools.partial(
        _attention_kernel, chunk_rows=chunk_rows, n_chunks=rows // chunk_rows,
        valid_keys=valid_keys, first_chunk=first_chunk, mem_rows=mem_rows)
    out = jax.ShapeDtypeStruct((batch * seq, WIDTH), BF16)
    return pl.pallas_call(
        kern,
        grid=(batch, n_steps),
        in_specs=[tok, whole_seq, whole_seq, _resident(bias.shape), tok, mem, mem],
        out_specs=[tok, tok],
        out_shape=[out, out],
        compiler_params=_params("arbitrary", "arbitrary"),
        name="attention",
    )(q, k_pad, v_pad, bias, qm, mk, mv)


def _merge_kernel(x_ref, a_ref, c_ref, m_ref, wg_ref, wa_ref, wc_ref, wm_ref, wo_ref,
                  g_ref, b_ref, wr2_ref, wrh_ref, br_ref,
                  y_ref, sel_ref, topi_ref, topg_ref):
    x = x_ref[...]
    xb = x.astype(BF16)
    comb = None
    for i, (br, wbr) in enumerate(((a_ref, wa_ref), (c_ref, wc_ref), (m_ref, wm_ref))):
        gate = jax.nn.sigmoid(jnp.dot(xb, wg_ref[:, i * D_MODEL:(i + 1) * D_MODEL],
                                      preferred_element_type=F32))
        term = gate * jnp.dot(br[...], wbr[...], preferred_element_type=F32)
        comb = term if comb is None else comb + term
    out = jnp.dot(comb.astype(BF16), wo_ref[...], preferred_element_type=F32)
    y = _layer_norm(DN_ALPHA * x + out, g_ref[...], b_ref[...])
    y_ref[...] = y

    y_hi = y.astype(BF16)
    y_lo = (y - y_hi.astype(F32)).astype(BF16)
    both = jnp.dot(y_hi, wr2_ref[...], preferred_element_type=F32)
    logits = (both[:, :N_EXPERTS] + both[:, N_EXPERTS:]
              + jnp.dot(y_lo, wrh_ref[...], preferred_element_type=F32) + br_ref[...])

    rows = logits.shape[0]
    lane = lax.broadcasted_iota(I32, (rows, N_EXPERTS), 1).astype(F32)
    lane_out = lax.broadcasted_iota(I32, (rows, V7X_LANES), 1)
    work = logits
    sel = jnp.zeros((rows, N_EXPERTS), F32)
    topi = jnp.zeros((rows, V7X_LANES), I32)
    vals = []
    for k in range(TOP_K):
        top = jnp.max(work, axis=-1, keepdims=True)
        idx = jnp.min(jnp.where(work == top, lane, float(N_EXPERTS)), axis=-1, keepdims=True)
        hit = lane == idx
        sel = jnp.where(hit, 1.0, sel)
        work = jnp.where(hit, -jnp.inf, work)
        topi = jnp.where(lane_out == k, idx.astype(I32), topi)
        vals.append(top)
    exps = [jnp.exp(v - vals[0]) for v in vals]
    denom = exps[0] + exps[1] + exps[2] + exps[3]
    topg = jnp.zeros((rows, V7X_LANES), F32)
    for k in range(TOP_K):
        topg = jnp.where(lane_out == k, exps[k] / denom, topg)
    sel_ref[...] = sel.astype(BF16)
    topi_ref[...] = topi
    topg_ref[...] = topg


def _merge(x2d, a, c, m, weights, *, rows):
    n = x2d.shape[0]
    wide = pl.BlockSpec((rows, D_MODEL), lambda i: (i, 0))
    narrow = pl.BlockSpec((rows, WIDTH), lambda i: (i, 0))
    lanes = pl.BlockSpec((rows, V7X_LANES), lambda i: (i, 0))
    return pl.pallas_call(
        _merge_kernel,
        grid=(n // rows,),
        in_specs=[wide, narrow, narrow, narrow] + [_resident(w.shape) for w in weights],
        out_specs=[wide, pl.BlockSpec((rows, N_EXPERTS), lambda i: (i, 0)), lanes, lanes],
        out_shape=[jax.ShapeDtypeStruct((n, D_MODEL), F32),
                   jax.ShapeDtypeStruct((n, N_EXPERTS), BF16),
                   jax.ShapeDtypeStruct((n, V7X_LANES), I32),
                   jax.ShapeDtypeStruct((n, V7X_LANES), F32)],
        compiler_params=_params("arbitrary"),
        name="merge_router",
    )(x2d, a, c, m, *weights)


def _rank_kernel(sel_ref, topi_ref, init_ref, lpos_ref, before_ref, cnt_ref, carry_ref):
    @pl.when(pl.program_id(0) == 0)
    def _():
        carry_ref[...] = init_ref[...]

    before_ref[0] = carry_ref[...]
    sel = sel_ref[...]
    rows = sel.shape[0]
    r = lax.broadcasted_iota(I32, (rows, rows), 0)
    c = lax.broadcasted_iota(I32, (rows, rows), 1)
    earlier = jnp.where(c < r, 1.0, 0.0).astype(BF16)
    in_tile = jnp.dot(earlier, sel, preferred_element_type=F32)
    n_run = jnp.sum(sel.astype(F32), axis=0, keepdims=True)
    run_rows = jnp.ceil(n_run * (1.0 / RUN_CHUNK)) * RUN_CHUNK
    er = lax.broadcasted_iota(I32, (N_EXPERTS, N_EXPERTS), 0)
    ec = lax.broadcasted_iota(I32, (N_EXPERTS, N_EXPERTS), 1)
    lower_experts = jnp.where(er < ec, 1.0, 0.0).astype(BF16)
    local0 = jnp.dot(jnp.broadcast_to(run_rows, (8, N_EXPERTS)).astype(BF16), lower_experts,
                     preferred_element_type=F32)[0:1]
    pos = in_tile + local0
    lane = lax.broadcasted_iota(I32, (rows, N_EXPERTS), 1)
    lane_out = lax.broadcasted_iota(I32, (rows, V7X_LANES), 1)
    topi = topi_ref[...]
    lpos = jnp.full((rows, V7X_LANES), -1, I32)
    for k in range(TOP_K):
        mine = jnp.sum(jnp.where(lane == topi[:, k:k + 1], pos, 0.0), axis=-1, keepdims=True)
        lpos = jnp.where(lane_out == k, mine.astype(I32), lpos)
    lpos_ref[...] = lpos
    total = carry_ref[...] + n_run
    carry_ref[...] = total
    cnt_ref[...] = total


def _ranks(sel, topi, init, *, rows):
    n = sel.shape[0]
    small = pl.BlockSpec((8, N_EXPERTS), lambda i: (0, 0))
    lanes = pl.BlockSpec((rows, V7X_LANES), lambda i: (i, 0))
    return pl.pallas_call(
        _rank_kernel,
        grid=(n // rows,),
        in_specs=[pl.BlockSpec((rows, N_EXPERTS), lambda i: (i, 0)), lanes, small],
        out_specs=[lanes, pl.BlockSpec((1, 8, N_EXPERTS), lambda i: (i, 0, 0)), small],
        out_shape=[jax.ShapeDtypeStruct((n, V7X_LANES), I32),
                   jax.ShapeDtypeStruct((n // rows, 8, N_EXPERTS), F32),
                   jax.ShapeDtypeStruct((8, N_EXPERTS), F32)],
        scratch_shapes=[pltpu.VMEM((8, N_EXPERTS), F32)],
        compiler_params=_params("arbitrary"),
        name="expert_ranks",
    )(sel, topi, init)


SCAL_SORTED0 = 0
SCAL_CHUNKS = N_EXPERTS
SCAL_LOCAL0 = 2 * N_EXPERTS
SCAL_TOTAL = 3 * N_EXPERTS
SCAL_PREV_TOTAL = SCAL_TOTAL + 1


def _chunk_copy(local_ref, local_row, sorted_ref, sorted_row, sem, to_sorted):
    loc = local_ref.at[pl.ds(local_row, RUN_CHUNK)]
    srt = sorted_ref.at[pl.ds(sorted_row, RUN_CHUNK)]
    return pltpu.make_async_copy(loc, srt, sem) if to_sorted else pltpu.make_async_copy(srt, loc, sem)


def _start_runs(scal_ref, local_ref, sorted_ref, sem, *, to_sorted):
    def expert(e, carry):
        sorted0 = scal_ref[0, 0, SCAL_SORTED0 + e]
        local0 = scal_ref[0, 0, SCAL_LOCAL0 + e]

        def chunk(c, inner):
            _chunk_copy(local_ref, pl.multiple_of(local0 + c * RUN_CHUNK, RUN_CHUNK), sorted_ref,
                        pl.multiple_of(sorted0 + c * RUN_CHUNK, RUN_CHUNK), sem, to_sorted).start()
            return inner

        lax.fori_loop(0, scal_ref[0, 0, SCAL_CHUNKS + e], chunk, 0)
        return carry

    lax.fori_loop(0, N_EXPERTS, expert, 0)


def _wait_chunks(n_chunks, local_ref, sorted_ref, sem, *, to_sorted):
    def drain(c, carry):
        _chunk_copy(local_ref, 0, sorted_ref, 0, sem, to_sorted).wait()
        return carry

    lax.fori_loop(0, n_chunks, drain, 0)


def _dispatch_kernel(scal_ref, lpos_ref, y_ref, *rest, local_rows):
    sorted_ref, stage, sems = rest[-3:]
    i = pl.program_id(0)
    slot = i % 2
    rows = y_ref.shape[0]
    yb = y_ref[...].astype(BF16)
    lpos = lpos_ref[0]
    blk = LOCAL_BLOCK
    for rb in range(local_rows // blk):
        j = lax.broadcasted_iota(I32, (blk, rows), 0) + rb * blk
        hit = (j == lpos[0:1]) | (j == lpos[1:2]) | (j == lpos[2:3]) | (j == lpos[3:4])
        onehot = jnp.where(hit, 1.0, 0.0).astype(BF16)
        stage[slot, rb * blk:(rb + 1) * blk, :] = jnp.dot(onehot, yb, preferred_element_type=F32)
    _start_runs(scal_ref, stage.at[slot], sorted_ref, sems.at[slot], to_sorted=True)

    @pl.when(i > 0)
    def _():
        _wait_chunks(scal_ref[0, 0, SCAL_PREV_TOTAL], stage.at[1 - slot], sorted_ref,
                     sems.at[1 - slot], to_sorted=True)

    @pl.when(i == pl.num_programs(0) - 1)
    def _():
        _wait_chunks(scal_ref[0, 0, SCAL_TOTAL], stage.at[slot], sorted_ref, sems.at[slot],
                     to_sorted=True)


def _local_rows(rows):
    worst = rows * TOP_K + N_EXPERTS * (RUN_CHUNK - 1)
    return (worst + LOCAL_BLOCK - 1) // LOCAL_BLOCK * LOCAL_BLOCK


def _dispatch(scal, lpos_t, y1, sorted_rows, *, rows, n_sorted):
    n = y1.shape[0]
    local_rows = _local_rows(rows)
    any_spec = pl.BlockSpec(memory_space=pl.ANY)
    in_specs = [pl.BlockSpec((1, 1, V7X_LANES), lambda i: (i, 0, 0), memory_space=pltpu.SMEM),
                pl.BlockSpec((1, 8, rows), lambda i: (i, 0, 0)),
                pl.BlockSpec((rows, D_MODEL), lambda i: (i, 0))]
    args = [scal, lpos_t, y1]
    aliases = {}
    if sorted_rows is not None:
        in_specs.append(any_spec)
        args.append(sorted_rows)
        aliases = {3: 0}
    return pl.pallas_call(
        functools.partial(_dispatch_kernel, local_rows=local_rows),
        grid=(n // rows,),
        in_specs=in_specs,
        out_specs=any_spec,
        out_shape=jax.ShapeDtypeStruct((n_sorted, D_MODEL), F32),
        scratch_shapes=[pltpu.VMEM((2, local_rows, D_MODEL), F32), pltpu.SemaphoreType.DMA((2,))],
        input_output_aliases=aliases,
        compiler_params=_params("arbitrary"),
        name="dispatch_rows",
    )(*args)


def _expert_kernel(te_ref, tr_ref, x_ref, wgu_ref, bgu_ref, wd_ref, bd_ref, o_ref,
                   wgu_b, wd_b):
    i = pl.program_id(0)
    e = te_ref[i]
    changed = (i == 0) | (e != te_ref[jnp.maximum(i - 1, 0)])

    @pl.when(changed)
    def _():
        wgu_b[...] = wgu_ref[0].astype(BF16)
        wd_b[...] = wd_ref[0].astype(BF16)

    @pl.when(tr_ref[i] == 0)
    def _():
        o_ref[...] = jnp.zeros(o_ref.shape, o_ref.dtype)

    @pl.when(tr_ref[i] > 0)
    def _():
        row = lax.broadcasted_iota(I32, x_ref.shape, 0)
        xb = jnp.where(row < tr_ref[i], x_ref[...], 0.0).astype(BF16)
        bgu = bgu_ref[0]
        acc = None
        n_col = 256
        for j in range(D_FF // n_col):
            cg = slice(j * n_col, (j + 1) * n_col)
            cl = slice(D_FF + j * n_col, D_FF + (j + 1) * n_col)
            hg = jnp.dot(xb, wgu_b[:, cg], preferred_element_type=F32) + bgu[:, cg]
            hl = jnp.dot(xb, wgu_b[:, cl], preferred_element_type=F32) + bgu[:, cl]
            glu = jnp.minimum(hg, SWIGLU_LIMIT)
            lin = jnp.clip(hl, -SWIGLU_LIMIT, SWIGLU_LIMIT)
            act = glu * jax.nn.sigmoid(SWIGLU_ALPHA * glu) * (lin + 1.0)
            part = jnp.dot(act.astype(BF16), wd_b[cg, :], preferred_element_type=F32)
            acc = part if acc is None else acc + part
        o_ref[...] = acc + bd_ref[0]


def _experts(tile_expert, tile_rows, xs, w_gate_up, b_gate_up, w_down, b_down):
    n_tiles = xs.shape[0] // EXPERT_TILE
    grid_spec = pltpu.PrefetchScalarGridSpec(
        num_scalar_prefetch=2,
        grid=(n_tiles,),
        in_specs=[
            pl.BlockSpec((EXPERT_TILE, D_MODEL), lambda i, te, tr: (i, 0)),
            pl.BlockSpec((1, D_MODEL, 2 * D_FF), lambda i, te, tr: (te[i], 0, 0)),
            pl.BlockSpec((1, 1, 2 * D_FF), lambda i, te, tr: (te[i], 0, 0)),
            pl.BlockSpec((1, D_FF, D_MODEL), lambda i, te, tr: (te[i], 0, 0)),
            pl.BlockSpec((1, 1, D_MODEL), lambda i, te, tr: (te[i], 0, 0)),
        ],
        out_specs=pl.BlockSpec((EXPERT_TILE, D_MODEL), lambda i, te, tr: (i, 0)),
        scratch_shapes=[pltpu.VMEM((D_MODEL, 2 * D_FF), BF16), pltpu.VMEM((D_FF, D_MODEL), BF16)],
    )
    return pl.pallas_call(
        _expert_kernel,
        grid_spec=grid_spec,
        out_shape=jax.ShapeDtypeStruct(xs.shape, F32),
        compiler_params=_params("arbitrary"),
        name="expert_ffn",
    )(tile_expert, tile_rows, xs, w_gate_up, b_gate_up, w_down, b_down)


def _combine_kernel(scal_ref, next_scal_ref, lpos_ref, gate_ref, y_ref, g_ref, b_ref, eo_ref,
                    out_ref, buf, sems, *, local_rows):
    i = pl.program_id(0)
    slot = i % 2

    @pl.when(i == 0)
    def _():
        buf[...] = jnp.zeros(buf.shape, buf.dtype)
        _start_runs(scal_ref, buf.at[0], eo_ref, sems.at[0], to_sorted=False)

    @pl.when(i + 1 < pl.num_programs(0))
    def _():
        _start_runs(next_scal_ref, buf.at[1 - slot], eo_ref, sems.at[1 - slot], to_sorted=False)

    _wait_chunks(scal_ref[0, 0, SCAL_TOTAL], buf.at[slot], eo_ref, sems.at[slot], to_sorted=False)
    rows = y_ref.shape[0]
    lpos = lpos_ref[...]
    gate = gate_ref[...]
    blk = LOCAL_BLOCK
    moe = None
    for kb in range(local_rows // blk):
        lane = lax.broadcasted_iota(I32, (rows, blk), 1) + kb * blk
        w = jnp.zeros((rows, blk), F32)
        for k in range(TOP_K):
            w = jnp.where(lane == lpos[:, k:k + 1], gate[:, k:k + 1], w)
        part = jnp.dot(w.astype(BF16), buf[slot, kb * blk:(kb + 1) * blk, :].astype(BF16),
                       preferred_element_type=F32)
        moe = part if moe is None else moe + part
    out_ref[...] = _layer_norm(DN_ALPHA * y_ref[...] + moe, g_ref[...], b_ref[...])


def _combine(scal, lpos, gates, y1, ln_g, ln_b, expert_out, *, rows):
    n = y1.shape[0]
    local_rows = _local_rows(rows)
    wide = pl.BlockSpec((rows, D_MODEL), lambda i: (i, 0))
    lanes = pl.BlockSpec((rows, V7X_LANES), lambda i: (i, 0))
    n_steps = n // rows
    scalars = pl.BlockSpec((1, 1, V7X_LANES), lambda i: (i, 0, 0), memory_space=pltpu.SMEM)
    next_scalars = pl.BlockSpec((1, 1, V7X_LANES), lambda i: (jnp.minimum(i + 1, n_steps - 1), 0, 0),
                                memory_space=pltpu.SMEM)
    return pl.pallas_call(
        functools.partial(_combine_kernel, local_rows=local_rows),
        grid=(n_steps,),
        in_specs=[scalars, next_scalars, lanes, lanes, wide, _resident(ln_g.shape),
                  _resident(ln_b.shape), pl.BlockSpec(memory_space=pl.ANY)],
        out_specs=wide,
        out_shape=jax.ShapeDtypeStruct((n, D_MODEL), F32),
        scratch_shapes=[pltpu.VMEM((2, local_rows, D_MODEL), F32), pltpu.SemaphoreType.DMA((2,))],
        compiler_params=_params("arbitrary"),
        name="combine_norm",
    )(scal, scal, lpos, gates, y1, ln_g, ln_b, expert_out)


def _tile_tables(lpos, n_chunk, sorted0, rows):
    n_tiles = lpos.shape[0] // rows
    local0 = (jnp.cumsum(n_chunk, axis=1) - n_chunk) * RUN_CHUNK
    total = jnp.sum(n_chunk, axis=1, keepdims=True)
    prev_total = jnp.concatenate([jnp.zeros((1, 1), I32), total[:-1]], axis=0)
    pad = jnp.zeros((n_tiles, V7X_LANES - SCAL_PREV_TOTAL - 1), I32)
    scal = jnp.concatenate([sorted0, n_chunk, local0, total, prev_total, pad], axis=1)
    lpos_t = jnp.transpose(lpos[:, :8].reshape(n_tiles, rows, 8), (0, 2, 1))
    return scal.reshape(n_tiles, 1, V7X_LANES), lpos_t


def kernel(x_prompt, x_sample, mem_prompt, cache_attn_k, cache_attn_v, cache_conv, cache_mem_k,
           cache_mem_v, w_in, rel_bias, conv_w, w_mem_kv, w_gate, w_br_attn, w_br_conv, w_br_mem,
           w_out, ln1_g, ln1_b, w_router, b_router, w_gate_up, b_gate_up, w_down, b_down,
           ln2_g, ln2_b):
    depth = w_in.shape[0]
    assert depth == 1, "one layer only"
    batch, seq, _ = x_prompt.shape
    dec_batch, dec_seq, _ = x_sample.shape
    n_prompt = batch * seq
    n_sample = dec_batch * dec_seq
    assert seq % ROW_TILE == 0 and HIST == ROW_TILE and cache_attn_k.shape[2] == HIST
    assert dec_seq % 16 == 0 and dec_seq <= CHUNK and cache_conv.shape[2] == 2

    w_in_b = w_in[0].astype(BF16)
    w_mem_b = w_mem_kv[0].astype(BF16)
    wr = w_router[0]
    wr_hi = wr.astype(BF16)
    wr_lo = (wr - wr_hi.astype(F32)).astype(BF16)
    merge_w = (w_gate[0].astype(BF16), w_br_attn[0].astype(BF16), w_br_conv[0].astype(BF16),
               w_br_mem[0].astype(BF16), w_out[0].astype(BF16), ln1_g, ln1_b,
               jnp.concatenate([wr_hi, wr_lo], axis=1), wr_hi, b_router)
    conv_w8 = jnp.pad(conv_w[0], ((0, 5), (0, 0)))
    bias = _band_bias(rel_bias[0])

    xp = x_prompt.reshape(n_prompt, D_MODEL)
    mk_f, mv_f, mk_b, mv_b = _memory_kv(mem_prompt.reshape(batch * MEM_TOKENS, D_MODEL), w_mem_b)
    q, k_pad, v_pad, c, qm, k_tail, v_tail, u_tail = _proj_in(
        xp, w_in_b, conv_w8, jnp.zeros((batch, 8, WIDTH), F32),
        batch=batch, seq=seq, rows=ROW_TILE, pad_steps=1)
    a, m = _attention(
        q, k_pad, v_pad, bias.reshape(ATT_HEADS * CHUNK, WINDOW), qm,
        mk_b.reshape(batch, MEM_TOKENS, WIDTH),
        mv_b.reshape(batch, MEM_TOKENS, WIDTH), batch=batch, seq=seq, rows=ROW_TILE,
        chunk_rows=CHUNK, valid_keys=BAND, first_chunk=0, mem_rows=128)
    y1_p, sel_p, topi_p, topg_p = _merge(xp, a, c, m, merge_w, rows=ROW_TILE)

    xs = x_sample.reshape(n_sample, D_MODEL)
    conv_init = jnp.pad(cache_conv[0], ((0, 0), (6, 0), (0, 0)))
    q_s, k_s, v_s, c_s, qm_s, k_new, v_new, u_tail_s = _proj_in(
        xs, w_in_b, conv_w8, conv_init, batch=dec_batch, seq=dec_seq, rows=dec_seq, pad_steps=0)
    pad_rows = WINDOW - HIST - dec_seq

    def window(cache, new):
        cache = cache[0].reshape(dec_batch, HIST, WIDTH).astype(BF16)
        return jnp.pad(jnp.concatenate([cache, new], axis=1), ((0, 0), (0, pad_rows), (0, 0)))

    a_s, m_s = _attention(
        q_s, window(cache_attn_k, k_s), window(cache_attn_v, v_s),
        bias[:, :dec_seq].reshape(ATT_HEADS * dec_seq, WINDOW), qm_s,
        cache_mem_k[0].reshape(dec_batch, MEM_TOKENS, WIDTH).astype(BF16),
        cache_mem_v[0].reshape(dec_batch, MEM_TOKENS, WIDTH).astype(BF16),
        batch=dec_batch, seq=dec_seq, rows=dec_seq, chunk_rows=dec_seq,
        valid_keys=HIST + dec_seq, first_chunk=LEFT_CHUNKS, mem_rows=dec_seq)
    y1_s, sel_s, topi_s, topg_s = _merge(xs, a_s, c_s, m_s, merge_w, rows=n_sample)

    lpos_p, before_p, cnt_p = _ranks(sel_p, topi_p, jnp.zeros((8, N_EXPERTS), F32), rows=ROW_TILE)
    lpos_s, before_s, cnt = _ranks(sel_s, topi_s, cnt_p, rows=n_sample)
    before = jnp.concatenate([before_p[:, 0], before_s[:, 0]], axis=0).astype(I32)
    n_run = jnp.concatenate([before[1:], cnt[:1].astype(I32)], axis=0) - before
    n_chunk = (n_run + RUN_CHUNK - 1) // RUN_CHUNK
    run_rows = n_chunk * RUN_CHUNK
    counts = jnp.sum(run_rows, axis=0)
    padded = (counts + EXPERT_TILE - 1) // EXPERT_TILE * EXPERT_TILE
    ends = jnp.cumsum(padded)
    starts = ends - padded
    sorted0 = starts[None, :] + jnp.cumsum(run_rows, axis=0) - run_rows
    n_token_tiles = n_prompt // ROW_TILE + 1
    n_tiles = ((n_prompt + n_sample) * TOP_K
               + n_token_tiles * N_EXPERTS * (RUN_CHUNK - 1)) // EXPERT_TILE + N_EXPERTS
    tile_start = jnp.arange(n_tiles, dtype=I32) * EXPERT_TILE
    tile_expert = jnp.minimum(
        jnp.sum((ends[None, :] <= tile_start[:, None]).astype(I32), axis=1), N_EXPERTS - 1)
    tile_rows = jnp.clip(starts[tile_expert] + counts[tile_expert] - tile_start, 0, EXPERT_TILE)
    scal_p, lpos_tp = _tile_tables(lpos_p, n_chunk[:-1], sorted0[:-1], ROW_TILE)
    scal_s, lpos_ts = _tile_tables(lpos_s, n_chunk[-1:], sorted0[-1:], n_sample)

    n_sorted = n_tiles * EXPERT_TILE
    sorted_rows = _dispatch(scal_p, lpos_tp, y1_p, None, rows=ROW_TILE, n_sorted=n_sorted)
    sorted_rows = _dispatch(scal_s, lpos_ts, y1_s, sorted_rows, rows=n_sample, n_sorted=n_sorted)
    expert_out = _experts(tile_expert, tile_rows, sorted_rows, w_gate_up[0],
                          b_gate_up[0].reshape(N_EXPERTS, 1, 2 * D_FF), w_down[0],
                          b_down[0].reshape(N_EXPERTS, 1, D_MODEL))
    y_p = _combine(scal_p, lpos_p, topg_p, y1_p, ln2_g, ln2_b, expert_out, rows=ROW_TILE)
    y_s = _combine(scal_s, lpos_s, topg_s, y1_s, ln2_g, ln2_b, expert_out, rows=n_sample)

    def heads(t, n_b, n_rows, n_heads, dim):
        return t.reshape(1, n_b, n_rows, n_heads, dim)

    return (y_p.reshape(batch, seq, D_MODEL),
            y_s.reshape(dec_batch, dec_seq, D_MODEL),
            heads(k_tail, batch, HIST, ATT_HEADS, ATT_DIM),
            heads(v_tail, batch, HIST, ATT_HEADS, ATT_DIM),
            u_tail[:, 6:8].reshape(1, batch, 2, WIDTH),
            heads(mk_f, batch, MEM_TOKENS, MEM_HEADS, MEM_DIM),
            heads(mv_f, batch, MEM_TOKENS, MEM_HEADS, MEM_DIM),
            heads(k_new, dec_batch, dec_seq, ATT_HEADS, ATT_DIM),
            heads(v_new, dec_batch, dec_seq, ATT_HEADS, ATT_DIM),
            u_tail_s[:, 6:8].reshape(1, dec_batch, 2, WIDTH))
```

```python
import functools

import jax
import jax.numpy as jnp
from jax import lax
from jax.experimental import pallas as pl
from jax.experimental.pallas import tpu as pltpu

F32 = jnp.float32
BF16 = jnp.bfloat16
I32 = jnp.int32

D_MODEL = 1024
CHUNK = 64
LEFT_CHUNKS = 8
HIST = LEFT_CHUNKS * CHUNK
BAND = HIST + CHUNK
ATT_HEADS = 8
ATT_DIM = 64
REL_MAX = 128
WIDTH = 512
MEM_TOKENS = 256
MEM_HEADS = 4
MEM_DIM = 128
N_EXPERTS = 32
TOP_K = 4
D_FF = 1024
SWIGLU_LIMIT = 7.0
SWIGLU_ALPHA = 1.702
DN_ALPHA = 2.0 ** 0.25
LN_EPS = 1e-5
NEG_INF = -1e30

V7X_LANES = 128
V7X_VMEM_BYTES = 64 * 1024 * 1024
VMEM_LIMIT = V7X_VMEM_BYTES - 8 * 1024 * 1024

ROW_TILE = 512
MERGE_TILE = 1024
WINDOW = BAND + CHUNK
EXPERT_TILE = 512
RUN_CHUNK = 8
LOCAL_BLOCK = 256


def _params(*sem):
    return pltpu.CompilerParams(dimension_semantics=sem, vmem_limit_bytes=VMEM_LIMIT)


def _resident(shape):
    nd = len(shape)
    return pl.BlockSpec(shape, lambda *_: (0,) * nd, pipeline_mode=pl.Buffered(1))


def _layer_norm(r, g, b):
    mu = jnp.mean(r, axis=-1, keepdims=True)
    d = r - mu
    var = jnp.mean(d * d, axis=-1, keepdims=True)
    return d * lax.rsqrt(var + LN_EPS) * g + b


def _memkv_kernel(x_ref, w_ref, kf_ref, vf_ref, kb_ref, vb_ref):
    y = jnp.dot(x_ref[...].astype(BF16), w_ref[...], preferred_element_type=F32)
    k = y[:, :WIDTH]
    v = y[:, WIDTH:]
    kf_ref[...] = k
    vf_ref[...] = v
    kb_ref[...] = k.astype(BF16)
    vb_ref[...] = v.astype(BF16)


def _memory_kv(mem2d, w_b):
    rows = mem2d.shape[0]
    tile = pl.BlockSpec((ROW_TILE, D_MODEL), lambda i: (i, 0))
    half = pl.BlockSpec((ROW_TILE, WIDTH), lambda i: (i, 0))
    return pl.pallas_call(
        _memkv_kernel,
        grid=(rows // ROW_TILE,),
        in_specs=[tile, _resident((D_MODEL, 2 * WIDTH))],
        out_specs=[half, half, half, half],
        out_shape=[jax.ShapeDtypeStruct((rows, WIDTH), F32)] * 2
        + [jax.ShapeDtypeStruct((rows, WIDTH), BF16)] * 2,
        compiler_params=_params("arbitrary"),
        name="memory_kv",
    )(mem2d, w_b)


def _proj_in_kernel(x_ref, w_ref, cw_ref, cinit_ref,
                    q_ref, k_ref, v_ref, c_ref, qm_ref, kt_ref, vt_ref, ut_ref,
                    carry_ref, *, pad_steps):
    s = pl.program_id(1)
    ns = pl.num_programs(1)
    rows = x_ref.shape[0]
    first = pad_steps
    last = ns - 1 - pad_steps

    if pad_steps:
        @pl.when((s < first) | (s > last))
        def _():
            k_ref[...] = jnp.zeros(k_ref.shape, k_ref.dtype)
            v_ref[...] = jnp.zeros(v_ref.shape, v_ref.dtype)

    @pl.when((s >= first) & (s <= last))
    def _():
        @pl.when(s == first)
        def _():
            carry_ref[...] = cinit_ref[0]

        xb = x_ref[...].astype(BF16)

        def proj(g):
            return jnp.dot(xb, w_ref[:, g * WIDTH:(g + 1) * WIDTH], preferred_element_type=F32)

        q_ref[...] = (proj(0) * (ATT_DIM ** -0.5)).astype(BF16)
        k = proj(1)
        v = proj(2)
        k_ref[0] = k.astype(BF16)
        v_ref[0] = v.astype(BF16)

        @pl.when(s == last)
        def _():
            kt_ref[0] = k
            vt_ref[0] = v

        bg = proj(3)
        u = proj(4) * proj(5)
        prev = carry_ref[...]
        row = lax.broadcasted_iota(I32, u.shape, 0)
        u1 = jnp.where(row == 0, prev[7:8], pltpu.roll(u, 1, 0))
        u2 = jnp.where(row == 0, prev[6:7], jnp.where(row == 1, prev[7:8], pltpu.roll(u, 2, 0)))
        cw = cw_ref[...]
        c_ref[...] = (bg * (cw[0:1] * u2 + cw[1:2] * u1 + cw[2:3] * u)).astype(BF16)
        tail = u[rows - 8:]
        carry_ref[...] = tail

        @pl.when(s == last)
        def _():
            ut_ref[0] = tail

        qm_ref[...] = proj(6).astype(BF16)


def _proj_in(x2d, w_b, conv_w8, conv_init, *, batch, seq, rows, pad_steps):
    n_data = seq // rows
    n_steps = n_data + 2 * pad_steps

    def data_idx(b, s):
        return (b * n_data + jnp.clip(s - pad_steps, 0, n_data - 1), 0)

    wide = pl.BlockSpec((rows, D_MODEL), data_idx)
    narrow = pl.BlockSpec((rows, WIDTH), data_idx)
    seq_blk = pl.BlockSpec((1, rows, WIDTH), lambda b, s: (b, s, 0))
    tail_blk = pl.BlockSpec((1, rows, WIDTH), lambda b, s: (b, 0, 0))
    tail8 = pl.BlockSpec((1, 8, WIDTH), lambda b, s: (b, 0, 0))
    tok = jax.ShapeDtypeStruct((batch * seq, WIDTH), BF16)
    kv = jax.ShapeDtypeStruct((batch, n_steps * rows, WIDTH), BF16)
    tail = jax.ShapeDtypeStruct((batch, rows, WIDTH), F32)
    return pl.pallas_call(
        functools.partial(_proj_in_kernel, pad_steps=pad_steps),
        grid=(batch, n_steps),
        in_specs=[wide, _resident(w_b.shape), _resident(conv_w8.shape), tail8],
        out_specs=[narrow, seq_blk, seq_blk, narrow, narrow, tail_blk, tail_blk, tail8],
        out_shape=[tok, kv, kv, tok, tok, tail, tail,
                   jax.ShapeDtypeStruct((batch, 8, WIDTH), F32)],
        scratch_shapes=[pltpu.VMEM((8, WIDTH), F32)],
        compiler_params=_params("arbitrary", "arbitrary"),
        name="proj_in",
    )(x2d, w_b, conv_w8, conv_init)


def _bias_kernel(tab_ref, out_ref):
    shape = (CHUNK, WINDOW)
    i = lax.broadcasted_iota(I32, shape, 0)
    j = lax.broadcasted_iota(I32, shape, 1)
    idx = jnp.clip(HIST + i - j, -REL_MAX, REL_MAX) + REL_MAX
    for h in range(ATT_HEADS):
        def body(d, acc, h=h):
            return jnp.where(idx == d, tab_ref[h, d], acc)
        out_ref[h] = lax.fori_loop(0, 2 * REL_MAX + 1, body, jnp.zeros(shape, F32))


def _band_bias(rel_table):
    return pl.pallas_call(
        _bias_kernel,
        in_specs=[pl.BlockSpec(memory_space=pltpu.SMEM)],
        out_shape=jax.ShapeDtypeStruct((ATT_HEADS, CHUNK, WINDOW), F32),
        name="band_bias",
    )(rel_table)


def _softmax_pv(sc, v):
    m = jnp.max(sc, axis=-1, keepdims=True)
    p = jnp.exp(sc - m)
    l = jnp.sum(p, axis=-1, keepdims=True)
    return jnp.dot(p.astype(BF16), v, preferred_element_type=F32) / l


def _attention_kernel(q_ref, k_ref, v_ref, bias_ref, qm_ref, mk_ref, mv_ref,
                      a_ref, m_ref, *, chunk_rows, n_chunks, valid_keys, first_chunk, mem_rows):
    s = pl.program_id(1)
    nr = chunk_rows
    pair_w = 2 * ATT_DIM
    col = lax.broadcasted_iota(I32, (ATT_HEADS * nr, WINDOW), 1)
    lane2 = lax.broadcasted_iota(I32, (2 * nr, pair_w), 1)
    row2 = lax.broadcasted_iota(I32, (2 * nr, pair_w), 0)
    own = (lane2 >= ATT_DIM) == (row2 >= nr)
    low_lanes = lax.broadcasted_iota(I32, (nr, pair_w), 1) < ATT_DIM
    nt = (((1,), (1,)), ((), ()))

    def chunk(c, carry):
        g = first_chunk + s * n_chunks + c
        r0 = pl.multiple_of(c * nr, nr)
        w0 = pl.multiple_of((s * n_chunks + c) * CHUNK, CHUNK)
        qc = q_ref[pl.ds(r0, nr), :].astype(F32)
        kw = k_ref[0, pl.ds(w0, WINDOW), :]
        vw = v_ref[0, pl.ds(w0, WINDOW), :]
        scores = []
        for pair in range(ATT_HEADS // 2):
            lanes = slice(pair * pair_w, (pair + 1) * pair_w)
            qp = qc[:, lanes]
            q2 = jnp.where(own, jnp.concatenate([qp, qp], axis=0), 0.0).astype(BF16)
            scores.append(lax.dot_general(q2, kw[:, lanes], nt, preferred_element_type=F32))
        sc = jnp.concatenate(scores, axis=0) + bias_ref[...]
        sc = jnp.where((col >= HIST - CHUNK * g) & (col < valid_keys), sc, NEG_INF)
        m = jnp.max(sc, axis=-1, keepdims=True)
        p = jnp.exp(sc - m)
        inv = 1.0 / jnp.sum(p, axis=-1, keepdims=True)
        pb = p.astype(BF16)
        outs = []
        for pair in range(ATT_HEADS // 2):
            lanes = slice(pair * pair_w, (pair + 1) * pair_w)
            rows = slice(pair * 2 * nr, (pair + 1) * 2 * nr)
            o = jnp.dot(pb[rows], vw[:, lanes], preferred_element_type=F32) * inv[rows]
            outs.append(jnp.where(low_lanes, o[:nr], o[nr:]))
        a_ref[pl.ds(r0, nr), :] = jnp.concatenate(outs, axis=1).astype(BF16)
        return carry

    lax.fori_loop(0, n_chunks, chunk, 0)

    rows = qm_ref.shape[0]
    for rb in range(rows // mem_rows):
        rs = slice(rb * mem_rows, (rb + 1) * mem_rows)
        for h in range(MEM_HEADS):
            lanes = slice(h * MEM_DIM, (h + 1) * MEM_DIM)
            sc = lax.dot_general(qm_ref[rs, lanes], mk_ref[0, :, lanes],
                                 (((1,), (1,)), ((), ())), preferred_element_type=F32)
            o = _softmax_pv(sc * (MEM_DIM ** -0.5), mv_ref[0, :, lanes])
            m_ref[rs, lanes] = o.astype(BF16)


def _attention(q, k_pad, v_pad, bias, qm, mk, mv, *, batch, seq, rows, chunk_rows,
               valid_keys, first_chunk, mem_rows):
    n_steps = seq // rows
    tok = pl.BlockSpec((rows, WIDTH), lambda b, s: (b * n_steps + s, 0))
    whole_seq = pl.BlockSpec((1, k_pad.shape[1], WIDTH), lambda b, s: (b, 0, 0))
    mem = pl.BlockSpec((1, MEM_TOKENS, WIDTH), lambda b, s: (b, 0, 0))
    kern = functools.partial(
        _attention_kernel, chunk_rows=chunk_rows, n_chunks=rows // chunk_rows,
        valid_keys=valid_keys, first_chunk=first_chunk, mem_rows=mem_rows)
    out = jax.ShapeDtypeStruct((batch * seq, WIDTH), BF16)
    return pl.pallas_call(
        kern,
        grid=(batch, n_steps),
        in_specs=[tok, whole_seq, whole_seq, _resident(bias.shape), tok, mem, mem],
        out_specs=[tok, tok],
        out_shape=[out, out],
        compiler_params=_params("arbitrary", "arbitrary"),
        name="attention",
    )(q, k_pad, v_pad, bias, qm, mk, mv)


def _merge_kernel(x_ref, a_ref, c_ref, m_ref, wg_ref, wa_ref, wc_ref, wm_ref, wo_ref,
                  g_ref, b_ref, wr2_ref, wrh_ref, br_ref,
                  y_ref, sel_ref, topi_ref, topg_ref):
    x = x_ref[...]
    xb = x.astype(BF16)
    comb = None
    for i, (br, wbr) in enumerate(((a_ref, wa_ref), (c_ref, wc_ref), (m_ref, wm_ref))):
        gate = jax.nn.sigmoid(jnp.dot(xb, wg_ref[:, i * D_MODEL:(i + 1) * D_MODEL],
                                      preferred_element_type=F32))
        term = gate * jnp.dot(br[...], wbr[...], preferred_element_type=F32)
        comb = term if comb is None else comb + term
    out = jnp.dot(comb.astype(BF16), wo_ref[...], preferred_element_type=F32)
    y = _layer_norm(DN_ALPHA * x + out, g_ref[...], b_ref[...])
    y_ref[...] = y

    y_hi = y.astype(BF16)
    y_lo = (y - y_hi.astype(F32)).astype(BF16)
    both = jnp.dot(y_hi, wr2_ref[...], preferred_element_type=F32)
    logits = (both[:, :N_EXPERTS] + both[:, N_EXPERTS:]
              + jnp.dot(y_lo, wrh_ref[...], preferred_element_type=F32) + br_ref[...])

    rows = logits.shape[0]
    lane = lax.broadcasted_iota(I32, (rows, N_EXPERTS), 1).astype(F32)
    lane_out = lax.broadcasted_iota(I32, (rows, V7X_LANES), 1)
    work = logits
    sel = jnp.zeros((rows, N_EXPERTS), F32)
    topi = jnp.zeros((rows, V7X_LANES), I32)
    vals = []
    for k in range(TOP_K):
        top = jnp.max(work, axis=-1, keepdims=True)
        idx = jnp.min(jnp.where(work == top, lane, float(N_EXPERTS)), axis=-1, keepdims=True)
        hit = lane == idx
        sel = jnp.where(hit, 1.0, sel)
        work = jnp.where(hit, -jnp.inf, work)
        topi = jnp.where(lane_out == k, idx.astype(I32), topi)
        vals.append(top)
    exps = [jnp.exp(v - vals[0]) for v in vals]
    denom = exps[0] + exps[1] + exps[2] + exps[3]
    topg = jnp.zeros((rows, V7X_LANES), F32)
    for k in range(TOP_K):
        topg = jnp.where(lane_out == k, exps[k] / denom, topg)
    sel_ref[...] = sel.astype(BF16)
    topi_ref[...] = topi
    topg_ref[...] = topg


def _merge(x2d, a, c, m, weights, *, rows):
    n = x2d.shape[0]
    wide = pl.BlockSpec((rows, D_MODEL), lambda i: (i, 0))
    narrow = pl.BlockSpec((rows, WIDTH), lambda i: (i, 0))
    lanes = pl.BlockSpec((rows, V7X_LANES), lambda i: (i, 0))
    return pl.pallas_call(
        _merge_kernel,
        grid=(n // rows,),
        in_specs=[wide, narrow, narrow, narrow] + [_resident(w.shape) for w in weights],
        out_specs=[wide, pl.BlockSpec((rows, N_EXPERTS), lambda i: (i, 0)), lanes, lanes],
        out_shape=[jax.ShapeDtypeStruct((n, D_MODEL), F32),
                   jax.ShapeDtypeStruct((n, N_EXPERTS), BF16),
                   jax.ShapeDtypeStruct((n, V7X_LANES), I32),
                   jax.ShapeDtypeStruct((n, V7X_LANES), F32)],
        compiler_params=_params("arbitrary"),
        name="merge_router",
    )(x2d, a, c, m, *weights)


def _rank_kernel(sel_ref, topi_ref, init_ref, lpos_ref, before_ref, cnt_ref, carry_ref):
    @pl.when(pl.program_id(0) == 0)
    def _():
        carry_ref[...] = init_ref[...]

    before_ref[0] = carry_ref[...]
    sel = sel_ref[...]
    rows = sel.shape[0]
    r = lax.broadcasted_iota(I32, (rows, rows), 0)
    c = lax.broadcasted_iota(I32, (rows, rows), 1)
    earlier = jnp.where(c < r, 1.0, 0.0).astype(BF16)
    in_tile = jnp.dot(earlier, sel, preferred_element_type=F32)
    n_run = jnp.sum(sel.astype(F32), axis=0, keepdims=True)
    run_rows = jnp.ceil(n_run * (1.0 / RUN_CHUNK)) * RUN_CHUNK
    er = lax.broadcasted_iota(I32, (N_EXPERTS, N_EXPERTS), 0)
    ec = lax.broadcasted_iota(I32, (N_EXPERTS, N_EXPERTS), 1)
    lower_experts = jnp.where(er < ec, 1.0, 0.0).astype(BF16)
    local0 = jnp.dot(jnp.broadcast_to(run_rows, (8, N_EXPERTS)).astype(BF16), lower_experts,
                     preferred_element_type=F32)[0:1]
    pos = in_tile + local0
    lane = lax.broadcasted_iota(I32, (rows, N_EXPERTS), 1)
    lane_out = lax.broadcasted_iota(I32, (rows, V7X_LANES), 1)
    topi = topi_ref[...]
    lpos = jnp.full((rows, V7X_LANES), -1, I32)
    for k in range(TOP_K):
        mine = jnp.sum(jnp.where(lane == topi[:, k:k + 1], pos, 0.0), axis=-1, keepdims=True)
        lpos = jnp.where(lane_out == k, mine.astype(I32), lpos)
    lpos_ref[...] = lpos
    total = carry_ref[...] + n_run
    carry_ref[...] = total
    cnt_ref[...] = total


def _ranks(sel, topi, init, *, rows):
    n = sel.shape[0]
    small = pl.BlockSpec((8, N_EXPERTS), lambda i: (0, 0))
    lanes = pl.BlockSpec((rows, V7X_LANES), lambda i: (i, 0))
    return pl.pallas_call(
        _rank_kernel,
        grid=(n // rows,),
        in_specs=[pl.BlockSpec((rows, N_EXPERTS), lambda i: (i, 0)), lanes, small],
        out_specs=[lanes, pl.BlockSpec((1, 8, N_EXPERTS), lambda i: (i, 0, 0)), small],
        out_shape=[jax.ShapeDtypeStruct((n, V7X_LANES), I32),
                   jax.ShapeDtypeStruct((n // rows, 8, N_EXPERTS), F32),
                   jax.ShapeDtypeStruct((8, N_EXPERTS), F32)],
        scratch_shapes=[pltpu.VMEM((8, N_EXPERTS), F32)],
        compiler_params=_params("arbitrary"),
        name="expert_ranks",
    )(sel, topi, init)


SCAL_SORTED0 = 0
SCAL_CHUNKS = N_EXPERTS
SCAL_LOCAL0 = 2 * N_EXPERTS
SCAL_TOTAL = 3 * N_EXPERTS
SCAL_PREV_TOTAL = SCAL_TOTAL + 2
BIG_CHUNKS = 8
COPY_ROWS = (BIG_CHUNKS * RUN_CHUNK, RUN_CHUNK)


def _run_copy(local_ref, local_row, sorted_ref, sorted_row, sems, size, to_sorted):
    loc = local_ref.at[pl.ds(local_row, COPY_ROWS[size])]
    srt = sorted_ref.at[pl.ds(sorted_row, COPY_ROWS[size])]
    sem = sems[size]
    return pltpu.make_async_copy(loc, srt, sem) if to_sorted else pltpu.make_async_copy(srt, loc, sem)


def _start_runs(scal_ref, local_ref, sorted_ref, sems, *, to_sorted):
    def expert(e, carry):
        sorted0 = scal_ref[0, 0, SCAL_SORTED0 + e]
        local0 = scal_ref[0, 0, SCAL_LOCAL0 + e]
        n_chunk = scal_ref[0, 0, SCAL_CHUNKS + e]
        n_big = n_chunk // BIG_CHUNKS
        rest0 = n_big * COPY_ROWS[0]

        def copy(size, offset):
            return _run_copy(local_ref, pl.multiple_of(local0 + offset, RUN_CHUNK), sorted_ref,
                             pl.multiple_of(sorted0 + offset, RUN_CHUNK), sems, size, to_sorted)

        def big(c, inner):
            copy(0, c * COPY_ROWS[0]).start()
            return inner

        def small(c, inner):
            copy(1, rest0 + c * COPY_ROWS[1]).start()
            return inner

        lax.fori_loop(0, n_big, big, 0)
        lax.fori_loop(0, n_chunk - n_big * BIG_CHUNKS, small, 0)
        return carry

    lax.fori_loop(0, N_EXPERTS, expert, 0)


def _wait_runs(scal_ref, total_lane, local_ref, sorted_ref, sems, *, to_sorted):
    for size in range(len(COPY_ROWS)):
        def drain(c, carry, size=size):
            _run_copy(local_ref, 0, sorted_ref, 0, sems, size, to_sorted).wait()
            return carry

        lax.fori_loop(0, scal_ref[0, 0, total_lane + size], drain, 0)


def _dispatch_kernel(scal_ref, lpos_ref, y_ref, *rest, local_rows):
    sorted_ref, stage, big_sems, small_sems = rest[-4:]
    i = pl.program_id(0)
    slot = i % 2
    sems = [(big_sems.at[s], small_sems.at[s]) for s in (slot, 1 - slot)]
    rows = y_ref.shape[0]
    yb = y_ref[...].astype(BF16)
    lpos = lpos_ref[0]
    blk = LOCAL_BLOCK
    for rb in range(local_rows // blk):
        j = lax.broadcasted_iota(I32, (blk, rows), 0) + rb * blk
        hit = (j == lpos[0:1]) | (j == lpos[1:2]) | (j == lpos[2:3]) | (j == lpos[3:4])
        onehot = jnp.where(hit, 1.0, 0.0).astype(BF16)
        stage[slot, rb * blk:(rb + 1) * blk, :] = jnp.dot(onehot, yb, preferred_element_type=F32)
    _start_runs(scal_ref, stage.at[slot], sorted_ref, sems[0], to_sorted=True)

    @pl.when(i > 0)
    def _():
        _wait_runs(scal_ref, SCAL_PREV_TOTAL, stage.at[1 - slot], sorted_ref, sems[1],
                   to_sorted=True)

    @pl.when(i == pl.num_programs(0) - 1)
    def _():
        _wait_runs(scal_ref, SCAL_TOTAL, stage.at[slot], sorted_ref, sems[0], to_sorted=True)


def _local_rows(rows):
    worst = rows * TOP_K + N_EXPERTS * (RUN_CHUNK - 1)
    return (worst + LOCAL_BLOCK - 1) // LOCAL_BLOCK * LOCAL_BLOCK


def _dispatch(scal, lpos_t, y1, sorted_rows, *, rows, n_sorted):
    n = y1.shape[0]
    local_rows = _local_rows(rows)
    any_spec = pl.BlockSpec(memory_space=pl.ANY)
    in_specs = [pl.BlockSpec((1, 1, V7X_LANES), lambda i: (i, 0, 0), memory_space=pltpu.SMEM),
                pl.BlockSpec((1, 8, rows), lambda i: (i, 0, 0)),
                pl.BlockSpec((rows, D_MODEL), lambda i: (i, 0))]
    args = [scal, lpos_t, y1]
    aliases = {}
    if sorted_rows is not None:
        in_specs.append(any_spec)
        args.append(sorted_rows)
        aliases = {3: 0}
    return pl.pallas_call(
        functools.partial(_dispatch_kernel, local_rows=local_rows),
        grid=(n // rows,),
        in_specs=in_specs,
        out_specs=any_spec,
        out_shape=jax.ShapeDtypeStruct((n_sorted, D_MODEL), F32),
        scratch_shapes=[pltpu.VMEM((2, local_rows, D_MODEL), F32),
                        pltpu.SemaphoreType.DMA((2,)), pltpu.SemaphoreType.DMA((2,))],
        input_output_aliases=aliases,
        compiler_params=_params("arbitrary"),
        name="dispatch_rows",
    )(*args)


def _expert_kernel(te_ref, tr_ref, x_ref, wgu_ref, bgu_ref, wd_ref, bd_ref, o_ref,
                   wgu_b, wd_b):
    i = pl.program_id(0)
    e = te_ref[i]
    changed = (i == 0) | (e != te_ref[jnp.maximum(i - 1, 0)])

    @pl.when(changed)
    def _():
        wgu_b[...] = wgu_ref[0].astype(BF16)
        wd_b[...] = wd_ref[0].astype(BF16)

    @pl.when(tr_ref[i] == 0)
    def _():
        o_ref[...] = jnp.zeros(o_ref.shape, o_ref.dtype)

    @pl.when(tr_ref[i] > 0)
    def _():
        row = lax.broadcasted_iota(I32, x_ref.shape, 0)
        xb = jnp.where(row < tr_ref[i], x_ref[...], 0.0).astype(BF16)
        bgu = bgu_ref[0]
        acc = None
        n_col = 256
        for j in range(D_FF // n_col):
            cg = slice(j * n_col, (j + 1) * n_col)
            cl = slice(D_FF + j * n_col, D_FF + (j + 1) * n_col)
            hg = jnp.dot(xb, wgu_b[:, cg], preferred_element_type=F32) + bgu[:, cg]
            hl = jnp.dot(xb, wgu_b[:, cl], preferred_element_type=F32) + bgu[:, cl]
            glu = jnp.minimum(hg, SWIGLU_LIMIT)
            lin = jnp.clip(hl, -SWIGLU_LIMIT, SWIGLU_LIMIT)
            act = glu * jax.nn.sigmoid(SWIGLU_ALPHA * glu) * (lin + 1.0)
            part = jnp.dot(act.astype(BF16), wd_b[cg, :], preferred_element_type=F32)
            acc = part if acc is None else acc + part
        o_ref[...] = acc + bd_ref[0]


def _experts(tile_expert, tile_rows, xs, w_gate_up, b_gate_up, w_down, b_down):
    n_tiles = xs.shape[0] // EXPERT_TILE
    grid_spec = pltpu.PrefetchScalarGridSpec(
        num_scalar_prefetch=2,
        grid=(n_tiles,),
        in_specs=[
            pl.BlockSpec((EXPERT_TILE, D_MODEL), lambda i, te, tr: (i, 0)),
            pl.BlockSpec((1, D_MODEL, 2 * D_FF), lambda i, te, tr: (te[i], 0, 0)),
            pl.BlockSpec((1, 1, 2 * D_FF), lambda i, te, tr: (te[i], 0, 0)),
            pl.BlockSpec((1, D_FF, D_MODEL), lambda i, te, tr: (te[i], 0, 0)),
            pl.BlockSpec((1, 1, D_MODEL), lambda i, te, tr: (te[i], 0, 0)),
        ],
        out_specs=pl.BlockSpec((EXPERT_TILE, D_MODEL), lambda i, te, tr: (i, 0)),
        scratch_shapes=[pltpu.VMEM((D_MODEL, 2 * D_FF), BF16), pltpu.VMEM((D_FF, D_MODEL), BF16)],
    )
    return pl.pallas_call(
        _expert_kernel,
        grid_spec=grid_spec,
        out_shape=jax.ShapeDtypeStruct(xs.shape, F32),
        compiler_params=_params("arbitrary"),
        name="expert_ffn",
    )(tile_expert, tile_rows, xs, w_gate_up, b_gate_up, w_down, b_down)


def _combine_kernel(scal_ref, next_scal_ref, lpos_ref, gate_ref, y_ref, g_ref, b_ref, eo_ref,
                    out_ref, buf, big_sems, small_sems, *, local_rows):
    i = pl.program_id(0)
    slot = i % 2
    sems = [(big_sems.at[s], small_sems.at[s]) for s in (slot, 1 - slot)]

    @pl.when(i == 0)
    def _():
        buf[...] = jnp.zeros(buf.shape, buf.dtype)
        _start_runs(scal_ref, buf.at[0], eo_ref, (big_sems.at[0], small_sems.at[0]), to_sorted=False)

    @pl.when(i + 1 < pl.num_programs(0))
    def _():
        _start_runs(next_scal_ref, buf.at[1 - slot], eo_ref, sems[1], to_sorted=False)

    _wait_runs(scal_ref, SCAL_TOTAL, buf.at[slot], eo_ref, sems[0], to_sorted=False)
    rows = y_ref.shape[0]
    lpos = lpos_ref[...]
    gate = gate_ref[...]
    blk = LOCAL_BLOCK
    moe = None
    for kb in range(local_rows // blk):
        lane = lax.broadcasted_iota(I32, (rows, blk), 1) + kb * blk
        w = jnp.zeros((rows, blk), F32)
        for k in range(TOP_K):
            w = jnp.where(lane == lpos[:, k:k + 1], gate[:, k:k + 1], w)
        part = jnp.dot(w.astype(BF16), buf[slot, kb * blk:(kb + 1) * blk, :].astype(BF16),
                       preferred_element_type=F32)
        moe = part if moe is None else moe + part
    out_ref[...] = _layer_norm(DN_ALPHA * y_ref[...] + moe, g_ref[...], b_ref[...])


def _combine(scal, lpos, gates, y1, ln_g, ln_b, expert_out, *, rows):
    n = y1.shape[0]
    local_rows = _local_rows(rows)
    wide = pl.BlockSpec((rows, D_MODEL), lambda i: (i, 0))
    lanes = pl.BlockSpec((rows, V7X_LANES), lambda i: (i, 0))
    n_steps = n // rows
    scalars = pl.BlockSpec((1, 1, V7X_LANES), lambda i: (i, 0, 0), memory_space=pltpu.SMEM)
    next_scalars = pl.BlockSpec((1, 1, V7X_LANES), lambda i: (jnp.minimum(i + 1, n_steps - 1), 0, 0),
                                memory_space=pltpu.SMEM)
    return pl.pallas_call(
        functools.partial(_combine_kernel, local_rows=local_rows),
        grid=(n_steps,),
        in_specs=[scalars, next_scalars, lanes, lanes, wide, _resident(ln_g.shape),
                  _resident(ln_b.shape), pl.BlockSpec(memory_space=pl.ANY)],
        out_specs=wide,
        out_shape=jax.ShapeDtypeStruct((n, D_MODEL), F32),
        scratch_shapes=[pltpu.VMEM((2, local_rows, D_MODEL), F32),
                        pltpu.SemaphoreType.DMA((2,)), pltpu.SemaphoreType.DMA((2,))],
        compiler_params=_params("arbitrary"),
        name="combine_norm",
    )(scal, scal, lpos, gates, y1, ln_g, ln_b, expert_out)


def _tile_tables(lpos, n_chunk, sorted0, rows):
    n_tiles = lpos.shape[0] // rows
    local0 = (jnp.cumsum(n_chunk, axis=1) - n_chunk) * RUN_CHUNK
    n_big = n_chunk // BIG_CHUNKS
    total = jnp.stack([jnp.sum(n_big, axis=1), jnp.sum(n_chunk - n_big * BIG_CHUNKS, axis=1)], axis=1)
    prev_total = jnp.concatenate([jnp.zeros((1, 2), I32), total[:-1]], axis=0)
    pad = jnp.zeros((n_tiles, V7X_LANES - SCAL_PREV_TOTAL - 2), I32)
    scal = jnp.concatenate([sorted0, n_chunk, local0, total, prev_total, pad], axis=1)
    lpos_t = jnp.transpose(lpos[:, :8].reshape(n_tiles, rows, 8), (0, 2, 1))
    return scal.reshape(n_tiles, 1, V7X_LANES), lpos_t


def kernel(x_prompt, x_sample, mem_prompt, cache_attn_k, cache_attn_v, cache_conv, cache_mem_k,
           cache_mem_v, w_in, rel_bias, conv_w, w_mem_kv, w_gate, w_br_attn, w_br_conv, w_br_mem,
           w_out, ln1_g, ln1_b, w_router, b_router, w_gate_up, b_gate_up, w_down, b_down,
           ln2_g, ln2_b):
    depth = w_in.shape[0]
    assert depth == 1, "one layer only"
    batch, seq, _ = x_prompt.shape
    dec_batch, dec_seq, _ = x_sample.shape
    n_prompt = batch * seq
    n_sample = dec_batch * dec_seq
    assert seq % ROW_TILE == 0 and HIST == ROW_TILE and cache_attn_k.shape[2] == HIST
    assert dec_seq % 16 == 0 and dec_seq <= CHUNK and cache_conv.shape[2] == 2

    w_in_b = w_in[0].astype(BF16)
    w_mem_b = w_mem_kv[0].astype(BF16)
    wr = w_router[0]
    wr_hi = wr.astype(BF16)
    wr_lo = (wr - wr_hi.astype(F32)).astype(BF16)
    merge_w = (w_gate[0].astype(BF16), w_br_attn[0].astype(BF16), w_br_conv[0].astype(BF16),
               w_br_mem[0].astype(BF16), w_out[0].astype(BF16), ln1_g, ln1_b,
               jnp.concatenate([wr_hi, wr_lo], axis=1), wr_hi, b_router)
    conv_w8 = jnp.pad(conv_w[0], ((0, 5), (0, 0)))
    bias = _band_bias(rel_bias[0])

    xp = x_prompt.reshape(n_prompt, D_MODEL)
    mk_f, mv_f, mk_b, mv_b = _memory_kv(mem_prompt.reshape(batch * MEM_TOKENS, D_MODEL), w_mem_b)
    q, k_pad, v_pad, c, qm, k_tail, v_tail, u_tail = _proj_in(
        xp, w_in_b, conv_w8, jnp.zeros((batch, 8, WIDTH), F32),
        batch=batch, seq=seq, rows=ROW_TILE, pad_steps=1)
    a, m = _attention(
        q, k_pad, v_pad, bias.reshape(ATT_HEADS * CHUNK, WINDOW), qm,
        mk_b.reshape(batch, MEM_TOKENS, WIDTH),
        mv_b.reshape(batch, MEM_TOKENS, WIDTH), batch=batch, seq=seq, rows=ROW_TILE,
        chunk_rows=CHUNK, valid_keys=BAND, first_chunk=0, mem_rows=128)
    y1_p, sel_p, topi_p, topg_p = _merge(xp, a, c, m, merge_w, rows=MERGE_TILE)

    xs = x_sample.reshape(n_sample, D_MODEL)
    conv_init = jnp.pad(cache_conv[0], ((0, 0), (6, 0), (0, 0)))
    q_s, k_s, v_s, c_s, qm_s, k_new, v_new, u_tail_s = _proj_in(
        xs, w_in_b, conv_w8, conv_init, batch=dec_batch, seq=dec_seq, rows=dec_seq, pad_steps=0)
    pad_rows = WINDOW - HIST - dec_seq

    def window(cache, new):
        cache = cache[0].reshape(dec_batch, HIST, WIDTH).astype(BF16)
        return jnp.pad(jnp.concatenate([cache, new], axis=1), ((0, 0), (0, pad_rows), (0, 0)))

    a_s, m_s = _attention(
        q_s, window(cache_attn_k, k_s), window(cache_attn_v, v_s),
        bias[:, :dec_seq].reshape(ATT_HEADS * dec_seq, WINDOW), qm_s,
        cache_mem_k[0].reshape(dec_batch, MEM_TOKENS, WIDTH).astype(BF16),
        cache_mem_v[0].reshape(dec_batch, MEM_TOKENS, WIDTH).astype(BF16),
        batch=dec_batch, seq=dec_seq, rows=dec_seq, chunk_rows=dec_seq,
        valid_keys=HIST + dec_seq, first_chunk=LEFT_CHUNKS, mem_rows=dec_seq)
    y1_s, sel_s, topi_s, topg_s = _merge(xs, a_s, c_s, m_s, merge_w, rows=n_sample)

    lpos_p, before_p, cnt_p = _ranks(sel_p, topi_p, jnp.zeros((8, N_EXPERTS), F32), rows=ROW_TILE)
    lpos_s, before_s, cnt = _ranks(sel_s, topi_s, cnt_p, rows=n_sample)
    before = jnp.concatenate([before_p[:, 0], before_s[:, 0]], axis=0).astype(I32)
    n_run = jnp.concatenate([before[1:], cnt[:1].astype(I32)], axis=0) - before
    n_chunk = (n_run + RUN_CHUNK - 1) // RUN_CHUNK
    run_rows = n_chunk * RUN_CHUNK
    counts = jnp.sum(run_rows, axis=0)
    padded = (counts + EXPERT_TILE - 1) // EXPERT_TILE * EXPERT_TILE
    ends = jnp.cumsum(padded)
    starts = ends - padded
    sorted0 = starts[None, :] + jnp.cumsum(run_rows, axis=0) - run_rows
    n_token_tiles = n_prompt // ROW_TILE + 1
    n_tiles = ((n_prompt + n_sample) * TOP_K
               + n_token_tiles * N_EXPERTS * (RUN_CHUNK - 1)) // EXPERT_TILE + N_EXPERTS
    tile_start = jnp.arange(n_tiles, dtype=I32) * EXPERT_TILE
    tile_expert = jnp.minimum(
        jnp.sum((ends[None, :] <= tile_start[:, None]).astype(I32), axis=1), N_EXPERTS - 1)
    tile_rows = jnp.clip(starts[tile_expert] + counts[tile_expert] - tile_start, 0, EXPERT_TILE)
    scal_p, lpos_tp = _tile_tables(lpos_p, n_chunk[:-1], sorted0[:-1], ROW_TILE)
    scal_s, lpos_ts = _tile_tables(lpos_s, n_chunk[-1:], sorted0[-1:], n_sample)

    n_sorted = n_tiles * EXPERT_TILE
    sorted_rows = _dispatch(scal_p, lpos_tp, y1_p, None, rows=ROW_TILE, n_sorted=n_sorted)
    sorted_rows = _dispatch(scal_s, lpos_ts, y1_s, sorted_rows, rows=n_sample, n_sorted=n_sorted)
    expert_out = _experts(tile_expert, tile_rows, sorted_rows, w_gate_up[0],
                          b_gate_up[0].reshape(N_EXPERTS, 1, 2 * D_FF), w_down[0],
                          b_down[0].reshape(N_EXPERTS, 1, D_MODEL))
    y_p = _combine(scal_p, lpos_p, topg_p, y1_p, ln2_g, ln2_b, expert_out, rows=ROW_TILE)
    y_s = _combine(scal_s, lpos_s, topg_s, y1_s, ln2_g, ln2_b, expert_out, rows=n_sample)

    def heads(t, n_b, n_rows, n_heads, dim):
        return t.reshape(1, n_b, n_rows, n_heads, dim)

    return (y_p.reshape(batch, seq, D_MODEL),
            y_s.reshape(dec_batch, dec_seq, D_MODEL),
            heads(k_tail, batch, HIST, ATT_HEADS, ATT_DIM),
            heads(v_tail, batch, HIST, ATT_HEADS, ATT_DIM),
            u_tail[:, 6:8].reshape(1, batch, 2, WIDTH),
            heads(mk_f, batch, MEM_TOKENS, MEM_HEADS, MEM_DIM),
            heads(mv_f, batch, MEM_TOKENS, MEM_HEADS, MEM_DIM),
            heads(k_new, dec_batch, dec_seq, ATT_HEADS, ATT_DIM),
            heads(v_new, dec_batch, dec_seq, ATT_HEADS, ATT_DIM),
            u_tail_s[:, 6:8].reshape(1, dec_batch, 2, WIDTH))
```

```python
import functools

import jax
import jax.numpy as jnp
from jax import lax
from jax.experimental import pallas as pl
from jax.experimental.pallas import tpu as pltpu

F32 = jnp.float32
BF16 = jnp.bfloat16
I32 = jnp.int32

D_MODEL = 1024
CHUNK = 64
LEFT_CHUNKS = 8
HIST = LEFT_CHUNKS * CHUNK
BAND = HIST + CHUNK
ATT_HEADS = 8
ATT_DIM = 64
REL_MAX = 128
WIDTH = 512
MEM_TOKENS = 256
MEM_HEADS = 4
MEM_DIM = 128
N_EXPERTS = 32
TOP_K = 4
D_FF = 1024
SWIGLU_LIMIT = 7.0
SWIGLU_ALPHA = 1.702
DN_ALPHA = 2.0 ** 0.25
LN_EPS = 1e-5
NEG_INF = -1e30

V7X_LANES = 128
V7X_VMEM_BYTES = 64 * 1024 * 1024
VMEM_LIMIT = V7X_VMEM_BYTES - 8 * 1024 * 1024

ROW_TILE = 512
MERGE_TILE = 1024
WINDOW = BAND + CHUNK
EXPERT_TILE = 512
RUN_CHUNK = 8
LOCAL_BLOCK = 256


def _params(*sem):
    return pltpu.CompilerParams(dimension_semantics=sem, vmem_limit_bytes=VMEM_LIMIT)


def _resident(shape):
    nd = len(shape)
    return pl.BlockSpec(shape, lambda *_: (0,) * nd, pipeline_mode=pl.Buffered(1))


def _layer_norm(r, g, b):
    mu = jnp.mean(r, axis=-1, keepdims=True)
    d = r - mu
    var = jnp.mean(d * d, axis=-1, keepdims=True)
    return d * lax.rsqrt(var + LN_EPS) * g + b


def _memkv_kernel(x_ref, w_ref, kf_ref, vf_ref, kb_ref, vb_ref):
    y = jnp.dot(x_ref[...].astype(BF16), w_ref[...], preferred_element_type=F32)
    k = y[:, :WIDTH]
    v = y[:, WIDTH:]
    kf_ref[...] = k
    vf_ref[...] = v
    kb_ref[...] = k.astype(BF16)
    vb_ref[...] = v.astype(BF16)


def _memory_kv(mem2d, w_b):
    rows = mem2d.shape[0]
    tile = pl.BlockSpec((ROW_TILE, D_MODEL), lambda i: (i, 0))
    half = pl.BlockSpec((ROW_TILE, WIDTH), lambda i: (i, 0))
    return pl.pallas_call(
        _memkv_kernel,
        grid=(rows // ROW_TILE,),
        in_specs=[tile, _resident((D_MODEL, 2 * WIDTH))],
        out_specs=[half, half, half, half],
        out_shape=[jax.ShapeDtypeStruct((rows, WIDTH), F32)] * 2
        + [jax.ShapeDtypeStruct((rows, WIDTH), BF16)] * 2,
        compiler_params=_params("arbitrary"),
        name="memory_kv",
    )(mem2d, w_b)


def _proj_in_kernel(x_ref, w_ref, cw_ref, cinit_ref,
                    q_ref, k_ref, v_ref, c_ref, qm_ref, kt_ref, vt_ref, ut_ref,
                    carry_ref, *, pad_steps):
    s = pl.program_id(1)
    ns = pl.num_programs(1)
    rows = x_ref.shape[0]
    first = pad_steps
    last = ns - 1 - pad_steps

    if pad_steps:
        @pl.when((s < first) | (s > last))
        def _():
            k_ref[...] = jnp.zeros(k_ref.shape, k_ref.dtype)
            v_ref[...] = jnp.zeros(v_ref.shape, v_ref.dtype)

    @pl.when((s >= first) & (s <= last))
    def _():
        @pl.when(s == first)
        def _():
            carry_ref[...] = cinit_ref[0]

        xb = x_ref[...].astype(BF16)

        def proj(g):
            return jnp.dot(xb, w_ref[:, g * WIDTH:(g + 1) * WIDTH], preferred_element_type=F32)

        q_ref[...] = (proj(0) * (ATT_DIM ** -0.5)).astype(BF16)
        k = proj(1)
        v = proj(2)
        k_ref[0] = k.astype(BF16)
        v_ref[0] = v.astype(BF16)

        @pl.when(s == last)
        def _():
            kt_ref[0] = k
            vt_ref[0] = v

        bg = proj(3)
        u = proj(4) * proj(5)
        prev = carry_ref[...]
        row = lax.broadcasted_iota(I32, u.shape, 0)
        u1 = jnp.where(row == 0, prev[7:8], pltpu.roll(u, 1, 0))
        u2 = jnp.where(row == 0, prev[6:7], jnp.where(row == 1, prev[7:8], pltpu.roll(u, 2, 0)))
        cw = cw_ref[...]
        c_ref[...] = (bg * (cw[0:1] * u2 + cw[1:2] * u1 + cw[2:3] * u)).astype(BF16)
        tail = u[rows - 8:]
        carry_ref[...] = tail

        @pl.when(s == last)
        def _():
            ut_ref[0] = tail

        qm_ref[...] = proj(6).astype(BF16)


def _proj_in(x2d, w_b, conv_w8, conv_init, *, batch, seq, rows, pad_steps):
    n_data = seq // rows
    n_steps = n_data + 2 * pad_steps

    def data_idx(b, s):
        return (b * n_data + jnp.clip(s - pad_steps, 0, n_data - 1), 0)

    wide = pl.BlockSpec((rows, D_MODEL), data_idx)
    narrow = pl.BlockSpec((rows, WIDTH), data_idx)
    seq_blk = pl.BlockSpec((1, rows, WIDTH), lambda b, s: (b, s, 0))
    tail_blk = pl.BlockSpec((1, rows, WIDTH), lambda b, s: (b, 0, 0))
    tail8 = pl.BlockSpec((1, 8, WIDTH), lambda b, s: (b, 0, 0))
    tok = jax.ShapeDtypeStruct((batch * seq, WIDTH), BF16)
    kv = jax.ShapeDtypeStruct((batch, n_steps * rows, WIDTH), BF16)
    tail = jax.ShapeDtypeStruct((batch, rows, WIDTH), F32)
    return pl.pallas_call(
        functools.partial(_proj_in_kernel, pad_steps=pad_steps),
        grid=(batch, n_steps),
        in_specs=[wide, _resident(w_b.shape), _resident(conv_w8.shape), tail8],
        out_specs=[narrow, seq_blk, seq_blk, narrow, narrow, tail_blk, tail_blk, tail8],
        out_shape=[tok, kv, kv, tok, tok, tail, tail,
                   jax.ShapeDtypeStruct((batch, 8, WIDTH), F32)],
        scratch_shapes=[pltpu.VMEM((8, WIDTH), F32)],
        compiler_params=_params("arbitrary", "arbitrary"),
        name="proj_in",
    )(x2d, w_b, conv_w8, conv_init)


def _bias_kernel(tab_ref, out_ref):
    shape = (CHUNK, WINDOW)
    i = lax.broadcasted_iota(I32, shape, 0)
    j = lax.broadcasted_iota(I32, shape, 1)
    idx = jnp.clip(HIST + i - j, -REL_MAX, REL_MAX) + REL_MAX
    for h in range(ATT_HEADS):
        def body(d, acc, h=h):
            return jnp.where(idx == d, tab_ref[h, d], acc)
        out_ref[h] = lax.fori_loop(0, 2 * REL_MAX + 1, body, jnp.zeros(shape, F32))


def _band_bias(rel_table):
    return pl.pallas_call(
        _bias_kernel,
        in_specs=[pl.BlockSpec(memory_space=pltpu.SMEM)],
        out_shape=jax.ShapeDtypeStruct((ATT_HEADS, CHUNK, WINDOW), F32),
        name="band_bias",
    )(rel_table)


def _attention_kernel(q_ref, k_ref, v_ref, bias_ref, qm_ref, mk_ref, mv_ref,
                      a_ref, m_ref, *, chunk_rows, n_chunks, valid_keys, first_chunk, mem_rows):
    s = pl.program_id(1)
    nr = chunk_rows
    pair_w = 2 * ATT_DIM
    col = lax.broadcasted_iota(I32, (ATT_HEADS * nr, WINDOW), 1)
    lane2 = lax.broadcasted_iota(I32, (2 * nr, pair_w), 1)
    row2 = lax.broadcasted_iota(I32, (2 * nr, pair_w), 0)
    own = (lane2 >= ATT_DIM) == (row2 >= nr)
    low_lanes = lax.broadcasted_iota(I32, (nr, pair_w), 1) < ATT_DIM
    nt = (((1,), (1,)), ((), ()))

    def chunk(c, carry):
        g = first_chunk + s * n_chunks + c
        r0 = pl.multiple_of(c * nr, nr)
        w0 = pl.multiple_of((s * n_chunks + c) * CHUNK, CHUNK)
        qc = q_ref[pl.ds(r0, nr), :].astype(F32)
        kw = k_ref[0, pl.ds(w0, WINDOW), :]
        vw = v_ref[0, pl.ds(w0, WINDOW), :]
        scores = []
        for pair in range(ATT_HEADS // 2):
            lanes = slice(pair * pair_w, (pair + 1) * pair_w)
            qp = qc[:, lanes]
            q2 = jnp.where(own, jnp.concatenate([qp, qp], axis=0), 0.0).astype(BF16)
            scores.append(lax.dot_general(q2, kw[:, lanes], nt, preferred_element_type=F32))
        sc = jnp.concatenate(scores, axis=0) + bias_ref[...]
        sc = jnp.where((col >= HIST - CHUNK * g) & (col < valid_keys), sc, NEG_INF)
        m = jnp.max(sc, axis=-1, keepdims=True)
        p = jnp.exp(sc - m)
        inv = 1.0 / jnp.sum(p, axis=-1, keepdims=True)
        pb = p.astype(BF16)
        outs = []
        for pair in range(ATT_HEADS // 2):
            lanes = slice(pair * pair_w, (pair + 1) * pair_w)
            rows = slice(pair * 2 * nr, (pair + 1) * 2 * nr)
            o = jnp.dot(pb[rows], vw[:, lanes], preferred_element_type=F32) * inv[rows]
            outs.append(jnp.where(low_lanes, o[:nr], o[nr:]))
        a_ref[pl.ds(r0, nr), :] = jnp.concatenate(outs, axis=1).astype(BF16)
        return carry

    lax.fori_loop(0, n_chunks, chunk, 0, unroll=min(n_chunks, 2))

    rows = qm_ref.shape[0]
    for rb in range(rows // mem_rows):
        rs = slice(rb * mem_rows, (rb + 1) * mem_rows)
        scores = []
        for h in range(MEM_HEADS):
            lanes = slice(h * MEM_DIM, (h + 1) * MEM_DIM)
            scores.append(lax.dot_general(qm_ref[rs, lanes], mk_ref[0, :, lanes], nt,
                                          preferred_element_type=F32))
        sc = jnp.concatenate(scores, axis=0) * (MEM_DIM ** -0.5)
        m = jnp.max(sc, axis=-1, keepdims=True)
        p = jnp.exp(sc - m)
        inv = 1.0 / jnp.sum(p, axis=-1, keepdims=True)
        pb = p.astype(BF16)
        for h in range(MEM_HEADS):
            lanes = slice(h * MEM_DIM, (h + 1) * MEM_DIM)
            hs = slice(h * mem_rows, (h + 1) * mem_rows)
            o = jnp.dot(pb[hs], mv_ref[0, :, lanes], preferred_element_type=F32) * inv[hs]
            m_ref[rs, lanes] = o.astype(BF16)


def _attention(q, k_pad, v_pad, bias, qm, mk, mv, *, batch, seq, rows, chunk_rows,
               valid_keys, first_chunk, mem_rows):
    n_steps = seq // rows
    tok = pl.BlockSpec((rows, WIDTH), lambda b, s: (b * n_steps + s, 0))
    whole_seq = pl.BlockSpec((1, k_pad.shape[1], WIDTH), lambda b, s: (b, 0, 0))
    mem = pl.BlockSpec((1, MEM_TOKENS, WIDTH), lambda b, s: (b, 0, 0))
    kern = functools.partial(
        _attention_kernel, chunk_rows=chunk_rows, n_chunks=rows // chunk_rows,
        valid_keys=valid_keys, first_chunk=first_chunk, mem_rows=mem_rows)
    out = jax.ShapeDtypeStruct((batch * seq, WIDTH), BF16)
    return pl.pallas_call(
        kern,
        grid=(batch, n_steps),
        in_specs=[tok, whole_seq, whole_seq, _resident(bias.shape), tok, mem, mem],
        out_specs=[tok, tok],
        out_shape=[out, out],
        compiler_params=_params("arbitrary", "arbitrary"),
        name="attention",
    )(q, k_pad, v_pad, bias, qm, mk, mv)


def _merge_kernel(x_ref, a_ref, c_ref, m_ref, wg_ref, wa_ref, wc_ref, wm_ref, wo_ref,
                  g_ref, b_ref, wr2_ref, wrh_ref, br_ref,
                  y_ref, sel_ref, topi_ref, topg_ref):
    x = x_ref[...]
    xb = x.astype(BF16)
    branches = [(a_ref[...], wa_ref), (c_ref[...], wc_ref), (m_ref[...], wm_ref)]
    out = None
    n_col = 256
    for j in range(D_MODEL // n_col):
        cols = slice(j * n_col, (j + 1) * n_col)
        comb = None
        for i, (br, wbr) in enumerate(branches):
            gcols = slice(i * D_MODEL + j * n_col, i * D_MODEL + (j + 1) * n_col)
            gate = jax.nn.sigmoid(jnp.dot(xb, wg_ref[:, gcols], preferred_element_type=F32))
            term = gate * jnp.dot(br, wbr[:, cols], preferred_element_type=F32)
            comb = term if comb is None else comb + term
        part = jnp.dot(comb.astype(BF16), wo_ref[cols, :], preferred_element_type=F32)
        out = part if out is None else out + part
    y = _layer_norm(DN_ALPHA * x + out, g_ref[...], b_ref[...])
    y_ref[...] = y

    y_hi = y.astype(BF16)
    y_lo = (y - y_hi.astype(F32)).astype(BF16)
    both = jnp.dot(y_hi, wr2_ref[...], preferred_element_type=F32)
    logits = (both[:, :N_EXPERTS] + both[:, N_EXPERTS:]
              + jnp.dot(y_lo, wrh_ref[...], preferred_element_type=F32) + br_ref[...])

    rows = logits.shape[0]
    lane = lax.broadcasted_iota(I32, (rows, N_EXPERTS), 1).astype(F32)
    lane_out = lax.broadcasted_iota(I32, (rows, V7X_LANES), 1)
    work = logits
    sel = jnp.zeros((rows, N_EXPERTS), F32)
    topi = jnp.zeros((rows, V7X_LANES), I32)
    vals = []
    for k in range(TOP_K):
        top = jnp.max(work, axis=-1, keepdims=True)
        idx = jnp.min(jnp.where(work == top, lane, float(N_EXPERTS)), axis=-1, keepdims=True)
        hit = lane == idx
        sel = jnp.where(hit, 1.0, sel)
        work = jnp.where(hit, -jnp.inf, work)
        topi = jnp.where(lane_out == k, idx.astype(I32), topi)
        vals.append(top)
    exps = [jnp.exp(v - vals[0]) for v in vals]
    denom = exps[0] + exps[1] + exps[2] + exps[3]
    topg = jnp.zeros((rows, V7X_LANES), F32)
    for k in range(TOP_K):
        topg = jnp.where(lane_out == k, exps[k] / denom, topg)
    sel_ref[...] = sel.astype(BF16)
    topi_ref[...] = topi
    topg_ref[...] = topg


def _merge(x2d, a, c, m, weights, *, rows):
    n = x2d.shape[0]
    wide = pl.BlockSpec((rows, D_MODEL), lambda i: (i, 0))
    narrow = pl.BlockSpec((rows, WIDTH), lambda i: (i, 0))
    lanes = pl.BlockSpec((rows, V7X_LANES), lambda i: (i, 0))
    return pl.pallas_call(
        _merge_kernel,
        grid=(n // rows,),
        in_specs=[wide, narrow, narrow, narrow] + [_resident(w.shape) for w in weights],
        out_specs=[wide, pl.BlockSpec((rows, N_EXPERTS), lambda i: (i, 0)), lanes, lanes],
        out_shape=[jax.ShapeDtypeStruct((n, D_MODEL), F32),
                   jax.ShapeDtypeStruct((n, N_EXPERTS), BF16),
                   jax.ShapeDtypeStruct((n, V7X_LANES), I32),
                   jax.ShapeDtypeStruct((n, V7X_LANES), F32)],
        compiler_params=_params("arbitrary"),
        name="merge_router",
    )(x2d, a, c, m, *weights)


def _rank_kernel(sel_ref, topi_ref, init_ref, lpos_ref, before_ref, cnt_ref, carry_ref):
    @pl.when(pl.program_id(0) == 0)
    def _():
        carry_ref[...] = init_ref[...]

    before_ref[0] = carry_ref[...]
    sel = sel_ref[...]
    rows = sel.shape[0]
    r = lax.broadcasted_iota(I32, (rows, rows), 0)
    c = lax.broadcasted_iota(I32, (rows, rows), 1)
    earlier = jnp.where(c < r, 1.0, 0.0).astype(BF16)
    in_tile = jnp.dot(earlier, sel, preferred_element_type=F32)
    n_run = jnp.sum(sel.astype(F32), axis=0, keepdims=True)
    run_rows = jnp.ceil(n_run * (1.0 / RUN_CHUNK)) * RUN_CHUNK
    er = lax.broadcasted_iota(I32, (N_EXPERTS, N_EXPERTS), 0)
    ec = lax.broadcasted_iota(I32, (N_EXPERTS, N_EXPERTS), 1)
    lower_experts = jnp.where(er < ec, 1.0, 0.0).astype(BF16)
    local0 = jnp.dot(jnp.broadcast_to(run_rows, (8, N_EXPERTS)).astype(BF16), lower_experts,
                     preferred_element_type=F32)[0:1]
    pos = in_tile + local0
    lane = lax.broadcasted_iota(I32, (rows, N_EXPERTS), 1)
    lane_out = lax.broadcasted_iota(I32, (rows, V7X_LANES), 1)
    topi = topi_ref[...]
    lpos = jnp.full((rows, V7X_LANES), -1, I32)
    for k in range(TOP_K):
        mine = jnp.sum(jnp.where(lane == topi[:, k:k + 1], pos, 0.0), axis=-1, keepdims=True)
        lpos = jnp.where(lane_out == k, mine.astype(I32), lpos)
    lpos_ref[...] = lpos
    total = carry_ref[...] + n_run
    carry_ref[...] = total
    cnt_ref[...] = total


def _ranks(sel, topi, init, *, rows):
    n = sel.shape[0]
    small = pl.BlockSpec((8, N_EXPERTS), lambda i: (0, 0))
    lanes = pl.BlockSpec((rows, V7X_LANES), lambda i: (i, 0))
    return pl.pallas_call(
        _rank_kernel,
        grid=(n // rows,),
        in_specs=[pl.BlockSpec((rows, N_EXPERTS), lambda i: (i, 0)), lanes, small],
        out_specs=[lanes, pl.BlockSpec((1, 8, N_EXPERTS), lambda i: (i, 0, 0)), small],
        out_shape=[jax.ShapeDtypeStruct((n, V7X_LANES), I32),
                   jax.ShapeDtypeStruct((n // rows, 8, N_EXPERTS), F32),
                   jax.ShapeDtypeStruct((8, N_EXPERTS), F32)],
        scratch_shapes=[pltpu.VMEM((8, N_EXPERTS), F32)],
        compiler_params=_params("arbitrary"),
        name="expert_ranks",
    )(sel, topi, init)


SCAL_SORTED0 = 0
SCAL_CHUNKS = N_EXPERTS
SCAL_LOCAL0 = 2 * N_EXPERTS
SCAL_TOTAL = 3 * N_EXPERTS
SCAL_PREV_TOTAL = SCAL_TOTAL + 2
BIG_CHUNKS = 8
COPY_ROWS = (BIG_CHUNKS * RUN_CHUNK, RUN_CHUNK)


def _run_copy(local_ref, local_row, sorted_ref, sorted_row, sems, size, to_sorted):
    loc = local_ref.at[pl.ds(local_row, COPY_ROWS[size])]
    srt = sorted_ref.at[pl.ds(sorted_row, COPY_ROWS[size])]
    sem = sems[size]
    return pltpu.make_async_copy(loc, srt, sem) if to_sorted else pltpu.make_async_copy(srt, loc, sem)


def _start_runs(scal_ref, local_ref, sorted_ref, sems, *, to_sorted):
    def expert(e, carry):
        sorted0 = scal_ref[0, 0, SCAL_SORTED0 + e]
        local0 = scal_ref[0, 0, SCAL_LOCAL0 + e]
        n_chunk = scal_ref[0, 0, SCAL_CHUNKS + e]
        n_big = n_chunk // BIG_CHUNKS
        rest0 = n_big * COPY_ROWS[0]

        def copy(size, offset):
            return _run_copy(local_ref, pl.multiple_of(local0 + offset, RUN_CHUNK), sorted_ref,
                             pl.multiple_of(sorted0 + offset, RUN_CHUNK), sems, size, to_sorted)

        def big(c, inner):
            copy(0, c * COPY_ROWS[0]).start()
            return inner

        def small(c, inner):
            copy(1, rest0 + c * COPY_ROWS[1]).start()
            return inner

        lax.fori_loop(0, n_big, big, 0)
        lax.fori_loop(0, n_chunk - n_big * BIG_CHUNKS, small, 0)
        return carry

    lax.fori_loop(0, N_EXPERTS, expert, 0)


def _wait_runs(scal_ref, total_lane, local_ref, sorted_ref, sems, *, to_sorted):
    for size in range(len(COPY_ROWS)):
        def drain(c, carry, size=size):
            _run_copy(local_ref, 0, sorted_ref, 0, sems, size, to_sorted).wait()
            return carry

        lax.fori_loop(0, scal_ref[0, 0, total_lane + size], drain, 0)


def _dispatch_kernel(scal_ref, lpos_ref, y_ref, *rest, local_rows):
    sorted_ref, stage, big_sems, small_sems = rest[-4:]
    i = pl.program_id(0)
    slot = i % 2
    sems = [(big_sems.at[s], small_sems.at[s]) for s in (slot, 1 - slot)]
    rows = y_ref.shape[0]
    yb = y_ref[...].astype(BF16)
    lpos = lpos_ref[0]
    blk = LOCAL_BLOCK
    for rb in range(local_rows // blk):
        j = lax.broadcasted_iota(I32, (blk, rows), 0) + rb * blk
        onehot = jnp.zeros((blk, rows), F32)
        for k in range(TOP_K):
            onehot = jnp.where(j == lpos[k:k + 1], 1.0, onehot)
        onehot = onehot.astype(BF16)
        stage[slot, rb * blk:(rb + 1) * blk, :] = jnp.dot(onehot, yb, preferred_element_type=F32)
    _start_runs(scal_ref, stage.at[slot], sorted_ref, sems[0], to_sorted=True)

    @pl.when(i > 0)
    def _():
        _wait_runs(scal_ref, SCAL_PREV_TOTAL, stage.at[1 - slot], sorted_ref, sems[1],
                   to_sorted=True)

    @pl.when(i == pl.num_programs(0) - 1)
    def _():
        _wait_runs(scal_ref, SCAL_TOTAL, stage.at[slot], sorted_ref, sems[0], to_sorted=True)


def _local_rows(rows):
    worst = rows * TOP_K + N_EXPERTS * (RUN_CHUNK - 1)
    return (worst + LOCAL_BLOCK - 1) // LOCAL_BLOCK * LOCAL_BLOCK


def _dispatch(scal, lpos_t, y1, sorted_rows, *, rows, n_sorted):
    n = y1.shape[0]
    local_rows = _local_rows(rows)
    any_spec = pl.BlockSpec(memory_space=pl.ANY)
    in_specs = [pl.BlockSpec((1, 1, V7X_LANES), lambda i: (i, 0, 0), memory_space=pltpu.SMEM),
                pl.BlockSpec((1, 8, rows), lambda i: (i, 0, 0)),
                pl.BlockSpec((rows, D_MODEL), lambda i: (i, 0))]
    args = [scal, lpos_t, y1]
    aliases = {}
    if sorted_rows is not None:
        in_specs.append(any_spec)
        args.append(sorted_rows)
        aliases = {3: 0}
    return pl.pallas_call(
        functools.partial(_dispatch_kernel, local_rows=local_rows),
        grid=(n // rows,),
        in_specs=in_specs,
        out_specs=any_spec,
        out_shape=jax.ShapeDtypeStruct((n_sorted, D_MODEL), F32),
        scratch_shapes=[pltpu.VMEM((2, local_rows, D_MODEL), F32),
                        pltpu.SemaphoreType.DMA((2,)), pltpu.SemaphoreType.DMA((2,))],
        input_output_aliases=aliases,
        compiler_params=_params("arbitrary"),
        name="dispatch_rows",
    )(*args)


def _expert_kernel(te_ref, tr_ref, x_ref, wgu_ref, bgu_ref, wd_ref, bd_ref, o_ref,
                   wgu_b, wd_b):
    i = pl.program_id(0)
    e = te_ref[i]
    changed = (i == 0) | (e != te_ref[jnp.maximum(i - 1, 0)])

    @pl.when(changed)
    def _():
        wgu_b[...] = wgu_ref[0].astype(BF16)
        wd_b[...] = wd_ref[0].astype(BF16)

    @pl.when(tr_ref[i] == 0)
    def _():
        o_ref[...] = jnp.zeros(o_ref.shape, o_ref.dtype)

    @pl.when(tr_ref[i] > 0)
    def _():
        row = lax.broadcasted_iota(I32, x_ref.shape, 0)
        xb = jnp.where(row < tr_ref[i], x_ref[...], 0.0).astype(BF16)
        bgu = bgu_ref[0]
        acc = None
        n_col = 256
        for j in range(D_FF // n_col):
            cg = slice(j * n_col, (j + 1) * n_col)
            cl = slice(D_FF + j * n_col, D_FF + (j + 1) * n_col)
            hg = jnp.dot(xb, wgu_b[:, cg], preferred_element_type=F32) + bgu[:, cg]
            hl = jnp.dot(xb, wgu_b[:, cl], preferred_element_type=F32) + bgu[:, cl]
            glu = jnp.minimum(hg, SWIGLU_LIMIT)
            lin = jnp.clip(hl, -SWIGLU_LIMIT, SWIGLU_LIMIT)
            act = glu * jax.nn.sigmoid(SWIGLU_ALPHA * glu) * (lin + 1.0)
            part = jnp.dot(act.astype(BF16), wd_b[cg, :], preferred_element_type=F32)
            acc = part if acc is None else acc + part
        o_ref[...] = acc + bd_ref[0]


def _experts(tile_expert, tile_rows, xs, w_gate_up, b_gate_up, w_down, b_down):
    n_tiles = xs.shape[0] // EXPERT_TILE
    grid_spec = pltpu.PrefetchScalarGridSpec(
        num_scalar_prefetch=2,
        grid=(n_tiles,),
        in_specs=[
            pl.BlockSpec((EXPERT_TILE, D_MODEL), lambda i, te, tr: (i, 0)),
            pl.BlockSpec((1, D_MODEL, 2 * D_FF), lambda i, te, tr: (te[i], 0, 0)),
            pl.BlockSpec((1, 1, 2 * D_FF), lambda i, te, tr: (te[i], 0, 0)),
            pl.BlockSpec((1, D_FF, D_MODEL), lambda i, te, tr: (te[i], 0, 0)),
            pl.BlockSpec((1, 1, D_MODEL), lambda i, te, tr: (te[i], 0, 0)),
        ],
        out_specs=pl.BlockSpec((EXPERT_TILE, D_MODEL), lambda i, te, tr: (i, 0)),
        scratch_shapes=[pltpu.VMEM((D_MODEL, 2 * D_FF), BF16), pltpu.VMEM((D_FF, D_MODEL), BF16)],
    )
    return pl.pallas_call(
        _expert_kernel,
        grid_spec=grid_spec,
        out_shape=jax.ShapeDtypeStruct(xs.shape, F32),
        compiler_params=_params("arbitrary"),
        name="expert_ffn",
    )(tile_expert, tile_rows, xs, w_gate_up, b_gate_up, w_down, b_down)


def _combine_kernel(scal_ref, next_scal_ref, lpos_ref, gate_ref, y_ref, g_ref, b_ref, eo_ref,
                    out_ref, buf, big_sems, small_sems, *, local_rows):
    i = pl.program_id(0)
    slot = i % 2
    sems = [(big_sems.at[s], small_sems.at[s]) for s in (slot, 1 - slot)]

    @pl.when(i == 0)
    def _():
        buf[...] = jnp.zeros(buf.shape, buf.dtype)
        _start_runs(scal_ref, buf.at[0], eo_ref, (big_sems.at[0], small_sems.at[0]), to_sorted=False)

    @pl.when(i + 1 < pl.num_programs(0))
    def _():
        _start_runs(next_scal_ref, buf.at[1 - slot], eo_ref, sems[1], to_sorted=False)

    _wait_runs(scal_ref, SCAL_TOTAL, buf.at[slot], eo_ref, sems[0], to_sorted=False)
    rows = y_ref.shape[0]
    lpos = lpos_ref[...]
    gate = gate_ref[...]
    blk = LOCAL_BLOCK
    lane = lax.broadcasted_iota(I32, (rows, V7X_LANES), 1)
    slot_row = [jnp.broadcast_to(lpos[:, k:k + 1], (rows, V7X_LANES)) for k in range(TOP_K)]
    slot_gate = [jnp.broadcast_to(gate[:, k:k + 1], (rows, V7X_LANES)) for k in range(TOP_K)]
    moe = None
    for kb in range(local_rows // blk):
        pieces = []
        for jb in range(blk // V7X_LANES):
            col = lane + (kb * blk + jb * V7X_LANES)
            w = jnp.zeros((rows, V7X_LANES), F32)
            for k in range(TOP_K):
                w = jnp.where(col == slot_row[k], slot_gate[k], w)
            pieces.append(w)
        weights = jnp.concatenate(pieces, axis=1).astype(BF16)
        part = jnp.dot(weights, buf[slot, kb * blk:(kb + 1) * blk, :].astype(BF16),
                       preferred_element_type=F32)
        moe = part if moe is None else moe + part
    out_ref[...] = _layer_norm(DN_ALPHA * y_ref[...] + moe, g_ref[...], b_ref[...])


def _combine(scal, lpos, gates, y1, ln_g, ln_b, expert_out, *, rows):
    n = y1.shape[0]
    local_rows = _local_rows(rows)
    wide = pl.BlockSpec((rows, D_MODEL), lambda i: (i, 0))
    lanes = pl.BlockSpec((rows, V7X_LANES), lambda i: (i, 0))
    n_steps = n // rows
    scalars = pl.BlockSpec((1, 1, V7X_LANES), lambda i: (i, 0, 0), memory_space=pltpu.SMEM)
    next_scalars = pl.BlockSpec((1, 1, V7X_LANES), lambda i: (jnp.minimum(i + 1, n_steps - 1), 0, 0),
                                memory_space=pltpu.SMEM)
    return pl.pallas_call(
        functools.partial(_combine_kernel, local_rows=local_rows),
        grid=(n_steps,),
        in_specs=[scalars, next_scalars, lanes, lanes, wide, _resident(ln_g.shape),
                  _resident(ln_b.shape), pl.BlockSpec(memory_space=pl.ANY)],
        out_specs=wide,
        out_shape=jax.ShapeDtypeStruct((n, D_MODEL), F32),
        scratch_shapes=[pltpu.VMEM((2, local_rows, D_MODEL), F32),
                        pltpu.SemaphoreType.DMA((2,)), pltpu.SemaphoreType.DMA((2,))],
        compiler_params=_params("arbitrary"),
        name="combine_norm",
    )(scal, scal, lpos, gates, y1, ln_g, ln_b, expert_out)


def _tile_tables(lpos, n_chunk, sorted0, rows):
    n_tiles = lpos.shape[0] // rows
    local0 = (jnp.cumsum(n_chunk, axis=1) - n_chunk) * RUN_CHUNK
    n_big = n_chunk // BIG_CHUNKS
    total = jnp.stack([jnp.sum(n_big, axis=1), jnp.sum(n_chunk - n_big * BIG_CHUNKS, axis=1)], axis=1)
    prev_total = jnp.concatenate([jnp.zeros((1, 2), I32), total[:-1]], axis=0)
    pad = jnp.zeros((n_tiles, V7X_LANES - SCAL_PREV_TOTAL - 2), I32)
    scal = jnp.concatenate([sorted0, n_chunk, local0, total, prev_total, pad], axis=1)
    lpos_t = jnp.transpose(lpos[:, :8].reshape(n_tiles, rows, 8), (0, 2, 1))
    return scal.reshape(n_tiles, 1, V7X_LANES), lpos_t


def kernel(x_prompt, x_sample, mem_prompt, cache_attn_k, cache_attn_v, cache_conv, cache_mem_k,
           cache_mem_v, w_in, rel_bias, conv_w, w_mem_kv, w_gate, w_br_attn, w_br_conv, w_br_mem,
           w_out, ln1_g, ln1_b, w_router, b_router, w_gate_up, b_gate_up, w_down, b_down,
           ln2_g, ln2_b):
    depth = w_in.shape[0]
    assert depth == 1, "one layer only"
    batch, seq, _ = x_prompt.shape
    dec_batch, dec_seq, _ = x_sample.shape
    n_prompt = batch * seq
    n_sample = dec_batch * dec_seq
    assert seq % ROW_TILE == 0 and HIST == ROW_TILE and cache_attn_k.shape[2] == HIST
    assert dec_seq % 16 == 0 and dec_seq <= CHUNK and cache_conv.shape[2] == 2

    w_in_b = w_in[0].astype(BF16)
    w_mem_b = w_mem_kv[0].astype(BF16)
    wr = w_router[0]
    wr_hi = wr.astype(BF16)
    wr_lo = (wr - wr_hi.astype(F32)).astype(BF16)
    merge_w = (w_gate[0].astype(BF16), w_br_attn[0].astype(BF16), w_br_conv[0].astype(BF16),
               w_br_mem[0].astype(BF16), w_out[0].astype(BF16), ln1_g, ln1_b,
               jnp.concatenate([wr_hi, wr_lo], axis=1), wr_hi, b_router)
    conv_w8 = jnp.pad(conv_w[0], ((0, 5), (0, 0)))
    bias = _band_bias(rel_bias[0])

    xp = x_prompt.reshape(n_prompt, D_MODEL)
    mk_f, mv_f, mk_b, mv_b = _memory_kv(mem_prompt.reshape(batch * MEM_TOKENS, D_MODEL), w_mem_b)
    q, k_pad, v_pad, c, qm, k_tail, v_tail, u_tail = _proj_in(
        xp, w_in_b, conv_w8, jnp.zeros((batch, 8, WIDTH), F32),
        batch=batch, seq=seq, rows=ROW_TILE, pad_steps=1)
    a, m = _attention(
        q, k_pad, v_pad, bias.reshape(ATT_HEADS * CHUNK, WINDOW), qm,
        mk_b.reshape(batch, MEM_TOKENS, WIDTH),
        mv_b.reshape(batch, MEM_TOKENS, WIDTH), batch=batch, seq=seq, rows=ROW_TILE,
        chunk_rows=CHUNK, valid_keys=BAND, first_chunk=0, mem_rows=256)
    y1_p, sel_p, topi_p, topg_p = _merge(xp, a, c, m, merge_w, rows=MERGE_TILE)

    xs = x_sample.reshape(n_sample, D_MODEL)
    conv_init = jnp.pad(cache_conv[0], ((0, 0), (6, 0), (0, 0)))
    q_s, k_s, v_s, c_s, qm_s, k_new, v_new, u_tail_s = _proj_in(
        xs, w_in_b, conv_w8, conv_init, batch=dec_batch, seq=dec_seq, rows=dec_seq, pad_steps=0)
    pad_rows = WINDOW - HIST - dec_seq

    def window(cache, new):
        cache = cache[0].reshape(dec_batch, HIST, WIDTH).astype(BF16)
        return jnp.pad(jnp.concatenate([cache, new], axis=1), ((0, 0), (0, pad_rows), (0, 0)))

    a_s, m_s = _attention(
        q_s, window(cache_attn_k, k_s), window(cache_attn_v, v_s),
        bias[:, :dec_seq].reshape(ATT_HEADS * dec_seq, WINDOW), qm_s,
        cache_mem_k[0].reshape(dec_batch, MEM_TOKENS, WIDTH).astype(BF16),
        cache_mem_v[0].reshape(dec_batch, MEM_TOKENS, WIDTH).astype(BF16),
        batch=dec_batch, seq=dec_seq, rows=dec_seq, chunk_rows=dec_seq,
        valid_keys=HIST + dec_seq, first_chunk=LEFT_CHUNKS, mem_rows=dec_seq)
    y1_s, sel_s, topi_s, topg_s = _merge(xs, a_s, c_s, m_s, merge_w, rows=n_sample)

    lpos_p, before_p, cnt_p = _ranks(sel_p, topi_p, jnp.zeros((8, N_EXPERTS), F32), rows=ROW_TILE)
    lpos_s, before_s, cnt = _ranks(sel_s, topi_s, cnt_p, rows=n_sample)
    before = jnp.concatenate([before_p[:, 0], before_s[:, 0]], axis=0).astype(I32)
    n_run = jnp.concatenate([before[1:], cnt[:1].astype(I32)], axis=0) - before
    n_chunk = (n_run + RUN_CHUNK - 1) // RUN_CHUNK
    run_rows = n_chunk * RUN_CHUNK
    counts = jnp.sum(run_rows, axis=0)
    padded = (counts + EXPERT_TILE - 1) // EXPERT_TILE * EXPERT_TILE
    ends = jnp.cumsum(padded)
    starts = ends - padded
    sorted0 = starts[None, :] + jnp.cumsum(run_rows, axis=0) - run_rows
    n_token_tiles = n_prompt // ROW_TILE + 1
    n_tiles = ((n_prompt + n_sample) * TOP_K
               + n_token_tiles * N_EXPERTS * (RUN_CHUNK - 1)) // EXPERT_TILE + N_EXPERTS
    tile_start = jnp.arange(n_tiles, dtype=I32) * EXPERT_TILE
    tile_expert = jnp.minimum(
        jnp.sum((ends[None, :] <= tile_start[:, None]).astype(I32), axis=1), N_EXPERTS - 1)
    tile_rows = jnp.clip(starts[tile_expert] + counts[tile_expert] - tile_start, 0, EXPERT_TILE)
    scal_p, lpos_tp = _tile_tables(lpos_p, n_chunk[:-1], sorted0[:-1], ROW_TILE)
    scal_s, lpos_ts = _tile_tables(lpos_s, n_chunk[-1:], sorted0[-1:], n_sample)

    n_sorted = n_tiles * EXPERT_TILE
    sorted_rows = _dispatch(scal_p, lpos_tp, y1_p, None, rows=ROW_TILE, n_sorted=n_sorted)
    sorted_rows = _dispatch(scal_s, lpos_ts, y1_s, sorted_rows, rows=n_sample, n_sorted=n_sorted)
    expert_out = _experts(tile_expert, tile_rows, sorted_rows, w_gate_up[0],
                          b_gate_up[0].reshape(N_EXPERTS, 1, 2 * D_FF), w_down[0],
                          b_down[0].reshape(N_EXPERTS, 1, D_MODEL))
    y_p = _combine(scal_p, lpos_p, topg_p, y1_p, ln2_g, ln2_b, expert_out, rows=ROW_TILE)
    y_s = _combine(scal_s, lpos_s, topg_s, y1_s, ln2_g, ln2_b, expert_out, rows=n_sample)

    def heads(t, n_b, n_rows, n_heads, dim):
        return t.reshape(1, n_b, n_rows, n_heads, dim)

    return (y_p.reshape(batch, seq, D_MODEL),
            y_s.reshape(dec_batch, dec_seq, D_MODEL),
            heads(k_tail, batch, HIST, ATT_HEADS, ATT_DIM),
            heads(v_tail, batch, HIST, ATT_HEADS, ATT_DIM),
            u_tail[:, 6:8].reshape(1, batch, 2, WIDTH),
            heads(mk_f, batch, MEM_TOKENS, MEM_HEADS, MEM_DIM),
            heads(mv_f, batch, MEM_TOKENS, MEM_HEADS, MEM_DIM),
            heads(k_new, dec_batch, dec_seq, ATT_HEADS, ATT_DIM),
            heads(v_new, dec_batch, dec_seq, ATT_HEADS, ATT_DIM),
            u_tail_s[:, 6:8].reshape(1, dec_batch, 2, WIDTH))
```

```python
import functools

import jax
import jax.numpy as jnp
from jax import lax
from jax.experimental import pallas as pl
from jax.experimental.pallas import tpu as pltpu

F32 = jnp.float32
BF16 = jnp.bfloat16
I32 = jnp.int32

D_MODEL = 1024
CHUNK = 64
LEFT_CHUNKS = 8
HIST = LEFT_CHUNKS * CHUNK
BAND = HIST + CHUNK
ATT_HEADS = 8
ATT_DIM = 64
REL_MAX = 128
WIDTH = 512
MEM_TOKENS = 256
MEM_HEADS = 4
MEM_DIM = 128
N_EXPERTS = 32
TOP_K = 4
D_FF = 1024
SWIGLU_LIMIT = 7.0
SWIGLU_ALPHA = 1.702
DN_ALPHA = 2.0 ** 0.25
LN_EPS = 1e-5
NEG_INF = -1e30

V7X_LANES = 128
V7X_VMEM_BYTES = 64 * 1024 * 1024
VMEM_LIMIT = V7X_VMEM_BYTES - 8 * 1024 * 1024

ROW_TILE = 512
MERGE_TILE = 1024
WINDOW = BAND + CHUNK
EXPERT_TILE = 1024
RUN_CHUNK = 8
LOCAL_BLOCK = 256


def _params(*sem):
    return pltpu.CompilerParams(dimension_semantics=sem, vmem_limit_bytes=VMEM_LIMIT)


def _resident(shape):
    nd = len(shape)
    return pl.BlockSpec(shape, lambda *_: (0,) * nd, pipeline_mode=pl.Buffered(1))


def _layer_norm(r, g, b):
    mu = jnp.mean(r, axis=-1, keepdims=True)
    d = r - mu
    var = jnp.mean(d * d, axis=-1, keepdims=True)
    return d * lax.rsqrt(var + LN_EPS) * g + b


def _memkv_kernel(x_ref, w_ref, kf_ref, vf_ref, kb_ref, vb_ref):
    y = jnp.dot(x_ref[...].astype(BF16), w_ref[...], preferred_element_type=F32)
    k = y[:, :WIDTH]
    v = y[:, WIDTH:]
    kf_ref[...] = k
    vf_ref[...] = v
    kb_ref[...] = k.astype(BF16)
    vb_ref[...] = v.astype(BF16)


def _memory_kv(mem2d, w_b):
    rows = mem2d.shape[0]
    tile = pl.BlockSpec((ROW_TILE, D_MODEL), lambda i: (i, 0))
    half = pl.BlockSpec((ROW_TILE, WIDTH), lambda i: (i, 0))
    return pl.pallas_call(
        _memkv_kernel,
        grid=(rows // ROW_TILE,),
        in_specs=[tile, _resident((D_MODEL, 2 * WIDTH))],
        out_specs=[half, half, half, half],
        out_shape=[jax.ShapeDtypeStruct((rows, WIDTH), F32)] * 2
        + [jax.ShapeDtypeStruct((rows, WIDTH), BF16)] * 2,
        compiler_params=_params("arbitrary"),
        name="memory_kv",
    )(mem2d, w_b)


def _proj_in_kernel(x_ref, w_ref, cw_ref, cinit_ref,
                    q_ref, k_ref, v_ref, c_ref, qm_ref, kt_ref, vt_ref, ut_ref,
                    carry_ref, *, pad_steps):
    s = pl.program_id(1)
    ns = pl.num_programs(1)
    rows = x_ref.shape[0]
    first = pad_steps
    last = ns - 1 - pad_steps

    if pad_steps:
        @pl.when((s < first) | (s > last))
        def _():
            k_ref[...] = jnp.zeros(k_ref.shape, k_ref.dtype)
            v_ref[...] = jnp.zeros(v_ref.shape, v_ref.dtype)

    @pl.when((s >= first) & (s <= last))
    def _():
        @pl.when(s == first)
        def _():
            carry_ref[...] = cinit_ref[0]

        xb = x_ref[...].astype(BF16)

        def proj(g):
            return jnp.dot(xb, w_ref[:, g * WIDTH:(g + 1) * WIDTH], preferred_element_type=F32)

        q_ref[...] = (proj(0) * (ATT_DIM ** -0.5)).astype(BF16)
        k = proj(1)
        v = proj(2)
        k_ref[0] = k.astype(BF16)
        v_ref[0] = v.astype(BF16)

        @pl.when(s == last)
        def _():
            kt_ref[0] = k
            vt_ref[0] = v

        bg = proj(3)
        u = proj(4) * proj(5)
        prev = carry_ref[...]
        row = lax.broadcasted_iota(I32, u.shape, 0)
        u1 = jnp.where(row == 0, prev[7:8], pltpu.roll(u, 1, 0))
        u2 = jnp.where(row == 0, prev[6:7], jnp.where(row == 1, prev[7:8], pltpu.roll(u, 2, 0)))
        cw = cw_ref[...]
        c_ref[...] = (bg * (cw[0:1] * u2 + cw[1:2] * u1 + cw[2:3] * u)).astype(BF16)
        tail = u[rows - 8:]
        carry_ref[...] = tail

        @pl.when(s == last)
        def _():
            ut_ref[0] = tail

        qm_ref[...] = proj(6).astype(BF16)


def _proj_in(x2d, w_b, conv_w8, conv_init, *, batch, seq, rows, pad_steps):
    n_data = seq // rows
    n_steps = n_data + 2 * pad_steps

    def data_idx(b, s):
        return (b * n_data + jnp.clip(s - pad_steps, 0, n_data - 1), 0)

    wide = pl.BlockSpec((rows, D_MODEL), data_idx)
    narrow = pl.BlockSpec((rows, WIDTH), data_idx)
    seq_blk = pl.BlockSpec((1, rows, WIDTH), lambda b, s: (b, s, 0))
    tail_blk = pl.BlockSpec((1, rows, WIDTH), lambda b, s: (b, 0, 0))
    tail8 = pl.BlockSpec((1, 8, WIDTH), lambda b, s: (b, 0, 0))
    tok = jax.ShapeDtypeStruct((batch * seq, WIDTH), BF16)
    kv = jax.ShapeDtypeStruct((batch, n_steps * rows, WIDTH), BF16)
    tail = jax.ShapeDtypeStruct((batch, rows, WIDTH), F32)
    return pl.pallas_call(
        functools.partial(_proj_in_kernel, pad_steps=pad_steps),
        grid=(batch, n_steps),
        in_specs=[wide, _resident(w_b.shape), _resident(conv_w8.shape), tail8],
        out_specs=[narrow, seq_blk, seq_blk, narrow, narrow, tail_blk, tail_blk, tail8],
        out_shape=[tok, kv, kv, tok, tok, tail, tail,
                   jax.ShapeDtypeStruct((batch, 8, WIDTH), F32)],
        scratch_shapes=[pltpu.VMEM((8, WIDTH), F32)],
        compiler_params=_params("arbitrary", "arbitrary"),
        name="proj_in",
    )(x2d, w_b, conv_w8, conv_init)


def _bias_kernel(tab_ref, out_ref):
    shape = (CHUNK, WINDOW)
    i = lax.broadcasted_iota(I32, shape, 0)
    j = lax.broadcasted_iota(I32, shape, 1)
    idx = jnp.clip(HIST + i - j, -REL_MAX, REL_MAX) + REL_MAX
    for h in range(ATT_HEADS):
        def body(d, acc, h=h):
            return jnp.where(idx == d, tab_ref[h, d], acc)
        out_ref[h] = lax.fori_loop(0, 2 * REL_MAX + 1, body, jnp.zeros(shape, F32))


def _band_bias(rel_table):
    return pl.pallas_call(
        _bias_kernel,
        in_specs=[pl.BlockSpec(memory_space=pltpu.SMEM)],
        out_shape=jax.ShapeDtypeStruct((ATT_HEADS, CHUNK, WINDOW), F32),
        name="band_bias",
    )(rel_table)


def _attention_kernel(q_ref, k_ref, v_ref, bias_ref, qm_ref, mk_ref, mv_ref,
                      a_ref, m_ref, *, chunk_rows, n_chunks, valid_keys, first_chunk, mem_rows):
    s = pl.program_id(1)
    nr = chunk_rows
    pair_w = 2 * ATT_DIM
    col = lax.broadcasted_iota(I32, (ATT_HEADS * nr, WINDOW), 1)
    lane2 = lax.broadcasted_iota(I32, (2 * nr, pair_w), 1)
    row2 = lax.broadcasted_iota(I32, (2 * nr, pair_w), 0)
    own = (lane2 >= ATT_DIM) == (row2 >= nr)
    low_lanes = lax.broadcasted_iota(I32, (nr, pair_w), 1) < ATT_DIM
    nt = (((1,), (1,)), ((), ()))

    def chunk(c, carry):
        g = first_chunk + s * n_chunks + c
        r0 = pl.multiple_of(c * nr, nr)
        w0 = pl.multiple_of((s * n_chunks + c) * CHUNK, CHUNK)
        qc = q_ref[pl.ds(r0, nr), :].astype(F32)
        kw = k_ref[0, pl.ds(w0, WINDOW), :]
        vw = v_ref[0, pl.ds(w0, WINDOW), :]
        scores = []
        for pair in range(ATT_HEADS // 2):
            lanes = slice(pair * pair_w, (pair + 1) * pair_w)
            qp = qc[:, lanes]
            q2 = jnp.where(own, jnp.concatenate([qp, qp], axis=0), 0.0).astype(BF16)
            scores.append(lax.dot_general(q2, kw[:, lanes], nt, preferred_element_type=F32))
        sc = jnp.concatenate(scores, axis=0) + bias_ref[...]
        sc = jnp.where((col >= HIST - CHUNK * g) & (col < valid_keys), sc, NEG_INF)
        m = jnp.max(sc, axis=-1, keepdims=True)
        p = jnp.exp(sc - m)
        inv = 1.0 / jnp.sum(p, axis=-1, keepdims=True)
        pb = p.astype(BF16)
        outs = []
        for pair in range(ATT_HEADS // 2):
            lanes = slice(pair * pair_w, (pair + 1) * pair_w)
            rows = slice(pair * 2 * nr, (pair + 1) * 2 * nr)
            o = jnp.dot(pb[rows], vw[:, lanes], preferred_element_type=F32) * inv[rows]
            outs.append(jnp.where(low_lanes, o[:nr], o[nr:]))
        a_ref[pl.ds(r0, nr), :] = jnp.concatenate(outs, axis=1).astype(BF16)
        return carry

    lax.fori_loop(0, n_chunks, chunk, 0, unroll=min(n_chunks, 4))

    rows = qm_ref.shape[0]
    for rb in range(rows // mem_rows):
        rs = slice(rb * mem_rows, (rb + 1) * mem_rows)
        scores = []
        for h in range(MEM_HEADS):
            lanes = slice(h * MEM_DIM, (h + 1) * MEM_DIM)
            scores.append(lax.dot_general(qm_ref[rs, lanes], mk_ref[0, :, lanes], nt,
                                          preferred_element_type=F32))
        sc = jnp.concatenate(scores, axis=0) * (MEM_DIM ** -0.5)
        m = jnp.max(sc, axis=-1, keepdims=True)
        p = jnp.exp(sc - m)
        inv = 1.0 / jnp.sum(p, axis=-1, keepdims=True)
        pb = p.astype(BF16)
        for h in range(MEM_HEADS):
            lanes = slice(h * MEM_DIM, (h + 1) * MEM_DIM)
            hs = slice(h * mem_rows, (h + 1) * mem_rows)
            o = jnp.dot(pb[hs], mv_ref[0, :, lanes], preferred_element_type=F32) * inv[hs]
            m_ref[rs, lanes] = o.astype(BF16)


def _attention(q, k_pad, v_pad, bias, qm, mk, mv, *, batch, seq, rows, chunk_rows,
               valid_keys, first_chunk, mem_rows):
    n_steps = seq // rows
    tok = pl.BlockSpec((rows, WIDTH), lambda b, s: (b * n_steps + s, 0))
    whole_seq = pl.BlockSpec((1, k_pad.shape[1], WIDTH), lambda b, s: (b, 0, 0))
    mem = pl.BlockSpec((1, MEM_TOKENS, WIDTH), lambda b, s: (b, 0, 0))
    kern = functools.partial(
        _attention_kernel, chunk_rows=chunk_rows, n_chunks=rows // chunk_rows,
        valid_keys=valid_keys, first_chunk=first_chunk, mem_rows=mem_rows)
    out = jax.ShapeDtypeStruct((batch * seq, WIDTH), BF16)
    return pl.pallas_call(
        kern,
        grid=(batch, n_steps),
        in_specs=[tok, whole_seq, whole_seq, _resident(bias.shape), tok, mem, mem],
        out_specs=[tok, tok],
        out_shape=[out, out],
        compiler_params=_params("arbitrary", "arbitrary"),
        name="attention",
    )(q, k_pad, v_pad, bias, qm, mk, mv)


def _merge_kernel(x_ref, a_ref, c_ref, m_ref, wg_ref, wa_ref, wc_ref, wm_ref, wo_ref,
                  g_ref, b_ref, wr2_ref, wrh_ref, br_ref,
                  y_ref, sel_ref, topi_ref, topg_ref):
    x = x_ref[...]
    xb = x.astype(BF16)
    branches = [(a_ref[...], wa_ref), (c_ref[...], wc_ref), (m_ref[...], wm_ref)]
    out = None
    n_col = 256
    for j in range(D_MODEL // n_col):
        cols = slice(j * n_col, (j + 1) * n_col)
        comb = None
        for i, (br, wbr) in enumerate(branches):
            gcols = slice(i * D_MODEL + j * n_col, i * D_MODEL + (j + 1) * n_col)
            gate = jax.nn.sigmoid(jnp.dot(xb, wg_ref[:, gcols], preferred_element_type=F32))
            term = gate * jnp.dot(br, wbr[:, cols], preferred_element_type=F32)
            comb = term if comb is None else comb + term
        part = jnp.dot(comb.astype(BF16), wo_ref[cols, :], preferred_element_type=F32)
        out = part if out is None else out + part
    y = _layer_norm(DN_ALPHA * x + out, g_ref[...], b_ref[...])
    y_ref[...] = y

    y_hi = y.astype(BF16)
    y_lo = (y - y_hi.astype(F32)).astype(BF16)
    both = jnp.dot(y_hi, wr2_ref[...], preferred_element_type=F32)
    logits = (both[:, :N_EXPERTS] + both[:, N_EXPERTS:]
              + jnp.dot(y_lo, wrh_ref[...], preferred_element_type=F32) + br_ref[...])

    rows = logits.shape[0]
    lane = lax.broadcasted_iota(I32, (rows, N_EXPERTS), 1).astype(F32)
    lane_out = lax.broadcasted_iota(I32, (rows, V7X_LANES), 1)
    work = logits
    sel = jnp.zeros((rows, N_EXPERTS), F32)
    topi = jnp.zeros((rows, V7X_LANES), I32)
    vals = []
    for k in range(TOP_K):
        top = jnp.max(work, axis=-1, keepdims=True)
        idx = jnp.min(jnp.where(work == top, lane, float(N_EXPERTS)), axis=-1, keepdims=True)
        hit = lane == idx
        sel = jnp.where(hit, 1.0, sel)
        work = jnp.where(hit, -jnp.inf, work)
        topi = jnp.where(lane_out == k, idx.astype(I32), topi)
        vals.append(top)
    exps = [jnp.exp(v - vals[0]) for v in vals]
    denom = exps[0] + exps[1] + exps[2] + exps[3]
    topg = jnp.zeros((rows, V7X_LANES), F32)
    for k in range(TOP_K):
        topg = jnp.where(lane_out == k, exps[k] / denom, topg)
    sel_ref[...] = sel.astype(BF16)
    topi_ref[...] = topi
    topg_ref[...] = topg


def _merge(x2d, a, c, m, weights, *, rows):
    n = x2d.shape[0]
    wide = pl.BlockSpec((rows, D_MODEL), lambda i: (i, 0))
    narrow = pl.BlockSpec((rows, WIDTH), lambda i: (i, 0))
    lanes = pl.BlockSpec((rows, V7X_LANES), lambda i: (i, 0))
    return pl.pallas_call(
        _merge_kernel,
        grid=(n // rows,),
        in_specs=[wide, narrow, narrow, narrow] + [_resident(w.shape) for w in weights],
        out_specs=[wide, pl.BlockSpec((rows, N_EXPERTS), lambda i: (i, 0)), lanes, lanes],
        out_shape=[jax.ShapeDtypeStruct((n, D_MODEL), F32),
                   jax.ShapeDtypeStruct((n, N_EXPERTS), BF16),
                   jax.ShapeDtypeStruct((n, V7X_LANES), I32),
                   jax.ShapeDtypeStruct((n, V7X_LANES), F32)],
        compiler_params=_params("arbitrary"),
        name="merge_router",
    )(x2d, a, c, m, *weights)


def _rank_kernel(sel_ref, topi_ref, init_ref, lpos_ref, before_ref, cnt_ref, carry_ref):
    @pl.when(pl.program_id(0) == 0)
    def _():
        carry_ref[...] = init_ref[...]

    before_ref[0] = carry_ref[...]
    sel = sel_ref[...]
    rows = sel.shape[0]
    r = lax.broadcasted_iota(I32, (rows, rows), 0)
    c = lax.broadcasted_iota(I32, (rows, rows), 1)
    earlier = jnp.where(c < r, 1.0, 0.0).astype(BF16)
    in_tile = jnp.dot(earlier, sel, preferred_element_type=F32)
    n_run = jnp.sum(sel.astype(F32), axis=0, keepdims=True)
    run_rows = jnp.ceil(n_run * (1.0 / RUN_CHUNK)) * RUN_CHUNK
    er = lax.broadcasted_iota(I32, (N_EXPERTS, N_EXPERTS), 0)
    ec = lax.broadcasted_iota(I32, (N_EXPERTS, N_EXPERTS), 1)
    lower_experts = jnp.where(er < ec, 1.0, 0.0).astype(BF16)
    local0 = jnp.dot(jnp.broadcast_to(run_rows, (8, N_EXPERTS)).astype(BF16), lower_experts,
                     preferred_element_type=F32)[0:1]
    pos = in_tile + local0
    lane = lax.broadcasted_iota(I32, (rows, N_EXPERTS), 1)
    lane_out = lax.broadcasted_iota(I32, (rows, V7X_LANES), 1)
    topi = topi_ref[...]
    lpos = jnp.full((rows, V7X_LANES), -1, I32)
    for k in range(TOP_K):
        mine = jnp.sum(jnp.where(lane == topi[:, k:k + 1], pos, 0.0), axis=-1, keepdims=True)
        lpos = jnp.where(lane_out == k, mine.astype(I32), lpos)
    lpos_ref[...] = lpos
    total = carry_ref[...] + n_run
    carry_ref[...] = total
    cnt_ref[...] = total


def _ranks(sel, topi, init, *, rows):
    n = sel.shape[0]
    small = pl.BlockSpec((8, N_EXPERTS), lambda i: (0, 0))
    lanes = pl.BlockSpec((rows, V7X_LANES), lambda i: (i, 0))
    return pl.pallas_call(
        _rank_kernel,
        grid=(n // rows,),
        in_specs=[pl.BlockSpec((rows, N_EXPERTS), lambda i: (i, 0)), lanes, small],
        out_specs=[lanes, pl.BlockSpec((1, 8, N_EXPERTS), lambda i: (i, 0, 0)), small],
        out_shape=[jax.ShapeDtypeStruct((n, V7X_LANES), I32),
                   jax.ShapeDtypeStruct((n // rows, 8, N_EXPERTS), F32),
                   jax.ShapeDtypeStruct((8, N_EXPERTS), F32)],
        scratch_shapes=[pltpu.VMEM((8, N_EXPERTS), F32)],
        compiler_params=_params("arbitrary"),
        name="expert_ranks",
    )(sel, topi, init)


SCAL_SORTED0 = 0
SCAL_CHUNKS = N_EXPERTS
SCAL_LOCAL0 = 2 * N_EXPERTS
SCAL_TOTAL = 3 * N_EXPERTS
SCAL_PREV_TOTAL = SCAL_TOTAL + 2
BIG_CHUNKS = 8
COPY_ROWS = (BIG_CHUNKS * RUN_CHUNK, RUN_CHUNK)


def _run_copy(local_ref, local_row, sorted_ref, sorted_row, sems, size, to_sorted):
    loc = local_ref.at[pl.ds(local_row, COPY_ROWS[size])]
    srt = sorted_ref.at[pl.ds(sorted_row, COPY_ROWS[size])]
    sem = sems[size]
    return pltpu.make_async_copy(loc, srt, sem) if to_sorted else pltpu.make_async_copy(srt, loc, sem)


def _start_runs(scal_ref, local_ref, sorted_ref, sems, *, to_sorted):
    def expert(e, carry):
        sorted0 = scal_ref[0, 0, SCAL_SORTED0 + e]
        local0 = scal_ref[0, 0, SCAL_LOCAL0 + e]
        n_chunk = scal_ref[0, 0, SCAL_CHUNKS + e]
        n_big = n_chunk // BIG_CHUNKS
        rest0 = n_big * COPY_ROWS[0]

        def copy(size, offset):
            return _run_copy(local_ref, pl.multiple_of(local0 + offset, RUN_CHUNK), sorted_ref,
                             pl.multiple_of(sorted0 + offset, RUN_CHUNK), sems, size, to_sorted)

        def big(c, inner):
            copy(0, c * COPY_ROWS[0]).start()
            return inner

        def small(c, inner):
            copy(1, rest0 + c * COPY_ROWS[1]).start()
            return inner

        lax.fori_loop(0, n_big, big, 0)
        lax.fori_loop(0, n_chunk - n_big * BIG_CHUNKS, small, 0)
        return carry

    lax.fori_loop(0, N_EXPERTS, expert, 0)


def _wait_runs(scal_ref, total_lane, local_ref, sorted_ref, sems, *, to_sorted):
    for size in range(len(COPY_ROWS)):
        def drain(c, carry, size=size):
            _run_copy(local_ref, 0, sorted_ref, 0, sems, size, to_sorted).wait()
            return carry

        lax.fori_loop(0, scal_ref[0, 0, total_lane + size], drain, 0)


def _dispatch_kernel(scal_ref, lpos_ref, y_ref, *rest, local_rows):
    sorted_ref, stage, big_sems, small_sems = rest[-4:]
    i = pl.program_id(0)
    slot = i % 2
    sems = [(big_sems.at[s], small_sems.at[s]) for s in (slot, 1 - slot)]
    rows = y_ref.shape[0]
    yb = y_ref[...].astype(BF16)
    lpos = lpos_ref[0]
    blk = LOCAL_BLOCK
    for rb in range(local_rows // blk):
        j = lax.broadcasted_iota(I32, (blk, rows), 0) + rb * blk
        onehot = jnp.zeros((blk, rows), F32)
        for k in range(TOP_K):
            onehot = jnp.where(j == lpos[k:k + 1], 1.0, onehot)
        onehot = onehot.astype(BF16)
        stage[slot, rb * blk:(rb + 1) * blk, :] = jnp.dot(onehot, yb, preferred_element_type=F32)
    _start_runs(scal_ref, stage.at[slot], sorted_ref, sems[0], to_sorted=True)

    @pl.when(i > 0)
    def _():
        _wait_runs(scal_ref, SCAL_PREV_TOTAL, stage.at[1 - slot], sorted_ref, sems[1],
                   to_sorted=True)

    @pl.when(i == pl.num_programs(0) - 1)
    def _():
        _wait_runs(scal_ref, SCAL_TOTAL, stage.at[slot], sorted_ref, sems[0], to_sorted=True)


def _local_rows(rows):
    worst = rows * TOP_K + N_EXPERTS * (RUN_CHUNK - 1)
    return (worst + LOCAL_BLOCK - 1) // LOCAL_BLOCK * LOCAL_BLOCK


def _dispatch(scal, lpos_t, y1, sorted_rows, *, rows, n_sorted):
    n = y1.shape[0]
    local_rows = _local_rows(rows)
    any_spec = pl.BlockSpec(memory_space=pl.ANY)
    in_specs = [pl.BlockSpec((1, 1, V7X_LANES), lambda i: (i, 0, 0), memory_space=pltpu.SMEM),
                pl.BlockSpec((1, 8, rows), lambda i: (i, 0, 0)),
                pl.BlockSpec((rows, D_MODEL), lambda i: (i, 0))]
    args = [scal, lpos_t, y1]
    aliases = {}
    if sorted_rows is not None:
        in_specs.append(any_spec)
        args.append(sorted_rows)
        aliases = {3: 0}
    return pl.pallas_call(
        functools.partial(_dispatch_kernel, local_rows=local_rows),
        grid=(n // rows,),
        in_specs=in_specs,
        out_specs=any_spec,
        out_shape=jax.ShapeDtypeStruct((n_sorted, D_MODEL), F32),
        scratch_shapes=[pltpu.VMEM((2, local_rows, D_MODEL), F32),
                        pltpu.SemaphoreType.DMA((2,)), pltpu.SemaphoreType.DMA((2,))],
        input_output_aliases=aliases,
        compiler_params=_params("arbitrary"),
        name="dispatch_rows",
    )(*args)


def _expert_kernel(te_ref, tr_ref, x_ref, wgu_ref, bgu_ref, wd_ref, bd_ref, o_ref,
                   wgu_b, wd_b):
    i = pl.program_id(0)
    e = te_ref[i]
    changed = (i == 0) | (e != te_ref[jnp.maximum(i - 1, 0)])

    @pl.when(changed)
    def _():
        wgu_b[...] = wgu_ref[0].astype(BF16)
        wd_b[...] = wd_ref[0].astype(BF16)

    @pl.when(tr_ref[i] == 0)
    def _():
        o_ref[...] = jnp.zeros(o_ref.shape, o_ref.dtype)

    @pl.when(tr_ref[i] > 0)
    def _():
        row = lax.broadcasted_iota(I32, x_ref.shape, 0)
        xb = jnp.where(row < tr_ref[i], x_ref[...], 0.0).astype(BF16)
        bgu = bgu_ref[0]
        acc = None
        n_col = 256
        for j in range(D_FF // n_col):
            cg = slice(j * n_col, (j + 1) * n_col)
            cl = slice(D_FF + j * n_col, D_FF + (j + 1) * n_col)
            hg = jnp.dot(xb, wgu_b[:, cg], preferred_element_type=F32) + bgu[:, cg]
            hl = jnp.dot(xb, wgu_b[:, cl], preferred_element_type=F32) + bgu[:, cl]
            glu = jnp.minimum(hg, SWIGLU_LIMIT)
            lin = jnp.clip(hl, -SWIGLU_LIMIT, SWIGLU_LIMIT)
            act = glu * jax.nn.sigmoid(SWIGLU_ALPHA * glu) * (lin + 1.0)
            part = jnp.dot(act.astype(BF16), wd_b[cg, :], preferred_element_type=F32)
            acc = part if acc is None else acc + part
        o_ref[...] = acc + bd_ref[0]


def _experts(tile_expert, tile_rows, xs, w_gate_up, b_gate_up, w_down, b_down):
    n_tiles = xs.shape[0] // EXPERT_TILE
    grid_spec = pltpu.PrefetchScalarGridSpec(
        num_scalar_prefetch=2,
        grid=(n_tiles,),
        in_specs=[
            pl.BlockSpec((EXPERT_TILE, D_MODEL), lambda i, te, tr: (i, 0)),
            pl.BlockSpec((1, D_MODEL, 2 * D_FF), lambda i, te, tr: (te[i], 0, 0)),
            pl.BlockSpec((1, 1, 2 * D_FF), lambda i, te, tr: (te[i], 0, 0)),
            pl.BlockSpec((1, D_FF, D_MODEL), lambda i, te, tr: (te[i], 0, 0)),
            pl.BlockSpec((1, 1, D_MODEL), lambda i, te, tr: (te[i], 0, 0)),
        ],
        out_specs=pl.BlockSpec((EXPERT_TILE, D_MODEL), lambda i, te, tr: (i, 0)),
        scratch_shapes=[pltpu.VMEM((D_MODEL, 2 * D_FF), BF16), pltpu.VMEM((D_FF, D_MODEL), BF16)],
    )
    return pl.pallas_call(
        _expert_kernel,
        grid_spec=grid_spec,
        out_shape=jax.ShapeDtypeStruct(xs.shape, F32),
        compiler_params=_params("arbitrary"),
        name="expert_ffn",
    )(tile_expert, tile_rows, xs, w_gate_up, b_gate_up, w_down, b_down)


def _combine_kernel(scal_ref, next_scal_ref, lpos_ref, gate_ref, y_ref, g_ref, b_ref, eo_ref,
                    out_ref, buf, big_sems, small_sems, *, local_rows):
    i = pl.program_id(0)
    slot = i % 2
    sems = [(big_sems.at[s], small_sems.at[s]) for s in (slot, 1 - slot)]

    @pl.when(i == 0)
    def _():
        buf[...] = jnp.zeros(buf.shape, buf.dtype)
        _start_runs(scal_ref, buf.at[0], eo_ref, (big_sems.at[0], small_sems.at[0]), to_sorted=False)

    @pl.when(i + 1 < pl.num_programs(0))
    def _():
        _start_runs(next_scal_ref, buf.at[1 - slot], eo_ref, sems[1], to_sorted=False)

    _wait_runs(scal_ref, SCAL_TOTAL, buf.at[slot], eo_ref, sems[0], to_sorted=False)
    rows = y_ref.shape[0]
    lpos = lpos_ref[...]
    gate = gate_ref[...]
    blk = LOCAL_BLOCK
    lane = lax.broadcasted_iota(I32, (rows, V7X_LANES), 1)
    slot_row = [jnp.broadcast_to(lpos[:, k:k + 1], (rows, V7X_LANES)) for k in range(TOP_K)]
    slot_gate = [jnp.broadcast_to(gate[:, k:k + 1], (rows, V7X_LANES)) for k in range(TOP_K)]
    moe = None
    for kb in range(local_rows // blk):
        pieces = []
        for jb in range(blk // V7X_LANES):
            col = lane + (kb * blk + jb * V7X_LANES)
            w = jnp.zeros((rows, V7X_LANES), F32)
            for k in range(TOP_K):
                w = jnp.where(col == slot_row[k], slot_gate[k], w)
            pieces.append(w)
        weights = jnp.concatenate(pieces, axis=1).astype(BF16)
        part = jnp.dot(weights, buf[slot, kb * blk:(kb + 1) * blk, :].astype(BF16),
                       preferred_element_type=F32)
        moe = part if moe is None else moe + part
    out_ref[...] = _layer_norm(DN_ALPHA * y_ref[...] + moe, g_ref[...], b_ref[...])


def _combine(scal, lpos, gates, y1, ln_g, ln_b, expert_out, *, rows):
    n = y1.shape[0]
    local_rows = _local_rows(rows)
    wide = pl.BlockSpec((rows, D_MODEL), lambda i: (i, 0))
    lanes = pl.BlockSpec((rows, V7X_LANES), lambda i: (i, 0))
    n_steps = n // rows
    scalars = pl.BlockSpec((1, 1, V7X_LANES), lambda i: (i, 0, 0), memory_space=pltpu.SMEM)
    next_scalars = pl.BlockSpec((1, 1, V7X_LANES), lambda i: (jnp.minimum(i + 1, n_steps - 1), 0, 0),
                                memory_space=pltpu.SMEM)
    return pl.pallas_call(
        functools.partial(_combine_kernel, local_rows=local_rows),
        grid=(n_steps,),
        in_specs=[scalars, next_scalars, lanes, lanes, wide, _resident(ln_g.shape),
                  _resident(ln_b.shape), pl.BlockSpec(memory_space=pl.ANY)],
        out_specs=wide,
        out_shape=jax.ShapeDtypeStruct((n, D_MODEL), F32),
        scratch_shapes=[pltpu.VMEM((2, local_rows, D_MODEL), F32),
                        pltpu.SemaphoreType.DMA((2,)), pltpu.SemaphoreType.DMA((2,))],
        compiler_params=_params("arbitrary"),
        name="combine_norm",
    )(scal, scal, lpos, gates, y1, ln_g, ln_b, expert_out)


def _tile_tables(lpos, n_chunk, sorted0, rows):
    n_tiles = lpos.shape[0] // rows
    local0 = (jnp.cumsum(n_chunk, axis=1) - n_chunk) * RUN_CHUNK
    n_big = n_chunk // BIG_CHUNKS
    total = jnp.stack([jnp.sum(n_big, axis=1), jnp.sum(n_chunk - n_big * BIG_CHUNKS, axis=1)], axis=1)
    prev_total = jnp.concatenate([jnp.zeros((1, 2), I32), total[:-1]], axis=0)
    pad = jnp.zeros((n_tiles, V7X_LANES - SCAL_PREV_TOTAL - 2), I32)
    scal = jnp.concatenate([sorted0, n_chunk, local0, total, prev_total, pad], axis=1)
    lpos_t = jnp.transpose(lpos[:, :8].reshape(n_tiles, rows, 8), (0, 2, 1))
    return scal.reshape(n_tiles, 1, V7X_LANES), lpos_t


def kernel(x_prompt, x_sample, mem_prompt, cache_attn_k, cache_attn_v, cache_conv, cache_mem_k,
           cache_mem_v, w_in, rel_bias, conv_w, w_mem_kv, w_gate, w_br_attn, w_br_conv, w_br_mem,
           w_out, ln1_g, ln1_b, w_router, b_router, w_gate_up, b_gate_up, w_down, b_down,
           ln2_g, ln2_b):
    depth = w_in.shape[0]
    assert depth == 1, "one layer only"
    batch, seq, _ = x_prompt.shape
    dec_batch, dec_seq, _ = x_sample.shape
    n_prompt = batch * seq
    n_sample = dec_batch * dec_seq
    assert seq % ROW_TILE == 0 and HIST == ROW_TILE and cache_attn_k.shape[2] == HIST
    assert dec_seq % 16 == 0 and dec_seq <= CHUNK and cache_conv.shape[2] == 2

    w_in_b = w_in[0].astype(BF16)
    w_mem_b = w_mem_kv[0].astype(BF16)
    wr = w_router[0]
    wr_hi = wr.astype(BF16)
    wr_lo = (wr - wr_hi.astype(F32)).astype(BF16)
    merge_w = (w_gate[0].astype(BF16), w_br_attn[0].astype(BF16), w_br_conv[0].astype(BF16),
               w_br_mem[0].astype(BF16), w_out[0].astype(BF16), ln1_g, ln1_b,
               jnp.concatenate([wr_hi, wr_lo], axis=1), wr_hi, b_router)
    conv_w8 = jnp.pad(conv_w[0], ((0, 5), (0, 0)))
    bias = _band_bias(rel_bias[0])

    xp = x_prompt.reshape(n_prompt, D_MODEL)
    mk_f, mv_f, mk_b, mv_b = _memory_kv(mem_prompt.reshape(batch * MEM_TOKENS, D_MODEL), w_mem_b)
    q, k_pad, v_pad, c, qm, k_tail, v_tail, u_tail = _proj_in(
        xp, w_in_b, conv_w8, jnp.zeros((batch, 8, WIDTH), F32),
        batch=batch, seq=seq, rows=ROW_TILE, pad_steps=1)
    a, m = _attention(
        q, k_pad, v_pad, bias.reshape(ATT_HEADS * CHUNK, WINDOW), qm,
        mk_b.reshape(batch, MEM_TOKENS, WIDTH),
        mv_b.reshape(batch, MEM_TOKENS, WIDTH), batch=batch, seq=seq, rows=ROW_TILE,
        chunk_rows=CHUNK, valid_keys=BAND, first_chunk=0, mem_rows=256)
    y1_p, sel_p, topi_p, topg_p = _merge(xp, a, c, m, merge_w, rows=MERGE_TILE)

    xs = x_sample.reshape(n_sample, D_MODEL)
    conv_init = jnp.pad(cache_conv[0], ((0, 0), (6, 0), (0, 0)))
    q_s, k_s, v_s, c_s, qm_s, k_new, v_new, u_tail_s = _proj_in(
        xs, w_in_b, conv_w8, conv_init, batch=dec_batch, seq=dec_seq, rows=dec_seq, pad_steps=0)
    pad_rows = WINDOW - HIST - dec_seq

    def window(cache, new):
        cache = cache[0].reshape(dec_batch, HIST, WIDTH).astype(BF16)
        return jnp.pad(jnp.concatenate([cache, new], axis=1), ((0, 0), (0, pad_rows), (0, 0)))

    a_s, m_s = _attention(
        q_s, window(cache_attn_k, k_s), window(cache_attn_v, v_s),
        bias[:, :dec_seq].reshape(ATT_HEADS * dec_seq, WINDOW), qm_s,
        cache_mem_k[0].reshape(dec_batch, MEM_TOKENS, WIDTH).astype(BF16),
        cache_mem_v[0].reshape(dec_batch, MEM_TOKENS, WIDTH).astype(BF16),
        batch=dec_batch, seq=dec_seq, rows=dec_seq, chunk_rows=dec_seq,
        valid_keys=HIST + dec_seq, first_chunk=LEFT_CHUNKS, mem_rows=dec_seq)
    y1_s, sel_s, topi_s, topg_s = _merge(xs, a_s, c_s, m_s, merge_w, rows=n_sample)

    lpos_p, before_p, cnt_p = _ranks(sel_p, topi_p, jnp.zeros((8, N_EXPERTS), F32), rows=ROW_TILE)
    lpos_s, before_s, cnt = _ranks(sel_s, topi_s, cnt_p, rows=n_sample)
    before = jnp.concatenate([before_p[:, 0], before_s[:, 0]], axis=0).astype(I32)
    n_run = jnp.concatenate([before[1:], cnt[:1].astype(I32)], axis=0) - before
    n_chunk = (n_run + RUN_CHUNK - 1) // RUN_CHUNK
    run_rows = n_chunk * RUN_CHUNK
    counts = jnp.sum(run_rows, axis=0)
    padded = (counts + EXPERT_TILE - 1) // EXPERT_TILE * EXPERT_TILE
    ends = jnp.cumsum(padded)
    starts = ends - padded
    sorted0 = starts[None, :] + jnp.cumsum(run_rows, axis=0) - run_rows
    n_token_tiles = n_prompt // ROW_TILE + 1
    n_tiles = ((n_prompt + n_sample) * TOP_K
               + n_token_tiles * N_EXPERTS * (RUN_CHUNK - 1)) // EXPERT_TILE + N_EXPERTS
    tile_start = jnp.arange(n_tiles, dtype=I32) * EXPERT_TILE
    tile_expert = jnp.minimum(
        jnp.sum((ends[None, :] <= tile_start[:, None]).astype(I32), axis=1), N_EXPERTS - 1)
    tile_rows = jnp.clip(starts[tile_expert] + counts[tile_expert] - tile_start, 0, EXPERT_TILE)
    scal_p, lpos_tp = _tile_tables(lpos_p, n_chunk[:-1], sorted0[:-1], ROW_TILE)
    scal_s, lpos_ts = _tile_tables(lpos_s, n_chunk[-1:], sorted0[-1:], n_sample)

    n_sorted = n_tiles * EXPERT_TILE
    sorted_rows = _dispatch(scal_p, lpos_tp, y1_p, None, rows=ROW_TILE, n_sorted=n_sorted)
    sorted_rows = _dispatch(scal_s, lpos_ts, y1_s, sorted_rows, rows=n_sample, n_sorted=n_sorted)
    expert_out = _experts(tile_expert, tile_rows, sorted_rows, w_gate_up[0],
                          b_gate_up[0].reshape(N_EXPERTS, 1, 2 * D_FF), w_down[0],
                          b_down[0].reshape(N_EXPERTS, 1, D_MODEL))
    y_p = _combine(scal_p, lpos_p, topg_p, y1_p, ln2_g, ln2_b, expert_out, rows=ROW_TILE)
    y_s = _combine(scal_s, lpos_s, topg_s, y1_s, ln2_g, ln2_b, expert_out, rows=n_sample)

    def heads(t, n_b, n_rows, n_heads, dim):
        return t.reshape(1, n_b, n_rows, n_heads, dim)

    return (y_p.reshape(batch, seq, D_MODEL),
            y_s.reshape(dec_batch, dec_seq, D_MODEL),
            heads(k_tail, batch, HIST, ATT_HEADS, ATT_DIM),
            heads(v_tail, batch, HIST, ATT_HEADS, ATT_DIM),
            u_tail[:, 6:8].reshape(1, batch, 2, WIDTH),
            heads(mk_f, batch, MEM_TOKENS, MEM_HEADS, MEM_DIM),
            heads(mv_f, batch, MEM_TOKENS, MEM_HEADS, MEM_DIM),
            heads(k_new, dec_batch, dec_seq, ATT_HEADS, ATT_DIM),
            heads(v_new, dec_batch, dec_seq, ATT_HEADS, ATT_DIM),
            u_tail_s[:, 6:8].reshape(1, dec_batch, 2, WIDTH))
```

```python
import functools

import jax
import jax.numpy as jnp
from jax import lax
from jax.experimental import pallas as pl
from jax.experimental.pallas import tpu as pltpu

F32 = jnp.float32
BF16 = jnp.bfloat16
I32 = jnp.int32

D_MODEL = 1024
CHUNK = 64
LEFT_CHUNKS = 8
HIST = LEFT_CHUNKS * CHUNK
BAND = HIST + CHUNK
ATT_HEADS = 8
ATT_DIM = 64
REL_MAX = 128
WIDTH = 512
MEM_TOKENS = 256
MEM_HEADS = 4
MEM_DIM = 128
N_EXPERTS = 32
TOP_K = 4
D_FF = 1024
SWIGLU_LIMIT = 7.0
SWIGLU_ALPHA = 1.702
DN_ALPHA = 2.0 ** 0.25
LN_EPS = 1e-5
NEG_INF = -1e30

V7X_LANES = 128
V7X_VMEM_BYTES = 64 * 1024 * 1024
VMEM_LIMIT = V7X_VMEM_BYTES - 8 * 1024 * 1024

ROW_TILE = 512
MERGE_TILE = 1024
MOE_TILE = 256
WINDOW = BAND + CHUNK
EXPERT_TILE = 1024
RUN_CHUNK = 8
LOCAL_BLOCK = 256


def _params(*sem):
    return pltpu.CompilerParams(dimension_semantics=sem, vmem_limit_bytes=VMEM_LIMIT)


def _resident(shape):
    nd = len(shape)
    return pl.BlockSpec(shape, lambda *_: (0,) * nd, pipeline_mode=pl.Buffered(1))


def _layer_norm(r, g, b):
    mu = jnp.mean(r, axis=-1, keepdims=True)
    d = r - mu
    var = jnp.mean(d * d, axis=-1, keepdims=True)
    return d * lax.rsqrt(var + LN_EPS) * g + b


def _memkv_kernel(x_ref, w_ref, kf_ref, vf_ref, kb_ref, vb_ref):
    y = jnp.dot(x_ref[...].astype(BF16), w_ref[...], preferred_element_type=F32)
    k = y[:, :WIDTH]
    v = y[:, WIDTH:]
    kf_ref[...] = k
    vf_ref[...] = v
    kb_ref[...] = k.astype(BF16)
    vb_ref[...] = v.astype(BF16)


def _memory_kv(mem2d, w_b):
    rows = mem2d.shape[0]
    tile = pl.BlockSpec((ROW_TILE, D_MODEL), lambda i: (i, 0))
    half = pl.BlockSpec((ROW_TILE, WIDTH), lambda i: (i, 0))
    return pl.pallas_call(
        _memkv_kernel,
        grid=(rows // ROW_TILE,),
        in_specs=[tile, _resident((D_MODEL, 2 * WIDTH))],
        out_specs=[half, half, half, half],
        out_shape=[jax.ShapeDtypeStruct((rows, WIDTH), F32)] * 2
        + [jax.ShapeDtypeStruct((rows, WIDTH), BF16)] * 2,
        compiler_params=_params("arbitrary"),
        name="memory_kv",
    )(mem2d, w_b)


def _proj_in_kernel(x_ref, w_ref, cw_ref, cinit_ref,
                    q_ref, k_ref, v_ref, c_ref, qm_ref, kt_ref, vt_ref, ut_ref,
                    carry_ref, *, pad_steps):
    s = pl.program_id(1)
    ns = pl.num_programs(1)
    rows = x_ref.shape[0]
    first = pad_steps
    last = ns - 1 - pad_steps

    if pad_steps:
        @pl.when((s < first) | (s > last))
        def _():
            k_ref[...] = jnp.zeros(k_ref.shape, k_ref.dtype)
            v_ref[...] = jnp.zeros(v_ref.shape, v_ref.dtype)

    @pl.when((s >= first) & (s <= last))
    def _():
        @pl.when(s == first)
        def _():
            carry_ref[...] = cinit_ref[0]

        xb = x_ref[...].astype(BF16)

        def proj(g):
            return jnp.dot(xb, w_ref[:, g * WIDTH:(g + 1) * WIDTH], preferred_element_type=F32)

        q_ref[...] = (proj(0) * (ATT_DIM ** -0.5)).astype(BF16)
        k = proj(1)
        v = proj(2)
        k_ref[0] = k.astype(BF16)
        v_ref[0] = v.astype(BF16)

        @pl.when(s == last)
        def _():
            kt_ref[0] = k
            vt_ref[0] = v

        bg = proj(3)
        u = proj(4) * proj(5)
        prev = carry_ref[...]
        row = lax.broadcasted_iota(I32, u.shape, 0)
        u1 = jnp.where(row == 0, prev[7:8], pltpu.roll(u, 1, 0))
        u2 = jnp.where(row == 0, prev[6:7], jnp.where(row == 1, prev[7:8], pltpu.roll(u, 2, 0)))
        cw = cw_ref[...]
        c_ref[...] = (bg * (cw[0:1] * u2 + cw[1:2] * u1 + cw[2:3] * u)).astype(BF16)
        tail = u[rows - 8:]
        carry_ref[...] = tail

        @pl.when(s == last)
        def _():
            ut_ref[0] = tail

        qm_ref[...] = proj(6).astype(BF16)


def _proj_in(x2d, w_b, conv_w8, conv_init, *, batch, seq, rows, pad_steps):
    n_data = seq // rows
    n_steps = n_data + 2 * pad_steps

    def data_idx(b, s):
        return (b * n_data + jnp.clip(s - pad_steps, 0, n_data - 1), 0)

    wide = pl.BlockSpec((rows, D_MODEL), data_idx)
    narrow = pl.BlockSpec((rows, WIDTH), data_idx)
    seq_blk = pl.BlockSpec((1, rows, WIDTH), lambda b, s: (b, s, 0))
    tail_blk = pl.BlockSpec((1, rows, WIDTH), lambda b, s: (b, 0, 0))
    tail8 = pl.BlockSpec((1, 8, WIDTH), lambda b, s: (b, 0, 0))
    tok = jax.ShapeDtypeStruct((batch * seq, WIDTH), BF16)
    kv = jax.ShapeDtypeStruct((batch, n_steps * rows, WIDTH), BF16)
    tail = jax.ShapeDtypeStruct((batch, rows, WIDTH), F32)
    return pl.pallas_call(
        functools.partial(_proj_in_kernel, pad_steps=pad_steps),
        grid=(batch, n_steps),
        in_specs=[wide, _resident(w_b.shape), _resident(conv_w8.shape), tail8],
        out_specs=[narrow, seq_blk, seq_blk, narrow, narrow, tail_blk, tail_blk, tail8],
        out_shape=[tok, kv, kv, tok, tok, tail, tail,
                   jax.ShapeDtypeStruct((batch, 8, WIDTH), F32)],
        scratch_shapes=[pltpu.VMEM((8, WIDTH), F32)],
        compiler_params=_params("arbitrary", "arbitrary"),
        name="proj_in",
    )(x2d, w_b, conv_w8, conv_init)


def _bias_kernel(tab_ref, out_ref):
    shape = (CHUNK, WINDOW)
    i = lax.broadcasted_iota(I32, shape, 0)
    j = lax.broadcasted_iota(I32, shape, 1)
    idx = jnp.clip(HIST + i - j, -REL_MAX, REL_MAX) + REL_MAX
    for h in range(ATT_HEADS):
        def body(d, acc, h=h):
            return jnp.where(idx == d, tab_ref[h, d], acc)
        out_ref[h] = lax.fori_loop(0, 2 * REL_MAX + 1, body, jnp.zeros(shape, F32))


def _band_bias(rel_table):
    return pl.pallas_call(
        _bias_kernel,
        in_specs=[pl.BlockSpec(memory_space=pltpu.SMEM)],
        out_shape=jax.ShapeDtypeStruct((ATT_HEADS, CHUNK, WINDOW), F32),
        name="band_bias",
    )(rel_table)


def _attention_kernel(q_ref, k_ref, v_ref, bias_ref, qm_ref, mk_ref, mv_ref,
                      a_ref, m_ref, *, chunk_rows, n_chunks, valid_keys, first_chunk, mem_rows):
    s = pl.program_id(1)
    nr = chunk_rows
    pair_w = 2 * ATT_DIM
    col = lax.broadcasted_iota(I32, (ATT_HEADS * nr, WINDOW), 1)
    lane2 = lax.broadcasted_iota(I32, (2 * nr, pair_w), 1)
    row2 = lax.broadcasted_iota(I32, (2 * nr, pair_w), 0)
    own = (lane2 >= ATT_DIM) == (row2 >= nr)
    low_lanes = lax.broadcasted_iota(I32, (nr, pair_w), 1) < ATT_DIM
    nt = (((1,), (1,)), ((), ()))

    def chunk(c, carry):
        g = first_chunk + s * n_chunks + c
        r0 = pl.multiple_of(c * nr, nr)
        w0 = pl.multiple_of((s * n_chunks + c) * CHUNK, CHUNK)
        qc = q_ref[pl.ds(r0, nr), :].astype(F32)
        kw = k_ref[0, pl.ds(w0, WINDOW), :]
        vw = v_ref[0, pl.ds(w0, WINDOW), :]
        scores = []
        for pair in range(ATT_HEADS // 2):
            lanes = slice(pair * pair_w, (pair + 1) * pair_w)
            qp = qc[:, lanes]
            q2 = jnp.where(own, jnp.concatenate([qp, qp], axis=0), 0.0).astype(BF16)
            scores.append(lax.dot_general(q2, kw[:, lanes], nt, preferred_element_type=F32))
        sc = jnp.concatenate(scores, axis=0) + bias_ref[...]
        sc = jnp.where((col >= HIST - CHUNK * g) & (col < valid_keys), sc, NEG_INF)
        m = jnp.max(sc, axis=-1, keepdims=True)
        p = jnp.exp(sc - m)
        inv = 1.0 / jnp.sum(p, axis=-1, keepdims=True)
        pb = p.astype(BF16)
        outs = []
        for pair in range(ATT_HEADS // 2):
            lanes = slice(pair * pair_w, (pair + 1) * pair_w)
            rows = slice(pair * 2 * nr, (pair + 1) * 2 * nr)
            o = jnp.dot(pb[rows], vw[:, lanes], preferred_element_type=F32) * inv[rows]
            outs.append(jnp.where(low_lanes, o[:nr], o[nr:]))
        a_ref[pl.ds(r0, nr), :] = jnp.concatenate(outs, axis=1).astype(BF16)
        return carry

    lax.fori_loop(0, n_chunks, chunk, 0, unroll=min(n_chunks, 4))

    rows = qm_ref.shape[0]
    for rb in range(rows // mem_rows):
        rs = slice(rb * mem_rows, (rb + 1) * mem_rows)
        scores = []
        for h in range(MEM_HEADS):
            lanes = slice(h * MEM_DIM, (h + 1) * MEM_DIM)
            scores.append(lax.dot_general(qm_ref[rs, lanes], mk_ref[0, :, lanes], nt,
                                          preferred_element_type=F32))
        sc = jnp.concatenate(scores, axis=0) * (MEM_DIM ** -0.5)
        m = jnp.max(sc, axis=-1, keepdims=True)
        p = jnp.exp(sc - m)
        inv = 1.0 / jnp.sum(p, axis=-1, keepdims=True)
        pb = p.astype(BF16)
        for h in range(MEM_HEADS):
            lanes = slice(h * MEM_DIM, (h + 1) * MEM_DIM)
            hs = slice(h * mem_rows, (h + 1) * mem_rows)
            o = jnp.dot(pb[hs], mv_ref[0, :, lanes], preferred_element_type=F32) * inv[hs]
            m_ref[rs, lanes] = o.astype(BF16)


def _attention(q, k_pad, v_pad, bias, qm, mk, mv, *, batch, seq, rows, chunk_rows,
               valid_keys, first_chunk, mem_rows):
    n_steps = seq // rows
    tok = pl.BlockSpec((rows, WIDTH), lambda b, s: (b * n_steps + s, 0))
    whole_seq = pl.BlockSpec((1, k_pad.shape[1], WIDTH), lambda b, s: (b, 0, 0))
    mem = pl.BlockSpec((1, MEM_TOKENS, WIDTH), lambda b, s: (b, 0, 0))
    kern = functools.partial(
        _attention_kernel, chunk_rows=chunk_rows, n_chunks=rows // chunk_rows,
        valid_keys=valid_keys, first_chunk=first_chunk, mem_rows=mem_rows)
    out = jax.ShapeDtypeStruct((batch * seq, WIDTH), BF16)
    return pl.pallas_call(
        kern,
        grid=(batch, n_steps),
        in_specs=[tok, whole_seq, whole_seq, _resident(bias.shape), tok, mem, mem],
        out_specs=[tok, tok],
        out_shape=[out, out],
        compiler_params=_params("arbitrary", "arbitrary"),
        name="attention",
    )(q, k_pad, v_pad, bias, qm, mk, mv)


def _merge_kernel(x_ref, a_ref, c_ref, m_ref, wg_ref, wa_ref, wc_ref, wm_ref, wo_ref,
                  g_ref, b_ref, wr2_ref, wrh_ref, br_ref,
                  y_ref, sel_ref, topi_ref, topg_ref):
    x = x_ref[...]
    xb = x.astype(BF16)
    branches = [(a_ref[...], wa_ref), (c_ref[...], wc_ref), (m_ref[...], wm_ref)]
    out = None
    n_col = 256
    for j in range(D_MODEL // n_col):
        cols = slice(j * n_col, (j + 1) * n_col)
        comb = None
        for i, (br, wbr) in enumerate(branches):
            gcols = slice(i * D_MODEL + j * n_col, i * D_MODEL + (j + 1) * n_col)
            gate = jax.nn.sigmoid(jnp.dot(xb, wg_ref[:, gcols], preferred_element_type=F32))
            term = gate * jnp.dot(br, wbr[:, cols], preferred_element_type=F32)
            comb = term if comb is None else comb + term
        part = jnp.dot(comb.astype(BF16), wo_ref[cols, :], preferred_element_type=F32)
        out = part if out is None else out + part
    y = _layer_norm(DN_ALPHA * x + out, g_ref[...], b_ref[...])
    y_ref[...] = y

    y_hi = y.astype(BF16)
    y_lo = (y - y_hi.astype(F32)).astype(BF16)
    both = jnp.dot(y_hi, wr2_ref[...], preferred_element_type=F32)
    logits = (both[:, :N_EXPERTS] + both[:, N_EXPERTS:]
              + jnp.dot(y_lo, wrh_ref[...], preferred_element_type=F32) + br_ref[...])

    rows = logits.shape[0]
    lane = lax.broadcasted_iota(I32, (rows, N_EXPERTS), 1).astype(F32)
    lane_out = lax.broadcasted_iota(I32, (rows, V7X_LANES), 1)
    work = logits
    sel = jnp.zeros((rows, N_EXPERTS), F32)
    topi = jnp.zeros((rows, V7X_LANES), I32)
    vals = []
    for k in range(TOP_K):
        top = jnp.max(work, axis=-1, keepdims=True)
        idx = jnp.min(jnp.where(work == top, lane, float(N_EXPERTS)), axis=-1, keepdims=True)
        hit = lane == idx
        sel = jnp.where(hit, 1.0, sel)
        work = jnp.where(hit, -jnp.inf, work)
        topi = jnp.where(lane_out == k, idx.astype(I32), topi)
        vals.append(top)
    exps = [jnp.exp(v - vals[0]) for v in vals]
    denom = exps[0] + exps[1] + exps[2] + exps[3]
    topg = jnp.zeros((rows, V7X_LANES), F32)
    for k in range(TOP_K):
        topg = jnp.where(lane_out == k, exps[k] / denom, topg)
    sel_ref[...] = sel.astype(BF16)
    topi_ref[...] = topi
    topg_ref[...] = topg


def _merge(x2d, a, c, m, weights, *, rows):
    n = x2d.shape[0]
    wide = pl.BlockSpec((rows, D_MODEL), lambda i: (i, 0))
    narrow = pl.BlockSpec((rows, WIDTH), lambda i: (i, 0))
    lanes = pl.BlockSpec((rows, V7X_LANES), lambda i: (i, 0))
    return pl.pallas_call(
        _merge_kernel,
        grid=(n // rows,),
        in_specs=[wide, narrow, narrow, narrow] + [_resident(w.shape) for w in weights],
        out_specs=[wide, pl.BlockSpec((rows, N_EXPERTS), lambda i: (i, 0)), lanes, lanes],
        out_shape=[jax.ShapeDtypeStruct((n, D_MODEL), F32),
                   jax.ShapeDtypeStruct((n, N_EXPERTS), BF16),
                   jax.ShapeDtypeStruct((n, V7X_LANES), I32),
                   jax.ShapeDtypeStruct((n, V7X_LANES), F32)],
        compiler_params=_params("arbitrary"),
        name="merge_router",
    )(x2d, a, c, m, *weights)


def _rank_kernel(sel_ref, topi_ref, init_ref, lpos_ref, before_ref, cnt_ref, carry_ref):
    @pl.when(pl.program_id(0) == 0)
    def _():
        carry_ref[...] = init_ref[...]

    before_ref[0] = carry_ref[...]
    sel = sel_ref[...]
    rows = sel.shape[0]
    r = lax.broadcasted_iota(I32, (rows, rows), 0)
    c = lax.broadcasted_iota(I32, (rows, rows), 1)
    earlier = jnp.where(c < r, 1.0, 0.0).astype(BF16)
    in_tile = jnp.dot(earlier, sel, preferred_element_type=F32)
    n_run = jnp.sum(sel.astype(F32), axis=0, keepdims=True)
    run_rows = jnp.ceil(n_run * (1.0 / RUN_CHUNK)) * RUN_CHUNK
    er = lax.broadcasted_iota(I32, (N_EXPERTS, N_EXPERTS), 0)
    ec = lax.broadcasted_iota(I32, (N_EXPERTS, N_EXPERTS), 1)
    lower_experts = jnp.where(er < ec, 1.0, 0.0).astype(BF16)
    local0 = jnp.dot(jnp.broadcast_to(run_rows, (8, N_EXPERTS)).astype(BF16), lower_experts,
                     preferred_element_type=F32)[0:1]
    pos = in_tile + local0
    lane = lax.broadcasted_iota(I32, (rows, N_EXPERTS), 1)
    lane_out = lax.broadcasted_iota(I32, (rows, V7X_LANES), 1)
    topi = topi_ref[...]
    lpos = jnp.full((rows, V7X_LANES), -1, I32)
    for k in range(TOP_K):
        mine = jnp.sum(jnp.where(lane == topi[:, k:k + 1], pos, 0.0), axis=-1, keepdims=True)
        lpos = jnp.where(lane_out == k, mine.astype(I32), lpos)
    lpos_ref[...] = lpos
    total = carry_ref[...] + n_run
    carry_ref[...] = total
    cnt_ref[...] = total


def _ranks(sel, topi, init, *, rows):
    n = sel.shape[0]
    small = pl.BlockSpec((8, N_EXPERTS), lambda i: (0, 0))
    lanes = pl.BlockSpec((rows, V7X_LANES), lambda i: (i, 0))
    return pl.pallas_call(
        _rank_kernel,
        grid=(n // rows,),
        in_specs=[pl.BlockSpec((rows, N_EXPERTS), lambda i: (i, 0)), lanes, small],
        out_specs=[lanes, pl.BlockSpec((1, 8, N_EXPERTS), lambda i: (i, 0, 0)), small],
        out_shape=[jax.ShapeDtypeStruct((n, V7X_LANES), I32),
                   jax.ShapeDtypeStruct((n // rows, 8, N_EXPERTS), F32),
                   jax.ShapeDtypeStruct((8, N_EXPERTS), F32)],
        scratch_shapes=[pltpu.VMEM((8, N_EXPERTS), F32)],
        compiler_params=_params("arbitrary"),
        name="expert_ranks",
    )(sel, topi, init)


SCAL_SORTED0 = 0
SCAL_CHUNKS = N_EXPERTS
SCAL_LOCAL0 = 2 * N_EXPERTS
SCAL_TOTAL = 3 * N_EXPERTS
SCAL_PREV_TOTAL = SCAL_TOTAL + 2
BIG_CHUNKS = 8
COPY_ROWS = (BIG_CHUNKS * RUN_CHUNK, RUN_CHUNK)


def _run_copy(local_ref, local_row, sorted_ref, sorted_row, sems, size, to_sorted):
    loc = local_ref.at[pl.ds(local_row, COPY_ROWS[size])]
    srt = sorted_ref.at[pl.ds(sorted_row, COPY_ROWS[size])]
    sem = sems[size]
    return pltpu.make_async_copy(loc, srt, sem) if to_sorted else pltpu.make_async_copy(srt, loc, sem)


def _start_runs(scal_ref, local_ref, sorted_ref, sems, *, to_sorted):
    def expert(e, carry):
        sorted0 = scal_ref[0, 0, SCAL_SORTED0 + e]
        local0 = scal_ref[0, 0, SCAL_LOCAL0 + e]
        n_chunk = scal_ref[0, 0, SCAL_CHUNKS + e]
        n_big = n_chunk // BIG_CHUNKS
        rest0 = n_big * COPY_ROWS[0]

        def copy(size, offset):
            return _run_copy(local_ref, pl.multiple_of(local0 + offset, RUN_CHUNK), sorted_ref,
                             pl.multiple_of(sorted0 + offset, RUN_CHUNK), sems, size, to_sorted)

        def big(c, inner):
            copy(0, c * COPY_ROWS[0]).start()
            return inner

        def small(c, inner):
            copy(1, rest0 + c * COPY_ROWS[1]).start()
            return inner

        lax.fori_loop(0, n_big, big, 0)
        lax.fori_loop(0, n_chunk - n_big * BIG_CHUNKS, small, 0)
        return carry

    lax.fori_loop(0, N_EXPERTS, expert, 0)


def _wait_runs(scal_ref, total_lane, local_ref, sorted_ref, sems, *, to_sorted):
    for size in range(len(COPY_ROWS)):
        def drain(c, carry, size=size):
            _run_copy(local_ref, 0, sorted_ref, 0, sems, size, to_sorted).wait()
            return carry

        lax.fori_loop(0, scal_ref[0, 0, total_lane + size], drain, 0)


def _dispatch_kernel(scal_ref, lpos_ref, y_ref, *rest, local_rows):
    sorted_ref, stage, big_sems, small_sems = rest[-4:]
    i = pl.program_id(0)
    slot = i % 2
    sems = [(big_sems.at[s], small_sems.at[s]) for s in (slot, 1 - slot)]
    rows = y_ref.shape[0]
    yb = y_ref[...].astype(BF16)
    lpos = lpos_ref[0]
    blk = LOCAL_BLOCK
    for rb in range(local_rows // blk):
        j = lax.broadcasted_iota(I32, (blk, rows), 0) + rb * blk
        onehot = jnp.zeros((blk, rows), F32)
        for k in range(TOP_K):
            onehot = jnp.where(j == lpos[k:k + 1], 1.0, onehot)
        onehot = onehot.astype(BF16)
        stage[slot, rb * blk:(rb + 1) * blk, :] = jnp.dot(onehot, yb, preferred_element_type=F32)
    _start_runs(scal_ref, stage.at[slot], sorted_ref, sems[0], to_sorted=True)

    @pl.when(i > 0)
    def _():
        _wait_runs(scal_ref, SCAL_PREV_TOTAL, stage.at[1 - slot], sorted_ref, sems[1],
                   to_sorted=True)

    @pl.when(i == pl.num_programs(0) - 1)
    def _():
        _wait_runs(scal_ref, SCAL_TOTAL, stage.at[slot], sorted_ref, sems[0], to_sorted=True)


def _local_rows(rows):
    worst = rows * TOP_K + N_EXPERTS * (RUN_CHUNK - 1)
    return (worst + LOCAL_BLOCK - 1) // LOCAL_BLOCK * LOCAL_BLOCK


def _dispatch(scal, lpos_t, y1, sorted_rows, *, rows, n_sorted):
    n = y1.shape[0]
    local_rows = _local_rows(rows)
    any_spec = pl.BlockSpec(memory_space=pl.ANY)
    in_specs = [pl.BlockSpec((1, 1, V7X_LANES), lambda i: (i, 0, 0), memory_space=pltpu.SMEM),
                pl.BlockSpec((1, 8, rows), lambda i: (i, 0, 0)),
                pl.BlockSpec((rows, D_MODEL), lambda i: (i, 0))]
    args = [scal, lpos_t, y1]
    aliases = {}
    if sorted_rows is not None:
        in_specs.append(any_spec)
        args.append(sorted_rows)
        aliases = {3: 0}
    return pl.pallas_call(
        functools.partial(_dispatch_kernel, local_rows=local_rows),
        grid=(n // rows,),
        in_specs=in_specs,
        out_specs=any_spec,
        out_shape=jax.ShapeDtypeStruct((n_sorted, D_MODEL), F32),
        scratch_shapes=[pltpu.VMEM((2, local_rows, D_MODEL), F32),
                        pltpu.SemaphoreType.DMA((2,)), pltpu.SemaphoreType.DMA((2,))],
        input_output_aliases=aliases,
        compiler_params=_params("arbitrary"),
        name="dispatch_rows",
    )(*args)


def _expert_kernel(te_ref, tr_ref, tb_ref, x_ref, wgu_ref, bgu_ref, wd_ref, bd_ref, o_ref,
                   wgu_b, wd_b):
    del tb_ref
    i = pl.program_id(0)
    e = te_ref[i]
    changed = (i == 0) | (e != te_ref[jnp.maximum(i - 1, 0)])

    @pl.when(changed)
    def _():
        wgu_b[...] = wgu_ref[0].astype(BF16)
        wd_b[...] = wd_ref[0].astype(BF16)

    @pl.when(tr_ref[i] > 0)
    def _():
        row = lax.broadcasted_iota(I32, x_ref.shape, 0)
        xb = jnp.where(row < tr_ref[i], x_ref[...], 0.0).astype(BF16)
        bgu = bgu_ref[0]
        acc = None
        n_col = 256
        for j in range(D_FF // n_col):
            cg = slice(j * n_col, (j + 1) * n_col)
            cl = slice(D_FF + j * n_col, D_FF + (j + 1) * n_col)
            hg = jnp.dot(xb, wgu_b[:, cg], preferred_element_type=F32) + bgu[:, cg]
            hl = jnp.dot(xb, wgu_b[:, cl], preferred_element_type=F32) + bgu[:, cl]
            glu = jnp.minimum(hg, SWIGLU_LIMIT)
            lin = jnp.clip(hl, -SWIGLU_LIMIT, SWIGLU_LIMIT)
            act = glu * jax.nn.sigmoid(SWIGLU_ALPHA * glu) * (lin + 1.0)
            part = jnp.dot(act.astype(BF16), wd_b[cg, :], preferred_element_type=F32)
            acc = part if acc is None else acc + part
        o_ref[...] = acc + bd_ref[0]


def _experts(tile_expert, tile_rows, tile_block, xs, w_gate_up, b_gate_up, w_down, b_down):
    n_tiles = xs.shape[0] // EXPERT_TILE
    grid_spec = pltpu.PrefetchScalarGridSpec(
        num_scalar_prefetch=3,
        grid=(n_tiles,),
        in_specs=[
            pl.BlockSpec((EXPERT_TILE, D_MODEL), lambda i, te, tr, tb: (tb[i], 0)),
            pl.BlockSpec((1, D_MODEL, 2 * D_FF), lambda i, te, tr, tb: (te[i], 0, 0)),
            pl.BlockSpec((1, 1, 2 * D_FF), lambda i, te, tr, tb: (te[i], 0, 0)),
            pl.BlockSpec((1, D_FF, D_MODEL), lambda i, te, tr, tb: (te[i], 0, 0)),
            pl.BlockSpec((1, 1, D_MODEL), lambda i, te, tr, tb: (te[i], 0, 0)),
        ],
        out_specs=pl.BlockSpec((EXPERT_TILE, D_MODEL), lambda i, te, tr, tb: (tb[i], 0)),
        scratch_shapes=[pltpu.VMEM((D_MODEL, 2 * D_FF), BF16), pltpu.VMEM((D_FF, D_MODEL), BF16)],
    )
    return pl.pallas_call(
        _expert_kernel,
        grid_spec=grid_spec,
        out_shape=jax.ShapeDtypeStruct(xs.shape, F32),
        compiler_params=_params("arbitrary"),
        name="expert_ffn",
    )(tile_expert, tile_rows, tile_block, xs, w_gate_up, b_gate_up, w_down, b_down)


def _combine_kernel(scal_ref, next_scal_ref, lpos_ref, gate_ref, y_ref, g_ref, b_ref, eo_ref,
                    out_ref, buf, big_sems, small_sems, *, local_rows):
    i = pl.program_id(0)
    slot = i % 2
    sems = [(big_sems.at[s], small_sems.at[s]) for s in (slot, 1 - slot)]

    @pl.when(i == 0)
    def _():
        buf[...] = jnp.zeros(buf.shape, buf.dtype)
        _start_runs(scal_ref, buf.at[0], eo_ref, (big_sems.at[0], small_sems.at[0]), to_sorted=False)

    @pl.when(i + 1 < pl.num_programs(0))
    def _():
        _start_runs(next_scal_ref, buf.at[1 - slot], eo_ref, sems[1], to_sorted=False)

    _wait_runs(scal_ref, SCAL_TOTAL, buf.at[slot], eo_ref, sems[0], to_sorted=False)
    rows = y_ref.shape[0]
    lpos = lpos_ref[...]
    gate = gate_ref[...]
    blk = LOCAL_BLOCK
    lane = lax.broadcasted_iota(I32, (rows, V7X_LANES), 1)
    slot_row = [jnp.broadcast_to(lpos[:, k:k + 1], (rows, V7X_LANES)) for k in range(TOP_K)]
    slot_gate = [jnp.broadcast_to(gate[:, k:k + 1], (rows, V7X_LANES)) for k in range(TOP_K)]
    moe = None
    for kb in range(local_rows // blk):
        pieces = []
        for jb in range(blk // V7X_LANES):
            col = lane + (kb * blk + jb * V7X_LANES)
            w = jnp.zeros((rows, V7X_LANES), F32)
            for k in range(TOP_K):
                w = jnp.where(col == slot_row[k], slot_gate[k], w)
            pieces.append(w)
        weights = jnp.concatenate(pieces, axis=1).astype(BF16)
        part = jnp.dot(weights, buf[slot, kb * blk:(kb + 1) * blk, :].astype(BF16),
                       preferred_element_type=F32)
        moe = part if moe is None else moe + part
    out_ref[...] = _layer_norm(DN_ALPHA * y_ref[...] + moe, g_ref[...], b_ref[...])


def _combine(scal, lpos, gates, y1, ln_g, ln_b, expert_out, *, rows):
    n = y1.shape[0]
    local_rows = _local_rows(rows)
    wide = pl.BlockSpec((rows, D_MODEL), lambda i: (i, 0))
    lanes = pl.BlockSpec((rows, V7X_LANES), lambda i: (i, 0))
    n_steps = n // rows
    scalars = pl.BlockSpec((1, 1, V7X_LANES), lambda i: (i, 0, 0), memory_space=pltpu.SMEM)
    next_scalars = pl.BlockSpec((1, 1, V7X_LANES), lambda i: (jnp.minimum(i + 1, n_steps - 1), 0, 0),
                                memory_space=pltpu.SMEM)
    return pl.pallas_call(
        functools.partial(_combine_kernel, local_rows=local_rows),
        grid=(n_steps,),
        in_specs=[scalars, next_scalars, lanes, lanes, wide, _resident(ln_g.shape),
                  _resident(ln_b.shape), pl.BlockSpec(memory_space=pl.ANY)],
        out_specs=wide,
        out_shape=jax.ShapeDtypeStruct((n, D_MODEL), F32),
        scratch_shapes=[pltpu.VMEM((2, local_rows, D_MODEL), F32),
                        pltpu.SemaphoreType.DMA((2,)), pltpu.SemaphoreType.DMA((2,))],
        compiler_params=_params("arbitrary"),
        name="combine_norm",
    )(scal, scal, lpos, gates, y1, ln_g, ln_b, expert_out)


def _tile_tables(lpos, n_chunk, sorted0, rows):
    n_tiles = lpos.shape[0] // rows
    local0 = (jnp.cumsum(n_chunk, axis=1) - n_chunk) * RUN_CHUNK
    n_big = n_chunk // BIG_CHUNKS
    total = jnp.stack([jnp.sum(n_big, axis=1), jnp.sum(n_chunk - n_big * BIG_CHUNKS, axis=1)], axis=1)
    prev_total = jnp.concatenate([jnp.zeros((1, 2), I32), total[:-1]], axis=0)
    pad = jnp.zeros((n_tiles, V7X_LANES - SCAL_PREV_TOTAL - 2), I32)
    scal = jnp.concatenate([sorted0, n_chunk, local0, total, prev_total, pad], axis=1)
    lpos_t = jnp.transpose(lpos[:, :8].reshape(n_tiles, rows, 8), (0, 2, 1))
    return scal.reshape(n_tiles, 1, V7X_LANES), lpos_t


def kernel(x_prompt, x_sample, mem_prompt, cache_attn_k, cache_attn_v, cache_conv, cache_mem_k,
           cache_mem_v, w_in, rel_bias, conv_w, w_mem_kv, w_gate, w_br_attn, w_br_conv, w_br_mem,
           w_out, ln1_g, ln1_b, w_router, b_router, w_gate_up, b_gate_up, w_down, b_down,
           ln2_g, ln2_b):
    depth = w_in.shape[0]
    assert depth == 1, "one layer only"
    batch, seq, _ = x_prompt.shape
    dec_batch, dec_seq, _ = x_sample.shape
    n_prompt = batch * seq
    n_sample = dec_batch * dec_seq
    assert seq % ROW_TILE == 0 and HIST == ROW_TILE and cache_attn_k.shape[2] == HIST
    assert dec_seq % 16 == 0 and dec_seq <= CHUNK and cache_conv.shape[2] == 2

    w_in_b = w_in[0].astype(BF16)
    w_mem_b = w_mem_kv[0].astype(BF16)
    wr = w_router[0]
    wr_hi = wr.astype(BF16)
    wr_lo = (wr - wr_hi.astype(F32)).astype(BF16)
    merge_w = (w_gate[0].astype(BF16), w_br_attn[0].astype(BF16), w_br_conv[0].astype(BF16),
               w_br_mem[0].astype(BF16), w_out[0].astype(BF16), ln1_g, ln1_b,
               jnp.concatenate([wr_hi, wr_lo], axis=1), wr_hi, b_router)
    conv_w8 = jnp.pad(conv_w[0], ((0, 5), (0, 0)))
    bias = _band_bias(rel_bias[0])

    xp = x_prompt.reshape(n_prompt, D_MODEL)
    mk_f, mv_f, mk_b, mv_b = _memory_kv(mem_prompt.reshape(batch * MEM_TOKENS, D_MODEL), w_mem_b)
    q, k_pad, v_pad, c, qm, k_tail, v_tail, u_tail = _proj_in(
        xp, w_in_b, conv_w8, jnp.zeros((batch, 8, WIDTH), F32),
        batch=batch, seq=seq, rows=ROW_TILE, pad_steps=1)
    a, m = _attention(
        q, k_pad, v_pad, bias.reshape(ATT_HEADS * CHUNK, WINDOW), qm,
        mk_b.reshape(batch, MEM_TOKENS, WIDTH),
        mv_b.reshape(batch, MEM_TOKENS, WIDTH), batch=batch, seq=seq, rows=ROW_TILE,
        chunk_rows=CHUNK, valid_keys=BAND, first_chunk=0, mem_rows=256)
    y1_p, sel_p, topi_p, topg_p = _merge(xp, a, c, m, merge_w, rows=MERGE_TILE)

    xs = x_sample.reshape(n_sample, D_MODEL)
    conv_init = jnp.pad(cache_conv[0], ((0, 0), (6, 0), (0, 0)))
    q_s, k_s, v_s, c_s, qm_s, k_new, v_new, u_tail_s = _proj_in(
        xs, w_in_b, conv_w8, conv_init, batch=dec_batch, seq=dec_seq, rows=dec_seq, pad_steps=0)
    pad_rows = WINDOW - HIST - dec_seq

    def window(cache, new):
        cache = cache[0].reshape(dec_batch, HIST, WIDTH).astype(BF16)
        return jnp.pad(jnp.concatenate([cache, new], axis=1), ((0, 0), (0, pad_rows), (0, 0)))

    a_s, m_s = _attention(
        q_s, window(cache_attn_k, k_s), window(cache_attn_v, v_s),
        bias[:, :dec_seq].reshape(ATT_HEADS * dec_seq, WINDOW), qm_s,
        cache_mem_k[0].reshape(dec_batch, MEM_TOKENS, WIDTH).astype(BF16),
        cache_mem_v[0].reshape(dec_batch, MEM_TOKENS, WIDTH).astype(BF16),
        batch=dec_batch, seq=dec_seq, rows=dec_seq, chunk_rows=dec_seq,
        valid_keys=HIST + dec_seq, first_chunk=LEFT_CHUNKS, mem_rows=dec_seq)
    y1_s, sel_s, topi_s, topg_s = _merge(xs, a_s, c_s, m_s, merge_w, rows=n_sample)

    lpos_p, before_p, cnt_p = _ranks(sel_p, topi_p, jnp.zeros((8, N_EXPERTS), F32), rows=MOE_TILE)
    lpos_s, before_s, cnt = _ranks(sel_s, topi_s, cnt_p, rows=n_sample)
    before = jnp.concatenate([before_p[:, 0], before_s[:, 0]], axis=0).astype(I32)
    n_run = jnp.concatenate([before[1:], cnt[:1].astype(I32)], axis=0) - before
    n_chunk = (n_run + RUN_CHUNK - 1) // RUN_CHUNK
    run_rows = n_chunk * RUN_CHUNK
    counts = jnp.sum(run_rows, axis=0)
    padded = (counts + EXPERT_TILE - 1) // EXPERT_TILE * EXPERT_TILE
    ends = jnp.cumsum(padded)
    starts = ends - padded
    sorted0 = starts[None, :] + jnp.cumsum(run_rows, axis=0) - run_rows
    n_token_tiles = n_prompt // MOE_TILE + 1
    n_tiles = ((n_prompt + n_sample) * TOP_K
               + n_token_tiles * N_EXPERTS * (RUN_CHUNK - 1)) // EXPERT_TILE + N_EXPERTS
    tile_start = jnp.arange(n_tiles, dtype=I32) * EXPERT_TILE
    tile_expert = jnp.minimum(
        jnp.sum((ends[None, :] <= tile_start[:, None]).astype(I32), axis=1), N_EXPERTS - 1)
    tile_rows = jnp.clip(starts[tile_expert] + counts[tile_expert] - tile_start, 0, EXPERT_TILE)
    last_used = jnp.maximum(ends[-1] // EXPERT_TILE - 1, 0)
    tile_block = jnp.minimum(jnp.arange(n_tiles, dtype=I32), last_used)
    tile_expert = jnp.where(jnp.arange(n_tiles) > last_used, tile_expert[last_used], tile_expert)
    scal_p, lpos_tp = _tile_tables(lpos_p, n_chunk[:-1], sorted0[:-1], MOE_TILE)
    scal_s, lpos_ts = _tile_tables(lpos_s, n_chunk[-1:], sorted0[-1:], n_sample)

    n_sorted = n_tiles * EXPERT_TILE
    sorted_rows = _dispatch(scal_p, lpos_tp, y1_p, None, rows=MOE_TILE, n_sorted=n_sorted)
    sorted_rows = _dispatch(scal_s, lpos_ts, y1_s, sorted_rows, rows=n_sample, n_sorted=n_sorted)
    expert_out = _experts(tile_expert, tile_rows, tile_block, sorted_rows, w_gate_up[0],
                          b_gate_up[0].reshape(N_EXPERTS, 1, 2 * D_FF), w_down[0],
                          b_down[0].reshape(N_EXPERTS, 1, D_MODEL))
    y_p = _combine(scal_p, lpos_p, topg_p, y1_p, ln2_g, ln2_b, expert_out, rows=MOE_TILE)
    y_s = _combine(scal_s, lpos_s, topg_s, y1_s, ln2_g, ln2_b, expert_out, rows=n_sample)

    def heads(t, n_b, n_rows, n_heads, dim):
        return t.reshape(1, n_b, n_rows, n_heads, dim)

    return (y_p.reshape(batch, seq, D_MODEL),
            y_s.reshape(dec_batch, dec_seq, D_MODEL),
            heads(k_tail, batch, HIST, ATT_HEADS, ATT_DIM),
            heads(v_tail, batch, HIST, ATT_HEADS, ATT_DIM),
            u_tail[:, 6:8].reshape(1, batch, 2, WIDTH),
            heads(mk_f, batch, MEM_TOKENS, MEM_HEADS, MEM_DIM),
            heads(mv_f, batch, MEM_TOKENS, MEM_HEADS, MEM_DIM),
            heads(k_new, dec_batch, dec_seq, ATT_HEADS, ATT_DIM),
            heads(v_new, dec_batch, dec_seq, ATT_HEADS, ATT_DIM),
            u_tail_s[:, 6:8].reshape(1, dec_batch, 2, WIDTH))
```

```python
import functools

import jax
import jax.numpy as jnp
from jax import lax
from jax.experimental import pallas as pl
from jax.experimental.pallas import tpu as pltpu

F32 = jnp.float32
BF16 = jnp.bfloat16
I32 = jnp.int32

D_MODEL = 1024
CHUNK = 64
LEFT_CHUNKS = 8
HIST = LEFT_CHUNKS * CHUNK
BAND = HIST + CHUNK
ATT_HEADS = 8
ATT_DIM = 64
REL_MAX = 128
WIDTH = 512
MEM_TOKENS = 256
MEM_HEADS = 4
MEM_DIM = 128
N_EXPERTS = 32
TOP_K = 4
D_FF = 1024
SWIGLU_LIMIT = 7.0
SWIGLU_ALPHA = 1.702
DN_ALPHA = 2.0 ** 0.25
LN_EPS = 1e-5
NEG_INF = -1e30

V7X_LANES = 128
V7X_VMEM_BYTES = 64 * 1024 * 1024
VMEM_LIMIT = V7X_VMEM_BYTES - 8 * 1024 * 1024

ROW_TILE = 512
MERGE_TILE = 1024
MOE_TILE = 512
WINDOW = BAND + CHUNK
EXPERT_TILE = 1024
RUN_CHUNK = 8
LOCAL_BLOCK = 256


def _params(*sem):
    return pltpu.CompilerParams(dimension_semantics=sem, vmem_limit_bytes=VMEM_LIMIT)


def _resident(shape):
    nd = len(shape)
    return pl.BlockSpec(shape, lambda *_: (0,) * nd, pipeline_mode=pl.Buffered(1))


def _layer_norm(r, g, b):
    mu = jnp.mean(r, axis=-1, keepdims=True)
    d = r - mu
    var = jnp.mean(d * d, axis=-1, keepdims=True)
    return d * lax.rsqrt(var + LN_EPS) * g + b


def _memkv_kernel(x_ref, w_ref, kf_ref, vf_ref, kb_ref, vb_ref):
    y = jnp.dot(x_ref[...].astype(BF16), w_ref[...], preferred_element_type=F32)
    k = y[:, :WIDTH]
    v = y[:, WIDTH:]
    kf_ref[...] = k
    vf_ref[...] = v
    kb_ref[...] = k.astype(BF16)
    vb_ref[...] = v.astype(BF16)


def _memory_kv(mem2d, w_b):
    rows = mem2d.shape[0]
    tile = pl.BlockSpec((ROW_TILE, D_MODEL), lambda i: (i, 0))
    half = pl.BlockSpec((ROW_TILE, WIDTH), lambda i: (i, 0))
    return pl.pallas_call(
        _memkv_kernel,
        grid=(rows // ROW_TILE,),
        in_specs=[tile, _resident((D_MODEL, 2 * WIDTH))],
        out_specs=[half, half, half, half],
        out_shape=[jax.ShapeDtypeStruct((rows, WIDTH), F32)] * 2
        + [jax.ShapeDtypeStruct((rows, WIDTH), BF16)] * 2,
        compiler_params=_params("arbitrary"),
        name="memory_kv",
    )(mem2d, w_b)


def _proj_in_kernel(x_ref, w_ref, cw_ref, cinit_ref,
                    q_ref, k_ref, v_ref, c_ref, qm_ref, kt_ref, vt_ref, ut_ref,
                    carry_ref, *, pad_steps):
    s = pl.program_id(1)
    ns = pl.num_programs(1)
    rows = x_ref.shape[0]
    first = pad_steps
    last = ns - 1 - pad_steps

    if pad_steps:
        @pl.when((s < first) | (s > last))
        def _():
            k_ref[...] = jnp.zeros(k_ref.shape, k_ref.dtype)
            v_ref[...] = jnp.zeros(v_ref.shape, v_ref.dtype)

    @pl.when((s >= first) & (s <= last))
    def _():
        @pl.when(s == first)
        def _():
            carry_ref[...] = cinit_ref[0]

        xb = x_ref[...].astype(BF16)

        def proj(g):
            return jnp.dot(xb, w_ref[:, g * WIDTH:(g + 1) * WIDTH], preferred_element_type=F32)

        q_ref[...] = (proj(0) * (ATT_DIM ** -0.5)).astype(BF16)
        k = proj(1)
        v = proj(2)
        k_ref[0] = k.astype(BF16)
        v_ref[0] = v.astype(BF16)

        @pl.when(s == last)
        def _():
            kt_ref[0] = k
            vt_ref[0] = v

        bg = proj(3)
        u = proj(4) * proj(5)
        prev = carry_ref[...]
        row = lax.broadcasted_iota(I32, u.shape, 0)
        u1 = jnp.where(row == 0, prev[7:8], pltpu.roll(u, 1, 0))
        u2 = jnp.where(row == 0, prev[6:7], jnp.where(row == 1, prev[7:8], pltpu.roll(u, 2, 0)))
        cw = cw_ref[...]
        c_ref[...] = (bg * (cw[0:1] * u2 + cw[1:2] * u1 + cw[2:3] * u)).astype(BF16)
        tail = u[rows - 8:]
        carry_ref[...] = tail

        @pl.when(s == last)
        def _():
            ut_ref[0] = tail

        qm_ref[...] = proj(6).astype(BF16)


def _proj_in(x2d, w_b, conv_w8, conv_init, *, batch, seq, rows, pad_steps):
    n_data = seq // rows
    n_steps = n_data + 2 * pad_steps

    def data_idx(b, s):
        return (b * n_data + jnp.clip(s - pad_steps, 0, n_data - 1), 0)

    wide = pl.BlockSpec((rows, D_MODEL), data_idx)
    narrow = pl.BlockSpec((rows, WIDTH), data_idx)
    seq_blk = pl.BlockSpec((1, rows, WIDTH), lambda b, s: (b, s, 0))
    tail_blk = pl.BlockSpec((1, rows, WIDTH), lambda b, s: (b, 0, 0))
    tail8 = pl.BlockSpec((1, 8, WIDTH), lambda b, s: (b, 0, 0))
    tok = jax.ShapeDtypeStruct((batch * seq, WIDTH), BF16)
    kv = jax.ShapeDtypeStruct((batch, n_steps * rows, WIDTH), BF16)
    tail = jax.ShapeDtypeStruct((batch, rows, WIDTH), F32)
    return pl.pallas_call(
        functools.partial(_proj_in_kernel, pad_steps=pad_steps),
        grid=(batch, n_steps),
        in_specs=[wide, _resident(w_b.shape), _resident(conv_w8.shape), tail8],
        out_specs=[narrow, seq_blk, seq_blk, narrow, narrow, tail_blk, tail_blk, tail8],
        out_shape=[tok, kv, kv, tok, tok, tail, tail,
                   jax.ShapeDtypeStruct((batch, 8, WIDTH), F32)],
        scratch_shapes=[pltpu.VMEM((8, WIDTH), F32)],
        compiler_params=_params("arbitrary", "arbitrary"),
        name="proj_in",
    )(x2d, w_b, conv_w8, conv_init)


def _bias_kernel(tab_ref, out_ref):
    shape = (CHUNK, WINDOW)
    i = lax.broadcasted_iota(I32, shape, 0)
    j = lax.broadcasted_iota(I32, shape, 1)
    idx = jnp.clip(HIST + i - j, -REL_MAX, REL_MAX) + REL_MAX
    for h in range(ATT_HEADS):
        def body(d, acc, h=h):
            return jnp.where(idx == d, tab_ref[h, d], acc)
        out_ref[h] = lax.fori_loop(0, 2 * REL_MAX + 1, body, jnp.zeros(shape, F32))


def _band_bias(rel_table):
    return pl.pallas_call(
        _bias_kernel,
        in_specs=[pl.BlockSpec(memory_space=pltpu.SMEM)],
        out_shape=jax.ShapeDtypeStruct((ATT_HEADS, CHUNK, WINDOW), F32),
        name="band_bias",
    )(rel_table)


def _attention_kernel(q_ref, k_ref, v_ref, bias_ref, qm_ref, mk_ref, mv_ref,
                      a_ref, m_ref, *, chunk_rows, n_chunks, valid_keys, first_chunk, mem_rows):
    s = pl.program_id(1)
    nr = chunk_rows
    pair_w = 2 * ATT_DIM
    col = lax.broadcasted_iota(I32, (ATT_HEADS * nr, WINDOW), 1)
    lane2 = lax.broadcasted_iota(I32, (2 * nr, pair_w), 1)
    row2 = lax.broadcasted_iota(I32, (2 * nr, pair_w), 0)
    own = (lane2 >= ATT_DIM) == (row2 >= nr)
    low_lanes = lax.broadcasted_iota(I32, (nr, pair_w), 1) < ATT_DIM
    nt = (((1,), (1,)), ((), ()))

    def chunk(c, carry):
        g = first_chunk + s * n_chunks + c
        r0 = pl.multiple_of(c * nr, nr)
        w0 = pl.multiple_of((s * n_chunks + c) * CHUNK, CHUNK)
        qc = q_ref[pl.ds(r0, nr), :].astype(F32)
        kw = k_ref[0, pl.ds(w0, WINDOW), :]
        vw = v_ref[0, pl.ds(w0, WINDOW), :]
        scores = []
        for pair in range(ATT_HEADS // 2):
            lanes = slice(pair * pair_w, (pair + 1) * pair_w)
            qp = qc[:, lanes]
            q2 = jnp.where(own, jnp.concatenate([qp, qp], axis=0), 0.0).astype(BF16)
            scores.append(lax.dot_general(q2, kw[:, lanes], nt, preferred_element_type=F32))
        sc = jnp.concatenate(scores, axis=0) + bias_ref[...]
        sc = jnp.where((col >= HIST - CHUNK * g) & (col < valid_keys), sc, NEG_INF)
        m = jnp.max(sc, axis=-1, keepdims=True)
        p = jnp.exp(sc - m)
        inv = 1.0 / jnp.sum(p, axis=-1, keepdims=True)
        pb = p.astype(BF16)
        outs = []
        for pair in range(ATT_HEADS // 2):
            lanes = slice(pair * pair_w, (pair + 1) * pair_w)
            rows = slice(pair * 2 * nr, (pair + 1) * 2 * nr)
            o = jnp.dot(pb[rows], vw[:, lanes], preferred_element_type=F32) * inv[rows]
            outs.append(jnp.where(low_lanes, o[:nr], o[nr:]))
        a_ref[pl.ds(r0, nr), :] = jnp.concatenate(outs, axis=1).astype(BF16)
        return carry

    lax.fori_loop(0, n_chunks, chunk, 0, unroll=min(n_chunks, 4))

    rows = qm_ref.shape[0]
    for rb in range(rows // mem_rows):
        rs = slice(rb * mem_rows, (rb + 1) * mem_rows)
        scores = []
        for h in range(MEM_HEADS):
            lanes = slice(h * MEM_DIM, (h + 1) * MEM_DIM)
            scores.append(lax.dot_general(qm_ref[rs, lanes], mk_ref[0, :, lanes], nt,
                                          preferred_element_type=F32))
        sc = jnp.concatenate(scores, axis=0) * (MEM_DIM ** -0.5)
        m = jnp.max(sc, axis=-1, keepdims=True)
        p = jnp.exp(sc - m)
        inv = 1.0 / jnp.sum(p, axis=-1, keepdims=True)
        pb = p.astype(BF16)
        for h in range(MEM_HEADS):
            lanes = slice(h * MEM_DIM, (h + 1) * MEM_DIM)
            hs = slice(h * mem_rows, (h + 1) * mem_rows)
            o = jnp.dot(pb[hs], mv_ref[0, :, lanes], preferred_element_type=F32) * inv[hs]
            m_ref[rs, lanes] = o.astype(BF16)


def _attention(q, k_pad, v_pad, bias, qm, mk, mv, *, batch, seq, rows, chunk_rows,
               valid_keys, first_chunk, mem_rows):
    n_steps = seq // rows
    tok = pl.BlockSpec((rows, WIDTH), lambda b, s: (b * n_steps + s, 0))
    whole_seq = pl.BlockSpec((1, k_pad.shape[1], WIDTH), lambda b, s: (b, 0, 0))
    mem = pl.BlockSpec((1, MEM_TOKENS, WIDTH), lambda b, s: (b, 0, 0))
    kern = functools.partial(
        _attention_kernel, chunk_rows=chunk_rows, n_chunks=rows // chunk_rows,
        valid_keys=valid_keys, first_chunk=first_chunk, mem_rows=mem_rows)
    out = jax.ShapeDtypeStruct((batch * seq, WIDTH), BF16)
    return pl.pallas_call(
        kern,
        grid=(batch, n_steps),
        in_specs=[tok, whole_seq, whole_seq, _resident(bias.shape), tok, mem, mem],
        out_specs=[tok, tok],
        out_shape=[out, out],
        compiler_params=_params("arbitrary", "arbitrary"),
        name="attention",
    )(q, k_pad, v_pad, bias, qm, mk, mv)


def _merge_kernel(x_ref, a_ref, c_ref, m_ref, wg_ref, wa_ref, wc_ref, wm_ref, wo_ref,
                  g_ref, b_ref, wr2_ref, wrh_ref, br_ref,
                  y_ref, sel_ref, topi_ref, topg_ref):
    x = x_ref[...]
    xb = x.astype(BF16)
    branches = [(a_ref[...], wa_ref), (c_ref[...], wc_ref), (m_ref[...], wm_ref)]
    out = None
    n_col = 256
    for j in range(D_MODEL // n_col):
        cols = slice(j * n_col, (j + 1) * n_col)
        comb = None
        for i, (br, wbr) in enumerate(branches):
            gcols = slice(i * D_MODEL + j * n_col, i * D_MODEL + (j + 1) * n_col)
            gate = jax.nn.sigmoid(jnp.dot(xb, wg_ref[:, gcols], preferred_element_type=F32))
            term = gate * jnp.dot(br, wbr[:, cols], preferred_element_type=F32)
            comb = term if comb is None else comb + term
        part = jnp.dot(comb.astype(BF16), wo_ref[cols, :], preferred_element_type=F32)
        out = part if out is None else out + part
    y = _layer_norm(DN_ALPHA * x + out, g_ref[...], b_ref[...])
    y_ref[...] = y

    y_hi = y.astype(BF16)
    y_lo = (y - y_hi.astype(F32)).astype(BF16)
    both = jnp.dot(y_hi, wr2_ref[...], preferred_element_type=F32)
    logits = (both[:, :N_EXPERTS] + both[:, N_EXPERTS:]
              + jnp.dot(y_lo, wrh_ref[...], preferred_element_type=F32) + br_ref[...])

    rows = logits.shape[0]
    lane = lax.broadcasted_iota(I32, (rows, N_EXPERTS), 1).astype(F32)
    lane_out = lax.broadcasted_iota(I32, (rows, V7X_LANES), 1)
    work = logits
    sel = jnp.zeros((rows, N_EXPERTS), F32)
    topi = jnp.zeros((rows, V7X_LANES), I32)
    vals = []
    for k in range(TOP_K):
        top = jnp.max(work, axis=-1, keepdims=True)
        idx = jnp.min(jnp.where(work == top, lane, float(N_EXPERTS)), axis=-1, keepdims=True)
        hit = lane == idx
        sel = jnp.where(hit, 1.0, sel)
        work = jnp.where(hit, -jnp.inf, work)
        topi = jnp.where(lane_out == k, idx.astype(I32), topi)
        vals.append(top)
    exps = [jnp.exp(v - vals[0]) for v in vals]
    denom = exps[0] + exps[1] + exps[2] + exps[3]
    topg = jnp.zeros((rows, V7X_LANES), F32)
    for k in range(TOP_K):
        topg = jnp.where(lane_out == k, exps[k] / denom, topg)
    sel_ref[...] = sel.astype(BF16)
    topi_ref[...] = topi
    topg_ref[...] = topg


def _merge(x2d, a, c, m, weights, *, rows):
    n = x2d.shape[0]
    wide = pl.BlockSpec((rows, D_MODEL), lambda i: (i, 0))
    narrow = pl.BlockSpec((rows, WIDTH), lambda i: (i, 0))
    lanes = pl.BlockSpec((rows, V7X_LANES), lambda i: (i, 0))
    return pl.pallas_call(
        _merge_kernel,
        grid=(n // rows,),
        in_specs=[wide, narrow, narrow, narrow] + [_resident(w.shape) for w in weights],
        out_specs=[wide, pl.BlockSpec((rows, N_EXPERTS), lambda i: (i, 0)), lanes, lanes],
        out_shape=[jax.ShapeDtypeStruct((n, D_MODEL), F32),
                   jax.ShapeDtypeStruct((n, N_EXPERTS), BF16),
                   jax.ShapeDtypeStruct((n, V7X_LANES), I32),
                   jax.ShapeDtypeStruct((n, V7X_LANES), F32)],
        compiler_params=_params("arbitrary"),
        name="merge_router",
    )(x2d, a, c, m, *weights)


def _rank_kernel(sel_ref, topi_ref, init_ref, lpos_ref, before_ref, cnt_ref, carry_ref):
    @pl.when(pl.program_id(0) == 0)
    def _():
        carry_ref[...] = init_ref[...]

    before_ref[0] = carry_ref[...]
    sel = sel_ref[...]
    rows = sel.shape[0]
    r = lax.broadcasted_iota(I32, (rows, rows), 0)
    c = lax.broadcasted_iota(I32, (rows, rows), 1)
    earlier = jnp.where(c < r, 1.0, 0.0).astype(BF16)
    in_tile = jnp.dot(earlier, sel, preferred_element_type=F32)
    n_run = jnp.sum(sel.astype(F32), axis=0, keepdims=True)
    run_rows = jnp.ceil(n_run * (1.0 / RUN_CHUNK)) * RUN_CHUNK
    er = lax.broadcasted_iota(I32, (N_EXPERTS, N_EXPERTS), 0)
    ec = lax.broadcasted_iota(I32, (N_EXPERTS, N_EXPERTS), 1)
    lower_experts = jnp.where(er < ec, 1.0, 0.0).astype(BF16)
    local0 = jnp.dot(jnp.broadcast_to(run_rows, (8, N_EXPERTS)).astype(BF16), lower_experts,
                     preferred_element_type=F32)[0:1]
    pos = in_tile + local0
    lane = lax.broadcasted_iota(I32, (rows, N_EXPERTS), 1)
    lane_out = lax.broadcasted_iota(I32, (rows, V7X_LANES), 1)
    topi = topi_ref[...]
    lpos = jnp.full((rows, V7X_LANES), -1, I32)
    for k in range(TOP_K):
        mine = jnp.sum(jnp.where(lane == topi[:, k:k + 1], pos, 0.0), axis=-1, keepdims=True)
        lpos = jnp.where(lane_out == k, mine.astype(I32), lpos)
    lpos_ref[...] = lpos
    total = carry_ref[...] + n_run
    carry_ref[...] = total
    cnt_ref[...] = total


def _ranks(sel, topi, init, *, rows):
    n = sel.shape[0]
    small = pl.BlockSpec((8, N_EXPERTS), lambda i: (0, 0))
    lanes = pl.BlockSpec((rows, V7X_LANES), lambda i: (i, 0))
    return pl.pallas_call(
        _rank_kernel,
        grid=(n // rows,),
        in_specs=[pl.BlockSpec((rows, N_EXPERTS), lambda i: (i, 0)), lanes, small],
        out_specs=[lanes, pl.BlockSpec((1, 8, N_EXPERTS), lambda i: (i, 0, 0)), small],
        out_shape=[jax.ShapeDtypeStruct((n, V7X_LANES), I32),
                   jax.ShapeDtypeStruct((n // rows, 8, N_EXPERTS), F32),
                   jax.ShapeDtypeStruct((8, N_EXPERTS), F32)],
        scratch_shapes=[pltpu.VMEM((8, N_EXPERTS), F32)],
        compiler_params=_params("arbitrary"),
        name="expert_ranks",
    )(sel, topi, init)


SCAL_SORTED0 = 0
SCAL_CHUNKS = N_EXPERTS
SCAL_LOCAL0 = 2 * N_EXPERTS
SCAL_TOTAL = 3 * N_EXPERTS
SCAL_PREV_TOTAL = SCAL_TOTAL + 2
BIG_CHUNKS = 8
COPY_ROWS = (BIG_CHUNKS * RUN_CHUNK, RUN_CHUNK)


def _run_copy(local_ref, local_row, sorted_ref, sorted_row, sems, size, to_sorted):
    loc = local_ref.at[pl.ds(local_row, COPY_ROWS[size])]
    srt = sorted_ref.at[pl.ds(sorted_row, COPY_ROWS[size])]
    sem = sems[size]
    return pltpu.make_async_copy(loc, srt, sem) if to_sorted else pltpu.make_async_copy(srt, loc, sem)


def _start_runs(scal_ref, local_ref, sorted_ref, sems, *, to_sorted):
    def expert(e, carry):
        sorted0 = scal_ref[0, 0, SCAL_SORTED0 + e]
        local0 = scal_ref[0, 0, SCAL_LOCAL0 + e]
        n_chunk = scal_ref[0, 0, SCAL_CHUNKS + e]
        n_big = n_chunk // BIG_CHUNKS
        rest0 = n_big * COPY_ROWS[0]

        def copy(size, offset):
            return _run_copy(local_ref, pl.multiple_of(local0 + offset, RUN_CHUNK), sorted_ref,
                             pl.multiple_of(sorted0 + offset, RUN_CHUNK), sems, size, to_sorted)

        def big(c, inner):
            copy(0, c * COPY_ROWS[0]).start()
            return inner

        def small(c, inner):
            copy(1, rest0 + c * COPY_ROWS[1]).start()
            return inner

        lax.fori_loop(0, n_big, big, 0)
        lax.fori_loop(0, n_chunk - n_big * BIG_CHUNKS, small, 0)
        return carry

    lax.fori_loop(0, N_EXPERTS, expert, 0)


def _wait_runs(scal_ref, total_lane, local_ref, sorted_ref, sems, *, to_sorted):
    for size in range(len(COPY_ROWS)):
        def drain(c, carry, size=size):
            _run_copy(local_ref, 0, sorted_ref, 0, sems, size, to_sorted).wait()
            return carry

        lax.fori_loop(0, scal_ref[0, 0, total_lane + size], drain, 0)


def _dispatch_kernel(scal_ref, lpos_ref, y_ref, *rest, local_rows):
    sorted_ref, stage, big_sems, small_sems = rest[-4:]
    i = pl.program_id(0)
    slot = i % 2
    sems = [(big_sems.at[s], small_sems.at[s]) for s in (slot, 1 - slot)]
    rows = y_ref.shape[0]
    yb = y_ref[...].astype(BF16)
    lpos = lpos_ref[0]
    blk = LOCAL_BLOCK
    for rb in range(local_rows // blk):
        j = lax.broadcasted_iota(I32, (blk, rows), 0) + rb * blk
        onehot = jnp.zeros((blk, rows), F32)
        for k in range(TOP_K):
            onehot = jnp.where(j == lpos[k:k + 1], 1.0, onehot)
        onehot = onehot.astype(BF16)
        stage[slot, rb * blk:(rb + 1) * blk, :] = jnp.dot(onehot, yb, preferred_element_type=F32)
    _start_runs(scal_ref, stage.at[slot], sorted_ref, sems[0], to_sorted=True)

    @pl.when(i > 0)
    def _():
        _wait_runs(scal_ref, SCAL_PREV_TOTAL, stage.at[1 - slot], sorted_ref, sems[1],
                   to_sorted=True)

    @pl.when(i == pl.num_programs(0) - 1)
    def _():
        _wait_runs(scal_ref, SCAL_TOTAL, stage.at[slot], sorted_ref, sems[0], to_sorted=True)


def _local_rows(rows):
    worst = rows * TOP_K + N_EXPERTS * (RUN_CHUNK - 1)
    return (worst + LOCAL_BLOCK - 1) // LOCAL_BLOCK * LOCAL_BLOCK


def _dispatch(scal, lpos_t, y1, sorted_rows, *, rows, n_sorted):
    n = y1.shape[0]
    local_rows = _local_rows(rows)
    any_spec = pl.BlockSpec(memory_space=pl.ANY)
    in_specs = [pl.BlockSpec((1, 1, V7X_LANES), lambda i: (i, 0, 0), memory_space=pltpu.SMEM),
                pl.BlockSpec((1, 8, rows), lambda i: (i, 0, 0)),
                pl.BlockSpec((rows, D_MODEL), lambda i: (i, 0))]
    args = [scal, lpos_t, y1]
    aliases = {}
    if sorted_rows is not None:
        in_specs.append(any_spec)
        args.append(sorted_rows)
        aliases = {3: 0}
    return pl.pallas_call(
        functools.partial(_dispatch_kernel, local_rows=local_rows),
        grid=(n // rows,),
        in_specs=in_specs,
        out_specs=any_spec,
        out_shape=jax.ShapeDtypeStruct((n_sorted, D_MODEL), F32),
        scratch_shapes=[pltpu.VMEM((2, local_rows, D_MODEL), F32),
                        pltpu.SemaphoreType.DMA((2,)), pltpu.SemaphoreType.DMA((2,))],
        input_output_aliases=aliases,
        compiler_params=_params("arbitrary"),
        name="dispatch_rows",
    )(*args)


def _expert_kernel(te_ref, tr_ref, tb_ref, x_ref, wgu_ref, bgu_ref, wd_ref, bd_ref, o_ref,
                   wgu_b, wd_b):
    del tb_ref
    i = pl.program_id(0)
    e = te_ref[i]
    changed = (i == 0) | (e != te_ref[jnp.maximum(i - 1, 0)])

    @pl.when(changed)
    def _():
        wgu_b[...] = wgu_ref[0].astype(BF16)
        wd_b[...] = wd_ref[0].astype(BF16)

    def ffn(n_rows):
        row = lax.broadcasted_iota(I32, (n_rows, D_MODEL), 0)
        xb = jnp.where(row < tr_ref[i], x_ref[0:n_rows, :], 0.0).astype(BF16)
        bgu = bgu_ref[0]
        acc = None
        n_col = 256
        for j in range(D_FF // n_col):
            cg = slice(j * n_col, (j + 1) * n_col)
            cl = slice(D_FF + j * n_col, D_FF + (j + 1) * n_col)
            hg = jnp.dot(xb, wgu_b[:, cg], preferred_element_type=F32) + bgu[:, cg]
            hl = jnp.dot(xb, wgu_b[:, cl], preferred_element_type=F32) + bgu[:, cl]
            glu = jnp.minimum(hg, SWIGLU_LIMIT)
            lin = jnp.clip(hl, -SWIGLU_LIMIT, SWIGLU_LIMIT)
            act = glu * jax.nn.sigmoid(SWIGLU_ALPHA * glu) * (lin + 1.0)
            part = jnp.dot(act.astype(BF16), wd_b[cg, :], preferred_element_type=F32)
            acc = part if acc is None else acc + part
        o_ref[0:n_rows, :] = acc + bd_ref[0]

    half = EXPERT_TILE // 2

    @pl.when(tr_ref[i] > half)
    def _():
        ffn(EXPERT_TILE)

    @pl.when((tr_ref[i] > 0) & (tr_ref[i] <= half))
    def _():
        ffn(half)


def _experts(tile_expert, tile_rows, tile_block, xs, w_gate_up, b_gate_up, w_down, b_down):
    n_tiles = xs.shape[0] // EXPERT_TILE
    grid_spec = pltpu.PrefetchScalarGridSpec(
        num_scalar_prefetch=3,
        grid=(n_tiles,),
        in_specs=[
            pl.BlockSpec((EXPERT_TILE, D_MODEL), lambda i, te, tr, tb: (tb[i], 0)),
            pl.BlockSpec((1, D_MODEL, 2 * D_FF), lambda i, te, tr, tb: (te[i], 0, 0)),
            pl.BlockSpec((1, 1, 2 * D_FF), lambda i, te, tr, tb: (te[i], 0, 0)),
            pl.BlockSpec((1, D_FF, D_MODEL), lambda i, te, tr, tb: (te[i], 0, 0)),
            pl.BlockSpec((1, 1, D_MODEL), lambda i, te, tr, tb: (te[i], 0, 0)),
        ],
        out_specs=pl.BlockSpec((EXPERT_TILE, D_MODEL), lambda i, te, tr, tb: (tb[i], 0)),
        scratch_shapes=[pltpu.VMEM((D_MODEL, 2 * D_FF), BF16), pltpu.VMEM((D_FF, D_MODEL), BF16)],
    )
    return pl.pallas_call(
        _expert_kernel,
        grid_spec=grid_spec,
        out_shape=jax.ShapeDtypeStruct(xs.shape, F32),
        compiler_params=_params("arbitrary"),
        name="expert_ffn",
    )(tile_expert, tile_rows, tile_block, xs, w_gate_up, b_gate_up, w_down, b_down)


def _combine_kernel(scal_ref, next_scal_ref, lpos_ref, gate_ref, y_ref, g_ref, b_ref, eo_ref,
                    out_ref, buf, big_sems, small_sems, *, local_rows):
    i = pl.program_id(0)
    slot = i % 2
    sems = [(big_sems.at[s], small_sems.at[s]) for s in (slot, 1 - slot)]

    @pl.when(i == 0)
    def _():
        buf[...] = jnp.zeros(buf.shape, buf.dtype)
        _start_runs(scal_ref, buf.at[0], eo_ref, (big_sems.at[0], small_sems.at[0]), to_sorted=False)

    @pl.when(i + 1 < pl.num_programs(0))
    def _():
        _start_runs(next_scal_ref, buf.at[1 - slot], eo_ref, sems[1], to_sorted=False)

    _wait_runs(scal_ref, SCAL_TOTAL, buf.at[slot], eo_ref, sems[0], to_sorted=False)
    rows = y_ref.shape[0]
    lpos = lpos_ref[...]
    gate = gate_ref[...]
    blk = LOCAL_BLOCK
    lane = lax.broadcasted_iota(I32, (rows, V7X_LANES), 1)
    slot_row = [jnp.broadcast_to(lpos[:, k:k + 1], (rows, V7X_LANES)) for k in range(TOP_K)]
    slot_gate = [jnp.broadcast_to(gate[:, k:k + 1], (rows, V7X_LANES)) for k in range(TOP_K)]
    moe = None
    for kb in range(local_rows // blk):
        pieces = []
        for jb in range(blk // V7X_LANES):
            col = lane + (kb * blk + jb * V7X_LANES)
            w = jnp.zeros((rows, V7X_LANES), F32)
            for k in range(TOP_K):
                w = jnp.where(col == slot_row[k], slot_gate[k], w)
            pieces.append(w)
        weights = jnp.concatenate(pieces, axis=1).astype(BF16)
        part = jnp.dot(weights, buf[slot, kb * blk:(kb + 1) * blk, :].astype(BF16),
                       preferred_element_type=F32)
        moe = part if moe is None else moe + part
    out_ref[...] = _layer_norm(DN_ALPHA * y_ref[...] + moe, g_ref[...], b_ref[...])


def _combine(scal, lpos, gates, y1, ln_g, ln_b, expert_out, *, rows):
    n = y1.shape[0]
    local_rows = _local_rows(rows)
    wide = pl.BlockSpec((rows, D_MODEL), lambda i: (i, 0))
    lanes = pl.BlockSpec((rows, V7X_LANES), lambda i: (i, 0))
    n_steps = n // rows
    scalars = pl.BlockSpec((1, 1, V7X_LANES), lambda i: (i, 0, 0), memory_space=pltpu.SMEM)
    next_scalars = pl.BlockSpec((1, 1, V7X_LANES), lambda i: (jnp.minimum(i + 1, n_steps - 1), 0, 0),
                                memory_space=pltpu.SMEM)
    return pl.pallas_call(
        functools.partial(_combine_kernel, local_rows=local_rows),
        grid=(n_steps,),
        in_specs=[scalars, next_scalars, lanes, lanes, wide, _resident(ln_g.shape),
                  _resident(ln_b.shape), pl.BlockSpec(memory_space=pl.ANY)],
        out_specs=wide,
        out_shape=jax.ShapeDtypeStruct((n, D_MODEL), F32),
        scratch_shapes=[pltpu.VMEM((2, local_rows, D_MODEL), F32),
                        pltpu.SemaphoreType.DMA((2,)), pltpu.SemaphoreType.DMA((2,))],
        compiler_params=_params("arbitrary"),
        name="combine_norm",
    )(scal, scal, lpos, gates, y1, ln_g, ln_b, expert_out)


def _tile_tables(lpos, n_chunk, sorted0, rows):
    n_tiles = lpos.shape[0] // rows
    local0 = (jnp.cumsum(n_chunk, axis=1) - n_chunk) * RUN_CHUNK
    n_big = n_chunk // BIG_CHUNKS
    total = jnp.stack([jnp.sum(n_big, axis=1), jnp.sum(n_chunk - n_big * BIG_CHUNKS, axis=1)], axis=1)
    prev_total = jnp.concatenate([jnp.zeros((1, 2), I32), total[:-1]], axis=0)
    pad = jnp.zeros((n_tiles, V7X_LANES - SCAL_PREV_TOTAL - 2), I32)
    scal = jnp.concatenate([sorted0, n_chunk, local0, total, prev_total, pad], axis=1)
    lpos_t = jnp.transpose(lpos[:, :8].reshape(n_tiles, rows, 8), (0, 2, 1))
    return scal.reshape(n_tiles, 1, V7X_LANES), lpos_t


def kernel(x_prompt, x_sample, mem_prompt, cache_attn_k, cache_attn_v, cache_conv, cache_mem_k,
           cache_mem_v, w_in, rel_bias, conv_w, w_mem_kv, w_gate, w_br_attn, w_br_conv, w_br_mem,
           w_out, ln1_g, ln1_b, w_router, b_router, w_gate_up, b_gate_up, w_down, b_down,
           ln2_g, ln2_b):
    depth = w_in.shape[0]
    assert depth == 1, "one layer only"
    batch, seq, _ = x_prompt.shape
    dec_batch, dec_seq, _ = x_sample.shape
    n_prompt = batch * seq
    n_sample = dec_batch * dec_seq
    assert seq % ROW_TILE == 0 and HIST == ROW_TILE and cache_attn_k.shape[2] == HIST
    assert dec_seq % 16 == 0 and dec_seq <= CHUNK and cache_conv.shape[2] == 2

    w_in_b = w_in[0].astype(BF16)
    w_mem_b = w_mem_kv[0].astype(BF16)
    wr = w_router[0]
    wr_hi = wr.astype(BF16)
    wr_lo = (wr - wr_hi.astype(F32)).astype(BF16)
    merge_w = (w_gate[0].astype(BF16), w_br_attn[0].astype(BF16), w_br_conv[0].astype(BF16),
               w_br_mem[0].astype(BF16), w_out[0].astype(BF16), ln1_g, ln1_b,
               jnp.concatenate([wr_hi, wr_lo], axis=1), wr_hi, b_router)
    conv_w8 = jnp.pad(conv_w[0], ((0, 5), (0, 0)))
    bias = _band_bias(rel_bias[0])

    xp = x_prompt.reshape(n_prompt, D_MODEL)
    mk_f, mv_f, mk_b, mv_b = _memory_kv(mem_prompt.reshape(batch * MEM_TOKENS, D_MODEL), w_mem_b)
    q, k_pad, v_pad, c, qm, k_tail, v_tail, u_tail = _proj_in(
        xp, w_in_b, conv_w8, jnp.zeros((batch, 8, WIDTH), F32),
        batch=batch, seq=seq, rows=ROW_TILE, pad_steps=1)
    a, m = _attention(
        q, k_pad, v_pad, bias.reshape(ATT_HEADS * CHUNK, WINDOW), qm,
        mk_b.reshape(batch, MEM_TOKENS, WIDTH),
        mv_b.reshape(batch, MEM_TOKENS, WIDTH), batch=batch, seq=seq, rows=ROW_TILE,
        chunk_rows=CHUNK, valid_keys=BAND, first_chunk=0, mem_rows=256)
    y1_p, sel_p, topi_p, topg_p = _merge(xp, a, c, m, merge_w, rows=MERGE_TILE)

    xs = x_sample.reshape(n_sample, D_MODEL)
    conv_init = jnp.pad(cache_conv[0], ((0, 0), (6, 0), (0, 0)))
    q_s, k_s, v_s, c_s, qm_s, k_new, v_new, u_tail_s = _proj_in(
        xs, w_in_b, conv_w8, conv_init, batch=dec_batch, seq=dec_seq, rows=dec_seq, pad_steps=0)
    pad_rows = WINDOW - HIST - dec_seq

    def window(cache, new):
        cache = cache[0].reshape(dec_batch, HIST, WIDTH).astype(BF16)
        return jnp.pad(jnp.concatenate([cache, new], axis=1), ((0, 0), (0, pad_rows), (0, 0)))

    a_s, m_s = _attention(
        q_s, window(cache_attn_k, k_s), window(cache_attn_v, v_s),
        bias[:, :dec_seq].reshape(ATT_HEADS * dec_seq, WINDOW), qm_s,
        cache_mem_k[0].reshape(dec_batch, MEM_TOKENS, WIDTH).astype(BF16),
        cache_mem_v[0].reshape(dec_batch, MEM_TOKENS, WIDTH).astype(BF16),
        batch=dec_batch, seq=dec_seq, rows=dec_seq, chunk_rows=dec_seq,
        valid_keys=HIST + dec_seq, first_chunk=LEFT_CHUNKS, mem_rows=dec_seq)
    y1_s, sel_s, topi_s, topg_s = _merge(xs, a_s, c_s, m_s, merge_w, rows=n_sample)

    lpos_p, before_p, cnt_p = _ranks(sel_p, topi_p, jnp.zeros((8, N_EXPERTS), F32), rows=MOE_TILE)
    lpos_s, before_s, cnt = _ranks(sel_s, topi_s, cnt_p, rows=n_sample)
    before = jnp.concatenate([before_p[:, 0], before_s[:, 0]], axis=0).astype(I32)
    n_run = jnp.concatenate([before[1:], cnt[:1].astype(I32)], axis=0) - before
    n_chunk = (n_run + RUN_CHUNK - 1) // RUN_CHUNK
    run_rows = n_chunk * RUN_CHUNK
    counts = jnp.sum(run_rows, axis=0)
    padded = (counts + EXPERT_TILE - 1) // EXPERT_TILE * EXPERT_TILE
    ends = jnp.cumsum(padded)
    starts = ends - padded
    sorted0 = starts[None, :] + jnp.cumsum(run_rows, axis=0) - run_rows
    n_token_tiles = n_prompt // MOE_TILE + 1
    n_tiles = ((n_prompt + n_sample) * TOP_K
               + n_token_tiles * N_EXPERTS * (RUN_CHUNK - 1)) // EXPERT_TILE + N_EXPERTS
    tile_start = jnp.arange(n_tiles, dtype=I32) * EXPERT_TILE
    tile_expert = jnp.minimum(
        jnp.sum((ends[None, :] <= tile_start[:, None]).astype(I32), axis=1), N_EXPERTS - 1)
    tile_rows = jnp.clip(starts[tile_expert] + counts[tile_expert] - tile_start, 0, EXPERT_TILE)
    last_used = jnp.maximum(ends[-1] // EXPERT_TILE - 1, 0)
    tile_block = jnp.minimum(jnp.arange(n_tiles, dtype=I32), last_used)
    tile_expert = jnp.where(jnp.arange(n_tiles) > last_used, tile_expert[last_used], tile_expert)
    scal_p, lpos_tp = _tile_tables(lpos_p, n_chunk[:-1], sorted0[:-1], MOE_TILE)
    scal_s, lpos_ts = _tile_tables(lpos_s, n_chunk[-1:], sorted0[-1:], n_sample)

    n_sorted = n_tiles * EXPERT_TILE
    sorted_rows = _dispatch(scal_p, lpos_tp, y1_p, None, rows=MOE_TILE, n_sorted=n_sorted)
    sorted_rows = _dispatch(scal_s, lpos_ts, y1_s, sorted_rows, rows=n_sample, n_sorted=n_sorted)
    expert_out = _experts(tile_expert, tile_rows, tile_block, sorted_rows, w_gate_up[0],
                          b_gate_up[0].reshape(N_EXPERTS, 1, 2 * D_FF), w_down[0],
                          b_down[0].reshape(N_EXPERTS, 1, D_MODEL))
    y_p = _combine(scal_p, lpos_p, topg_p, y1_p, ln2_g, ln2_b, expert_out, rows=MOE_TILE)
    y_s = _combine(scal_s, lpos_s, topg_s, y1_s, ln2_g, ln2_b, expert_out, rows=n_sample)

    def heads(t, n_b, n_rows, n_heads, dim):
        return t.reshape(1, n_b, n_rows, n_heads, dim)

    return (y_p.reshape(batch, seq, D_MODEL),
            y_s.reshape(dec_batch, dec_seq, D_MODEL),
            heads(k_tail, batch, HIST, ATT_HEADS, ATT_DIM),
            heads(v_tail, batch, HIST, ATT_HEADS, ATT_DIM),
            u_tail[:, 6:8].reshape(1, batch, 2, WIDTH),
            heads(mk_f, batch, MEM_TOKENS, MEM_HEADS, MEM_DIM),
            heads(mv_f, batch, MEM_TOKENS, MEM_HEADS, MEM_DIM),
            heads(k_new, dec_batch, dec_seq, ATT_HEADS, ATT_DIM),
            heads(v_new, dec_batch, dec_seq, ATT_HEADS, ATT_DIM),
            u_tail_s[:, 6:8].reshape(1, dec_batch, 2, WIDTH))
```

```python
import functools

import jax
import jax.numpy as jnp
from jax import lax
from jax.experimental import pallas as pl
from jax.experimental.pallas import tpu as pltpu

F32 = jnp.float32
BF16 = jnp.bfloat16
I32 = jnp.int32

D_MODEL = 1024
CHUNK = 64
LEFT_CHUNKS = 8
HIST = LEFT_CHUNKS * CHUNK
BAND = HIST + CHUNK
ATT_HEADS = 8
ATT_DIM = 64
REL_MAX = 128
WIDTH = 512
MEM_TOKENS = 256
MEM_HEADS = 4
MEM_DIM = 128
N_EXPERTS = 32
TOP_K = 4
D_FF = 1024
SWIGLU_LIMIT = 7.0
SWIGLU_ALPHA = 1.702
DN_ALPHA = 2.0 ** 0.25
LN_EPS = 1e-5
NEG_INF = -1e30

V7X_LANES = 128
V7X_VMEM_BYTES = 64 * 1024 * 1024
VMEM_LIMIT = V7X_VMEM_BYTES - 8 * 1024 * 1024

ROW_TILE = 512
MERGE_TILE = 1024
MOE_TILE = 512
WINDOW = BAND + CHUNK
EXPERT_TILE = 1024
RUN_CHUNK = 8
LOCAL_BLOCK = 256


def _params(*sem):
    return pltpu.CompilerParams(dimension_semantics=sem, vmem_limit_bytes=VMEM_LIMIT)


def _resident(shape):
    nd = len(shape)
    return pl.BlockSpec(shape, lambda *_: (0,) * nd, pipeline_mode=pl.Buffered(1))


def _layer_norm(r, g, b):
    mu = jnp.mean(r, axis=-1, keepdims=True)
    d = r - mu
    var = jnp.mean(d * d, axis=-1, keepdims=True)
    return d * lax.rsqrt(var + LN_EPS) * g + b


def _memkv_kernel(x_ref, w_ref, kf_ref, vf_ref, kb_ref, vb_ref):
    y = jnp.dot(x_ref[...].astype(BF16), w_ref[...], preferred_element_type=F32)
    k = y[:, :WIDTH]
    v = y[:, WIDTH:]
    kf_ref[...] = k
    vf_ref[...] = v
    kb_ref[...] = k.astype(BF16)
    vb_ref[...] = v.astype(BF16)


def _memory_kv(mem2d, w_b):
    rows = mem2d.shape[0]
    tile = pl.BlockSpec((ROW_TILE, D_MODEL), lambda i: (i, 0))
    half = pl.BlockSpec((ROW_TILE, WIDTH), lambda i: (i, 0))
    return pl.pallas_call(
        _memkv_kernel,
        grid=(rows // ROW_TILE,),
        in_specs=[tile, _resident((D_MODEL, 2 * WIDTH))],
        out_specs=[half, half, half, half],
        out_shape=[jax.ShapeDtypeStruct((rows, WIDTH), F32)] * 2
        + [jax.ShapeDtypeStruct((rows, WIDTH), BF16)] * 2,
        compiler_params=_params("arbitrary"),
        name="memory_kv",
    )(mem2d, w_b)


def _proj_in_kernel(x_ref, w_ref, cw_ref, cinit_ref,
                    q_ref, k_ref, v_ref, c_ref, qm_ref, kt_ref, vt_ref, ut_ref,
                    carry_ref, *, pad_steps):
    s = pl.program_id(1)
    ns = pl.num_programs(1)
    rows = x_ref.shape[0]
    first = pad_steps
    last = ns - 1 - pad_steps

    if pad_steps:
        @pl.when((s < first) | (s > last))
        def _():
            k_ref[...] = jnp.zeros(k_ref.shape, k_ref.dtype)
            v_ref[...] = jnp.zeros(v_ref.shape, v_ref.dtype)

    @pl.when((s >= first) & (s <= last))
    def _():
        @pl.when(s == first)
        def _():
            carry_ref[...] = cinit_ref[0]

        xb = x_ref[...].astype(BF16)

        def proj(g):
            return jnp.dot(xb, w_ref[:, g * WIDTH:(g + 1) * WIDTH], preferred_element_type=F32)

        q_ref[...] = (proj(0) * (ATT_DIM ** -0.5)).astype(BF16)
        k = proj(1)
        v = proj(2)
        k_ref[0] = k.astype(BF16)
        v_ref[0] = v.astype(BF16)

        @pl.when(s == last)
        def _():
            kt_ref[0] = k
            vt_ref[0] = v

        bg = proj(3)
        u = proj(4) * proj(5)
        prev = carry_ref[...]
        row = lax.broadcasted_iota(I32, u.shape, 0)
        u1 = jnp.where(row == 0, prev[7:8], pltpu.roll(u, 1, 0))
        u2 = jnp.where(row == 0, prev[6:7], jnp.where(row == 1, prev[7:8], pltpu.roll(u, 2, 0)))
        cw = cw_ref[...]
        c_ref[...] = (bg * (cw[0:1] * u2 + cw[1:2] * u1 + cw[2:3] * u)).astype(BF16)
        tail = u[rows - 8:]
        carry_ref[...] = tail

        @pl.when(s == last)
        def _():
            ut_ref[0] = tail

        qm_ref[...] = proj(6).astype(BF16)


def _proj_in(x2d, w_b, conv_w8, conv_init, *, batch, seq, rows, pad_steps):
    n_data = seq // rows
    n_steps = n_data + 2 * pad_steps

    def data_idx(b, s):
        return (b * n_data + jnp.clip(s - pad_steps, 0, n_data - 1), 0)

    wide = pl.BlockSpec((rows, D_MODEL), data_idx)
    narrow = pl.BlockSpec((rows, WIDTH), data_idx)
    seq_blk = pl.BlockSpec((1, rows, WIDTH), lambda b, s: (b, s, 0))
    tail_blk = pl.BlockSpec((1, rows, WIDTH), lambda b, s: (b, 0, 0))
    tail8 = pl.BlockSpec((1, 8, WIDTH), lambda b, s: (b, 0, 0))
    tok = jax.ShapeDtypeStruct((batch * seq, WIDTH), BF16)
    kv = jax.ShapeDtypeStruct((batch, n_steps * rows, WIDTH), BF16)
    tail = jax.ShapeDtypeStruct((batch, rows, WIDTH), F32)
    return pl.pallas_call(
        functools.partial(_proj_in_kernel, pad_steps=pad_steps),
        grid=(batch, n_steps),
        in_specs=[wide, _resident(w_b.shape), _resident(conv_w8.shape), tail8],
        out_specs=[narrow, seq_blk, seq_blk, narrow, narrow, tail_blk, tail_blk, tail8],
        out_shape=[tok, kv, kv, tok, tok, tail, tail,
                   jax.ShapeDtypeStruct((batch, 8, WIDTH), F32)],
        scratch_shapes=[pltpu.VMEM((8, WIDTH), F32)],
        compiler_params=_params("arbitrary", "arbitrary"),
        name="proj_in",
    )(x2d, w_b, conv_w8, conv_init)


def _bias_kernel(tab_ref, out_ref):
    shape = (CHUNK, WINDOW)
    i = lax.broadcasted_iota(I32, shape, 0)
    j = lax.broadcasted_iota(I32, shape, 1)
    idx = jnp.clip(HIST + i - j, -REL_MAX, REL_MAX) + REL_MAX
    for h in range(ATT_HEADS):
        def body(d, acc, h=h):
            return jnp.where(idx == d, tab_ref[h, d], acc)
        out_ref[h] = lax.fori_loop(0, 2 * REL_MAX + 1, body, jnp.zeros(shape, F32))


def _band_bias(rel_table):
    return pl.pallas_call(
        _bias_kernel,
        in_specs=[pl.BlockSpec(memory_space=pltpu.SMEM)],
        out_shape=jax.ShapeDtypeStruct((ATT_HEADS, CHUNK, WINDOW), F32),
        name="band_bias",
    )(rel_table)


def _attention_kernel(q_ref, k_ref, v_ref, bias_ref, qm_ref, mk_ref, mv_ref,
                      a_ref, m_ref, *, chunk_rows, n_chunks, valid_keys, first_chunk, mem_rows):
    s = pl.program_id(1)
    nr = chunk_rows
    pair_w = 2 * ATT_DIM
    col = lax.broadcasted_iota(I32, (ATT_HEADS * nr, WINDOW), 1)
    lane2 = lax.broadcasted_iota(I32, (2 * nr, pair_w), 1)
    row2 = lax.broadcasted_iota(I32, (2 * nr, pair_w), 0)
    own = (lane2 >= ATT_DIM) == (row2 >= nr)
    low_lanes = lax.broadcasted_iota(I32, (nr, pair_w), 1) < ATT_DIM
    nt = (((1,), (1,)), ((), ()))

    def chunk(c, carry):
        g = first_chunk + s * n_chunks + c
        r0 = pl.multiple_of(c * nr, nr)
        w0 = pl.multiple_of((s * n_chunks + c) * CHUNK, CHUNK)
        qc = q_ref[pl.ds(r0, nr), :].astype(F32)
        kw = k_ref[0, pl.ds(w0, WINDOW), :]
        vw = v_ref[0, pl.ds(w0, WINDOW), :]
        scores = []
        for pair in range(ATT_HEADS // 2):
            lanes = slice(pair * pair_w, (pair + 1) * pair_w)
            qp = qc[:, lanes]
            q2 = jnp.where(own, jnp.concatenate([qp, qp], axis=0), 0.0).astype(BF16)
            scores.append(lax.dot_general(q2, kw[:, lanes], nt, preferred_element_type=F32))
        sc = jnp.concatenate(scores, axis=0) + bias_ref[...]
        sc = jnp.where((col >= HIST - CHUNK * g) & (col < valid_keys), sc, NEG_INF)
        m = jnp.max(sc, axis=-1, keepdims=True)
        p = jnp.exp(sc - m)
        inv = 1.0 / jnp.sum(p, axis=-1, keepdims=True)
        pb = p.astype(BF16)
        outs = []
        for pair in range(ATT_HEADS // 2):
            lanes = slice(pair * pair_w, (pair + 1) * pair_w)
            rows = slice(pair * 2 * nr, (pair + 1) * 2 * nr)
            o = jnp.dot(pb[rows], vw[:, lanes], preferred_element_type=F32) * inv[rows]
            outs.append(jnp.where(low_lanes, o[:nr], o[nr:]))
        a_ref[pl.ds(r0, nr), :] = jnp.concatenate(outs, axis=1).astype(BF16)
        return carry

    lax.fori_loop(0, n_chunks, chunk, 0, unroll=min(n_chunks, 4))

    rows = qm_ref.shape[0]
    for rb in range(rows // mem_rows):
        rs = slice(rb * mem_rows, (rb + 1) * mem_rows)
        scores = []
        for h in range(MEM_HEADS):
            lanes = slice(h * MEM_DIM, (h + 1) * MEM_DIM)
            scores.append(lax.dot_general(qm_ref[rs, lanes], mk_ref[0, :, lanes], nt,
                                          preferred_element_type=F32))
        sc = jnp.concatenate(scores, axis=0) * (MEM_DIM ** -0.5)
        m = jnp.max(sc, axis=-1, keepdims=True)
        p = jnp.exp(sc - m)
        inv = 1.0 / jnp.sum(p, axis=-1, keepdims=True)
        pb = p.astype(BF16)
        for h in range(MEM_HEADS):
            lanes = slice(h * MEM_DIM, (h + 1) * MEM_DIM)
            hs = slice(h * mem_rows, (h + 1) * mem_rows)
            o = jnp.dot(pb[hs], mv_ref[0, :, lanes], preferred_element_type=F32) * inv[hs]
            m_ref[rs, lanes] = o.astype(BF16)


def _attention(q, k_pad, v_pad, bias, qm, mk, mv, *, batch, seq, rows, chunk_rows,
               valid_keys, first_chunk, mem_rows):
    n_steps = seq // rows
    tok = pl.BlockSpec((rows, WIDTH), lambda b, s: (b * n_steps + s, 0))
    whole_seq = pl.BlockSpec((1, k_pad.shape[1], WIDTH), lambda b, s: (b, 0, 0))
    mem = pl.BlockSpec((1, MEM_TOKENS, WIDTH), lambda b, s: (b, 0, 0))
    kern = functools.partial(
        _attention_kernel, chunk_rows=chunk_rows, n_chunks=rows // chunk_rows,
        valid_keys=valid_keys, first_chunk=first_chunk, mem_rows=mem_rows)
    out = jax.ShapeDtypeStruct((batch * seq, WIDTH), BF16)
    return pl.pallas_call(
        kern,
        grid=(batch, n_steps),
        in_specs=[tok, whole_seq, whole_seq, _resident(bias.shape), tok, mem, mem],
        out_specs=[tok, tok],
        out_shape=[out, out],
        compiler_params=_params("arbitrary", "arbitrary"),
        name="attention",
    )(q, k_pad, v_pad, bias, qm, mk, mv)


def _merge_kernel(x_ref, a_ref, c_ref, m_ref, wg_ref, wa_ref, wc_ref, wm_ref, wo_ref,
                  g_ref, b_ref, wr2_ref, wrh_ref, br_ref,
                  y_ref, sel_ref, topi_ref, topg_ref):
    x = x_ref[...]
    xb = x.astype(BF16)
    branches = [(a_ref[...], wa_ref), (c_ref[...], wc_ref), (m_ref[...], wm_ref)]
    out = None
    n_col = 256
    for j in range(D_MODEL // n_col):
        cols = slice(j * n_col, (j + 1) * n_col)
        comb = None
        for i, (br, wbr) in enumerate(branches):
            gcols = slice(i * D_MODEL + j * n_col, i * D_MODEL + (j + 1) * n_col)
            gate = jax.nn.sigmoid(jnp.dot(xb, wg_ref[:, gcols], preferred_element_type=F32))
            term = gate * jnp.dot(br, wbr[:, cols], preferred_element_type=F32)
            comb = term if comb is None else comb + term
        part = jnp.dot(comb.astype(BF16), wo_ref[cols, :], preferred_element_type=F32)
        out = part if out is None else out + part
    y = _layer_norm(DN_ALPHA * x + out, g_ref[...], b_ref[...])
    y_ref[...] = y

    y_hi = y.astype(BF16)
    y_lo = (y - y_hi.astype(F32)).astype(BF16)
    both = jnp.dot(y_hi, wr2_ref[...], preferred_element_type=F32)
    logits = (both[:, :N_EXPERTS] + both[:, N_EXPERTS:]
              + jnp.dot(y_lo, wrh_ref[...], preferred_element_type=F32) + br_ref[...])

    rows = logits.shape[0]
    lane = lax.broadcasted_iota(I32, (rows, N_EXPERTS), 1).astype(F32)
    lane_out = lax.broadcasted_iota(I32, (rows, V7X_LANES), 1)
    work = logits
    sel = jnp.zeros((rows, N_EXPERTS), F32)
    topi = jnp.zeros((rows, V7X_LANES), I32)
    vals = []
    for k in range(TOP_K):
        top = jnp.max(work, axis=-1, keepdims=True)
        idx = jnp.min(jnp.where(work == top, lane, float(N_EXPERTS)), axis=-1, keepdims=True)
        hit = lane == idx
        sel = jnp.where(hit, 1.0, sel)
        work = jnp.where(hit, -jnp.inf, work)
        topi = jnp.where(lane_out == k, idx.astype(I32), topi)
        vals.append(top)
    exps = [jnp.exp(v - vals[0]) for v in vals]
    denom = exps[0] + exps[1] + exps[2] + exps[3]
    topg = jnp.zeros((rows, V7X_LANES), F32)
    for k in range(TOP_K):
        topg = jnp.where(lane_out == k, exps[k] / denom, topg)
    sel_ref[...] = sel.astype(BF16)
    topi_ref[...] = topi
    topg_ref[...] = topg


def _merge(x2d, a, c, m, weights, *, rows):
    n = x2d.shape[0]
    wide = pl.BlockSpec((rows, D_MODEL), lambda i: (i, 0))
    narrow = pl.BlockSpec((rows, WIDTH), lambda i: (i, 0))
    lanes = pl.BlockSpec((rows, V7X_LANES), lambda i: (i, 0))
    return pl.pallas_call(
        _merge_kernel,
        grid=(n // rows,),
        in_specs=[wide, narrow, narrow, narrow] + [_resident(w.shape) for w in weights],
        out_specs=[wide, pl.BlockSpec((rows, N_EXPERTS), lambda i: (i, 0)), lanes, lanes],
        out_shape=[jax.ShapeDtypeStruct((n, D_MODEL), F32),
                   jax.ShapeDtypeStruct((n, N_EXPERTS), BF16),
                   jax.ShapeDtypeStruct((n, V7X_LANES), I32),
                   jax.ShapeDtypeStruct((n, V7X_LANES), F32)],
        compiler_params=_params("arbitrary"),
        name="merge_router",
    )(x2d, a, c, m, *weights)


def _rank_kernel(sel_ref, topi_ref, init_ref, lpos_ref, before_ref, cnt_ref, carry_ref):
    @pl.when(pl.program_id(0) == 0)
    def _():
        carry_ref[...] = init_ref[...]

    before_ref[0] = carry_ref[...]
    sel = sel_ref[...]
    rows = sel.shape[0]
    r = lax.broadcasted_iota(I32, (rows, rows), 0)
    c = lax.broadcasted_iota(I32, (rows, rows), 1)
    earlier = jnp.where(c < r, 1.0, 0.0).astype(BF16)
    in_tile = jnp.dot(earlier, sel, preferred_element_type=F32)
    n_run = jnp.sum(sel.astype(F32), axis=0, keepdims=True)
    run_rows = jnp.ceil(n_run * (1.0 / RUN_CHUNK)) * RUN_CHUNK
    er = lax.broadcasted_iota(I32, (N_EXPERTS, N_EXPERTS), 0)
    ec = lax.broadcasted_iota(I32, (N_EXPERTS, N_EXPERTS), 1)
    lower_experts = jnp.where(er < ec, 1.0, 0.0).astype(BF16)
    local0 = jnp.dot(jnp.broadcast_to(run_rows, (8, N_EXPERTS)).astype(BF16), lower_experts,
                     preferred_element_type=F32)[0:1]
    pos = in_tile + local0
    lane = lax.broadcasted_iota(I32, (rows, N_EXPERTS), 1)
    lane_out = lax.broadcasted_iota(I32, (rows, V7X_LANES), 1)
    topi = topi_ref[...]
    lpos = jnp.full((rows, V7X_LANES), -1, I32)
    for k in range(TOP_K):
        mine = jnp.sum(jnp.where(lane == topi[:, k:k + 1], pos, 0.0), axis=-1, keepdims=True)
        lpos = jnp.where(lane_out == k, mine.astype(I32), lpos)
    lpos_ref[...] = lpos
    total = carry_ref[...] + n_run
    carry_ref[...] = total
    cnt_ref[...] = total


def _ranks(sel, topi, init, *, rows):
    n = sel.shape[0]
    small = pl.BlockSpec((8, N_EXPERTS), lambda i: (0, 0))
    lanes = pl.BlockSpec((rows, V7X_LANES), lambda i: (i, 0))
    return pl.pallas_call(
        _rank_kernel,
        grid=(n // rows,),
        in_specs=[pl.BlockSpec((rows, N_EXPERTS), lambda i: (i, 0)), lanes, small],
        out_specs=[lanes, pl.BlockSpec((1, 8, N_EXPERTS), lambda i: (i, 0, 0)), small],
        out_shape=[jax.ShapeDtypeStruct((n, V7X_LANES), I32),
                   jax.ShapeDtypeStruct((n // rows, 8, N_EXPERTS), F32),
                   jax.ShapeDtypeStruct((8, N_EXPERTS), F32)],
        scratch_shapes=[pltpu.VMEM((8, N_EXPERTS), F32)],
        compiler_params=_params("arbitrary"),
        name="expert_ranks",
    )(sel, topi, init)


SCAL_SORTED0 = 0
SCAL_CHUNKS = N_EXPERTS
SCAL_LOCAL0 = 2 * N_EXPERTS
SCAL_TOTAL = 3 * N_EXPERTS
COPY_CHUNKS = (8, 4, 2, 1)
COPY_ROWS = tuple(c * RUN_CHUNK for c in COPY_CHUNKS)
SCAL_PREV_TOTAL = SCAL_TOTAL + len(COPY_CHUNKS)


def _run_copy(local_ref, local_row, sorted_ref, sorted_row, sems, size, to_sorted):
    loc = local_ref.at[pl.ds(local_row, COPY_ROWS[size])]
    srt = sorted_ref.at[pl.ds(sorted_row, COPY_ROWS[size])]
    sem = sems[size]
    return pltpu.make_async_copy(loc, srt, sem) if to_sorted else pltpu.make_async_copy(srt, loc, sem)


def _start_runs(scal_ref, local_ref, sorted_ref, sems, *, to_sorted):
    def expert(e, carry):
        sorted0 = scal_ref[0, 0, SCAL_SORTED0 + e]
        local0 = scal_ref[0, 0, SCAL_LOCAL0 + e]
        n_chunk = scal_ref[0, 0, SCAL_CHUNKS + e]
        n_big = lax.shift_right_logical(n_chunk, COPY_CHUNKS[0].bit_length() - 1)

        def copy(size, offset):
            return _run_copy(local_ref, pl.multiple_of(local0 + offset, RUN_CHUNK), sorted_ref,
                             pl.multiple_of(sorted0 + offset, RUN_CHUNK), sems, size, to_sorted)

        def big(c, inner):
            copy(0, c * COPY_ROWS[0]).start()
            return inner

        lax.fori_loop(0, n_big, big, 0)
        offset = n_big * COPY_ROWS[0]
        for size in range(1, len(COPY_CHUNKS)):
            take = (n_chunk & COPY_CHUNKS[size]) != 0

            @pl.when(take)
            def _(size=size, offset=offset):
                copy(size, offset).start()

            offset = offset + jnp.where(take, COPY_ROWS[size], 0)
        return carry

    lax.fori_loop(0, N_EXPERTS, expert, 0)


def _wait_runs(scal_ref, total_lane, local_ref, sorted_ref, sems, *, to_sorted):
    for size in range(len(COPY_ROWS)):
        def drain(c, carry, size=size):
            _run_copy(local_ref, 0, sorted_ref, 0, sems, size, to_sorted).wait()
            return carry

        lax.fori_loop(0, scal_ref[0, 0, total_lane + size], drain, 0)


def _dispatch_kernel(scal_ref, lpos_ref, y_ref, *rest, local_rows):
    sorted_ref, stage = rest[-2 - len(COPY_CHUNKS):-len(COPY_CHUNKS)]
    sem_arrays = rest[-len(COPY_CHUNKS):]
    i = pl.program_id(0)
    slot = i % 2
    sems = [tuple(a.at[s] for a in sem_arrays) for s in (slot, 1 - slot)]
    rows = y_ref.shape[0]
    yb = y_ref[...].astype(BF16)
    lpos = lpos_ref[0]
    blk = LOCAL_BLOCK
    for rb in range(local_rows // blk):
        j = lax.broadcasted_iota(I32, (blk, rows), 0) + rb * blk
        onehot = jnp.zeros((blk, rows), F32)
        for k in range(TOP_K):
            onehot = jnp.where(j == lpos[k:k + 1], 1.0, onehot)
        onehot = onehot.astype(BF16)
        stage[slot, rb * blk:(rb + 1) * blk, :] = jnp.dot(onehot, yb, preferred_element_type=F32)
    _start_runs(scal_ref, stage.at[slot], sorted_ref, sems[0], to_sorted=True)

    @pl.when(i > 0)
    def _():
        _wait_runs(scal_ref, SCAL_PREV_TOTAL, stage.at[1 - slot], sorted_ref, sems[1],
                   to_sorted=True)

    @pl.when(i == pl.num_programs(0) - 1)
    def _():
        _wait_runs(scal_ref, SCAL_TOTAL, stage.at[slot], sorted_ref, sems[0], to_sorted=True)


def _local_rows(rows):
    worst = rows * TOP_K + N_EXPERTS * (RUN_CHUNK - 1)
    return (worst + LOCAL_BLOCK - 1) // LOCAL_BLOCK * LOCAL_BLOCK


def _dispatch(scal, lpos_t, y1, sorted_rows, *, rows, n_sorted):
    n = y1.shape[0]
    local_rows = _local_rows(rows)
    any_spec = pl.BlockSpec(memory_space=pl.ANY)
    in_specs = [pl.BlockSpec((1, 1, V7X_LANES), lambda i: (i, 0, 0), memory_space=pltpu.SMEM),
                pl.BlockSpec((1, 8, rows), lambda i: (i, 0, 0)),
                pl.BlockSpec((rows, D_MODEL), lambda i: (i, 0))]
    args = [scal, lpos_t, y1]
    aliases = {}
    if sorted_rows is not None:
        in_specs.append(any_spec)
        args.append(sorted_rows)
        aliases = {3: 0}
    return pl.pallas_call(
        functools.partial(_dispatch_kernel, local_rows=local_rows),
        grid=(n // rows,),
        in_specs=in_specs,
        out_specs=any_spec,
        out_shape=jax.ShapeDtypeStruct((n_sorted, D_MODEL), F32),
        scratch_shapes=[pltpu.VMEM((2, local_rows, D_MODEL), F32),
                        *[pltpu.SemaphoreType.DMA((2,))] * len(COPY_CHUNKS)],
        input_output_aliases=aliases,
        compiler_params=_params("arbitrary"),
        name="dispatch_rows",
    )(*args)


def _expert_kernel(te_ref, tr_ref, tb_ref, x_ref, wgu_ref, bgu_ref, wd_ref, bd_ref, o_ref,
                   wgu_b, wd_b):
    del tb_ref
    i = pl.program_id(0)
    e = te_ref[i]
    changed = (i == 0) | (e != te_ref[jnp.maximum(i - 1, 0)])

    @pl.when(changed)
    def _():
        wgu_b[...] = wgu_ref[0].astype(BF16)
        wd_b[...] = wd_ref[0].astype(BF16)

    def ffn(n_rows):
        row = lax.broadcasted_iota(I32, (n_rows, D_MODEL), 0)
        xb = jnp.where(row < tr_ref[i], x_ref[0:n_rows, :], 0.0).astype(BF16)
        bgu = bgu_ref[0]
        acc = None
        n_col = 256
        for j in range(D_FF // n_col):
            cg = slice(j * n_col, (j + 1) * n_col)
            cl = slice(D_FF + j * n_col, D_FF + (j + 1) * n_col)
            hg = jnp.dot(xb, wgu_b[:, cg], preferred_element_type=F32) + bgu[:, cg]
            hl = jnp.dot(xb, wgu_b[:, cl], preferred_element_type=F32) + bgu[:, cl]
            glu = jnp.minimum(hg, SWIGLU_LIMIT)
            lin = jnp.clip(hl, -SWIGLU_LIMIT, SWIGLU_LIMIT)
            act = glu * jax.nn.sigmoid(SWIGLU_ALPHA * glu) * (lin + 1.0)
            part = jnp.dot(act.astype(BF16), wd_b[cg, :], preferred_element_type=F32)
            acc = part if acc is None else acc + part
        o_ref[0:n_rows, :] = acc + bd_ref[0]

    half = EXPERT_TILE // 2

    @pl.when(tr_ref[i] > half)
    def _():
        ffn(EXPERT_TILE)

    @pl.when((tr_ref[i] > 0) & (tr_ref[i] <= half))
    def _():
        ffn(half)


def _experts(tile_expert, tile_rows, tile_block, xs, w_gate_up, b_gate_up, w_down, b_down):
    n_tiles = xs.shape[0] // EXPERT_TILE
    grid_spec = pltpu.PrefetchScalarGridSpec(
        num_scalar_prefetch=3,
        grid=(n_tiles,),
        in_specs=[
            pl.BlockSpec((EXPERT_TILE, D_MODEL), lambda i, te, tr, tb: (tb[i], 0)),
            pl.BlockSpec((1, D_MODEL, 2 * D_FF), lambda i, te, tr, tb: (te[i], 0, 0)),
            pl.BlockSpec((1, 1, 2 * D_FF), lambda i, te, tr, tb: (te[i], 0, 0)),
            pl.BlockSpec((1, D_FF, D_MODEL), lambda i, te, tr, tb: (te[i], 0, 0)),
            pl.BlockSpec((1, 1, D_MODEL), lambda i, te, tr, tb: (te[i], 0, 0)),
        ],
        out_specs=pl.BlockSpec((EXPERT_TILE, D_MODEL), lambda i, te, tr, tb: (tb[i], 0)),
        scratch_shapes=[pltpu.VMEM((D_MODEL, 2 * D_FF), BF16), pltpu.VMEM((D_FF, D_MODEL), BF16)],
    )
    return pl.pallas_call(
        _expert_kernel,
        grid_spec=grid_spec,
        out_shape=jax.ShapeDtypeStruct(xs.shape, F32),
        compiler_params=_params("arbitrary"),
        name="expert_ffn",
    )(tile_expert, tile_rows, tile_block, xs, w_gate_up, b_gate_up, w_down, b_down)


def _combine_kernel(scal_ref, next_scal_ref, lpos_ref, gate_ref, y_ref, g_ref, b_ref, eo_ref,
                    out_ref, buf, *sem_arrays, local_rows):
    i = pl.program_id(0)
    slot = i % 2
    sems = [tuple(a.at[s] for a in sem_arrays) for s in (slot, 1 - slot)]

    @pl.when(i == 0)
    def _():
        buf[...] = jnp.zeros(buf.shape, buf.dtype)
        _start_runs(scal_ref, buf.at[0], eo_ref, tuple(a.at[0] for a in sem_arrays), to_sorted=False)

    @pl.when(i + 1 < pl.num_programs(0))
    def _():
        _start_runs(next_scal_ref, buf.at[1 - slot], eo_ref, sems[1], to_sorted=False)

    _wait_runs(scal_ref, SCAL_TOTAL, buf.at[slot], eo_ref, sems[0], to_sorted=False)
    rows = y_ref.shape[0]
    lpos = lpos_ref[...]
    gate = gate_ref[...]
    blk = LOCAL_BLOCK
    lane = lax.broadcasted_iota(I32, (rows, V7X_LANES), 1)
    slot_row = [jnp.broadcast_to(lpos[:, k:k + 1], (rows, V7X_LANES)) for k in range(TOP_K)]
    slot_gate = [jnp.broadcast_to(gate[:, k:k + 1], (rows, V7X_LANES)) for k in range(TOP_K)]
    moe = None
    for kb in range(local_rows // blk):
        pieces = []
        for jb in range(blk // V7X_LANES):
            col = lane + (kb * blk + jb * V7X_LANES)
            w = jnp.zeros((rows, V7X_LANES), F32)
            for k in range(TOP_K):
                w = jnp.where(col == slot_row[k], slot_gate[k], w)
            pieces.append(w)
        weights = jnp.concatenate(pieces, axis=1).astype(BF16)
        part = jnp.dot(weights, buf[slot, kb * blk:(kb + 1) * blk, :].astype(BF16),
                       preferred_element_type=F32)
        moe = part if moe is None else moe + part
    out_ref[...] = _layer_norm(DN_ALPHA * y_ref[...] + moe, g_ref[...], b_ref[...])


def _combine(scal, lpos, gates, y1, ln_g, ln_b, expert_out, *, rows):
    n = y1.shape[0]
    local_rows = _local_rows(rows)
    wide = pl.BlockSpec((rows, D_MODEL), lambda i: (i, 0))
    lanes = pl.BlockSpec((rows, V7X_LANES), lambda i: (i, 0))
    n_steps = n // rows
    scalars = pl.BlockSpec((1, 1, V7X_LANES), lambda i: (i, 0, 0), memory_space=pltpu.SMEM)
    next_scalars = pl.BlockSpec((1, 1, V7X_LANES), lambda i: (jnp.minimum(i + 1, n_steps - 1), 0, 0),
                                memory_space=pltpu.SMEM)
    return pl.pallas_call(
        functools.partial(_combine_kernel, local_rows=local_rows),
        grid=(n_steps,),
        in_specs=[scalars, next_scalars, lanes, lanes, wide, _resident(ln_g.shape),
                  _resident(ln_b.shape), pl.BlockSpec(memory_space=pl.ANY)],
        out_specs=wide,
        out_shape=jax.ShapeDtypeStruct((n, D_MODEL), F32),
        scratch_shapes=[pltpu.VMEM((2, local_rows, D_MODEL), F32),
                        *[pltpu.SemaphoreType.DMA((2,))] * len(COPY_CHUNKS)],
        compiler_params=_params("arbitrary"),
        name="combine_norm",
    )(scal, scal, lpos, gates, y1, ln_g, ln_b, expert_out)


def _tile_tables(lpos, n_chunk, sorted0, rows):
    n_tiles = lpos.shape[0] // rows
    local0 = (jnp.cumsum(n_chunk, axis=1) - n_chunk) * RUN_CHUNK
    per_size = [n_chunk // COPY_CHUNKS[0]] + [(n_chunk // c) % 2 for c in COPY_CHUNKS[1:]]
    total = jnp.stack([jnp.sum(n, axis=1) for n in per_size], axis=1)
    prev_total = jnp.concatenate([jnp.zeros((1, len(COPY_CHUNKS)), I32), total[:-1]], axis=0)
    pad = jnp.zeros((n_tiles, V7X_LANES - SCAL_PREV_TOTAL - len(COPY_CHUNKS)), I32)
    scal = jnp.concatenate([sorted0, n_chunk, local0, total, prev_total, pad], axis=1)
    lpos_t = jnp.transpose(lpos[:, :8].reshape(n_tiles, rows, 8), (0, 2, 1))
    return scal.reshape(n_tiles, 1, V7X_LANES), lpos_t


def kernel(x_prompt, x_sample, mem_prompt, cache_attn_k, cache_attn_v, cache_conv, cache_mem_k,
           cache_mem_v, w_in, rel_bias, conv_w, w_mem_kv, w_gate, w_br_attn, w_br_conv, w_br_mem,
           w_out, ln1_g, ln1_b, w_router, b_router, w_gate_up, b_gate_up, w_down, b_down,
           ln2_g, ln2_b):
    depth = w_in.shape[0]
    assert depth == 1, "one layer only"
    batch, seq, _ = x_prompt.shape
    dec_batch, dec_seq, _ = x_sample.shape
    n_prompt = batch * seq
    n_sample = dec_batch * dec_seq
    assert seq % ROW_TILE == 0 and HIST == ROW_TILE and cache_attn_k.shape[2] == HIST
    assert dec_seq % 16 == 0 and dec_seq <= CHUNK and cache_conv.shape[2] == 2

    w_in_b = w_in[0].astype(BF16)
    w_mem_b = w_mem_kv[0].astype(BF16)
    wr = w_router[0]
    wr_hi = wr.astype(BF16)
    wr_lo = (wr - wr_hi.astype(F32)).astype(BF16)
    merge_w = (w_gate[0].astype(BF16), w_br_attn[0].astype(BF16), w_br_conv[0].astype(BF16),
               w_br_mem[0].astype(BF16), w_out[0].astype(BF16), ln1_g, ln1_b,
               jnp.concatenate([wr_hi, wr_lo], axis=1), wr_hi, b_router)
    conv_w8 = jnp.pad(conv_w[0], ((0, 5), (0, 0)))
    bias = _band_bias(rel_bias[0])

    xp = x_prompt.reshape(n_prompt, D_MODEL)
    mk_f, mv_f, mk_b, mv_b = _memory_kv(mem_prompt.reshape(batch * MEM_TOKENS, D_MODEL), w_mem_b)
    q, k_pad, v_pad, c, qm, k_tail, v_tail, u_tail = _proj_in(
        xp, w_in_b, conv_w8, jnp.zeros((batch, 8, WIDTH), F32),
        batch=batch, seq=seq, rows=ROW_TILE, pad_steps=1)
    a, m = _attention(
        q, k_pad, v_pad, bias.reshape(ATT_HEADS * CHUNK, WINDOW), qm,
        mk_b.reshape(batch, MEM_TOKENS, WIDTH),
        mv_b.reshape(batch, MEM_TOKENS, WIDTH), batch=batch, seq=seq, rows=ROW_TILE,
        chunk_rows=CHUNK, valid_keys=BAND, first_chunk=0, mem_rows=256)
    y1_p, sel_p, topi_p, topg_p = _merge(xp, a, c, m, merge_w, rows=MERGE_TILE)

    xs = x_sample.reshape(n_sample, D_MODEL)
    conv_init = jnp.pad(cache_conv[0], ((0, 0), (6, 0), (0, 0)))
    q_s, k_s, v_s, c_s, qm_s, k_new, v_new, u_tail_s = _proj_in(
        xs, w_in_b, conv_w8, conv_init, batch=dec_batch, seq=dec_seq, rows=dec_seq, pad_steps=0)
    pad_rows = WINDOW - HIST - dec_seq

    def window(cache, new):
        cache = cache[0].reshape(dec_batch, HIST, WIDTH).astype(BF16)
        return jnp.pad(jnp.concatenate([cache, new], axis=1), ((0, 0), (0, pad_rows), (0, 0)))

    a_s, m_s = _attention(
        q_s, window(cache_attn_k, k_s), window(cache_attn_v, v_s),
        bias[:, :dec_seq].reshape(ATT_HEADS * dec_seq, WINDOW), qm_s,
        cache_mem_k[0].reshape(dec_batch, MEM_TOKENS, WIDTH).astype(BF16),
        cache_mem_v[0].reshape(dec_batch, MEM_TOKENS, WIDTH).astype(BF16),
        batch=dec_batch, seq=dec_seq, rows=dec_seq, chunk_rows=dec_seq,
        valid_keys=HIST + dec_seq, first_chunk=LEFT_CHUNKS, mem_rows=dec_seq)
    y1_s, sel_s, topi_s, topg_s = _merge(xs, a_s, c_s, m_s, merge_w, rows=n_sample)

    lpos_p, before_p, cnt_p = _ranks(sel_p, topi_p, jnp.zeros((8, N_EXPERTS), F32), rows=MOE_TILE)
    lpos_s, before_s, cnt = _ranks(sel_s, topi_s, cnt_p, rows=n_sample)
    before = jnp.concatenate([before_p[:, 0], before_s[:, 0]], axis=0).astype(I32)
    n_run = jnp.concatenate([before[1:], cnt[:1].astype(I32)], axis=0) - before
    n_chunk = (n_run + RUN_CHUNK - 1) // RUN_CHUNK
    run_rows = n_chunk * RUN_CHUNK
    counts = jnp.sum(run_rows, axis=0)
    padded = (counts + EXPERT_TILE - 1) // EXPERT_TILE * EXPERT_TILE
    ends = jnp.cumsum(padded)
    starts = ends - padded
    sorted0 = starts[None, :] + jnp.cumsum(run_rows, axis=0) - run_rows
    n_token_tiles = n_prompt // MOE_TILE + 1
    n_tiles = ((n_prompt + n_sample) * TOP_K
               + n_token_tiles * N_EXPERTS * (RUN_CHUNK - 1)) // EXPERT_TILE + N_EXPERTS
    tile_start = jnp.arange(n_tiles, dtype=I32) * EXPERT_TILE
    tile_expert = jnp.minimum(
        jnp.sum((ends[None, :] <= tile_start[:, None]).astype(I32), axis=1), N_EXPERTS - 1)
    tile_rows = jnp.clip(starts[tile_expert] + counts[tile_expert] - tile_start, 0, EXPERT_TILE)
    last_used = jnp.maximum(ends[-1] // EXPERT_TILE - 1, 0)
    tile_block = jnp.minimum(jnp.arange(n_tiles, dtype=I32), last_used)
    tile_expert = jnp.where(jnp.arange(n_tiles) > last_used, tile_expert[last_used], tile_expert)
    scal_p, lpos_tp = _tile_tables(lpos_p, n_chunk[:-1], sorted0[:-1], MOE_TILE)
    scal_s, lpos_ts = _tile_tables(lpos_s, n_chunk[-1:], sorted0[-1:], n_sample)

    n_sorted = n_tiles * EXPERT_TILE
    sorted_rows = _dispatch(scal_p, lpos_tp, y1_p, None, rows=MOE_TILE, n_sorted=n_sorted)
    sorted_rows = _dispatch(scal_s, lpos_ts, y1_s, sorted_rows, rows=n_sample, n_sorted=n_sorted)
    expert_out = _experts(tile_expert, tile_rows, tile_block, sorted_rows, w_gate_up[0],
                          b_gate_up[0].reshape(N_EXPERTS, 1, 2 * D_FF), w_down[0],
                          b_down[0].reshape(N_EXPERTS, 1, D_MODEL))
    y_p = _combine(scal_p, lpos_p, topg_p, y1_p, ln2_g, ln2_b, expert_out, rows=MOE_TILE)
    y_s = _combine(scal_s, lpos_s, topg_s, y1_s, ln2_g, ln2_b, expert_out, rows=n_sample)

    def heads(t, n_b, n_rows, n_heads, dim):
        return t.reshape(1, n_b, n_rows, n_heads, dim)

    return (y_p.reshape(batch, seq, D_MODEL),
            y_s.reshape(dec_batch, dec_seq, D_MODEL),
            heads(k_tail, batch, HIST, ATT_HEADS, ATT_DIM),
            heads(v_tail, batch, HIST, ATT_HEADS, ATT_DIM),
            u_tail[:, 6:8].reshape(1, batch, 2, WIDTH),
            heads(mk_f, batch, MEM_TOKENS, MEM_HEADS, MEM_DIM),
            heads(mv_f, batch, MEM_TOKENS, MEM_HEADS, MEM_DIM),
            heads(k_new, dec_batch, dec_seq, ATT_HEADS, ATT_DIM),
            heads(v_new, dec_batch, dec_seq, ATT_HEADS, ATT_DIM),
            u_tail_s[:, 6:8].reshape(1, dec_batch, 2, WIDTH))
```

```python
import functools

import jax
import jax.numpy as jnp
from jax import lax
from jax.experimental import pallas as pl
from jax.experimental.pallas import tpu as pltpu

F32 = jnp.float32
BF16 = jnp.bfloat16
I32 = jnp.int32

D_MODEL = 1024
CHUNK = 64
LEFT_CHUNKS = 8
HIST = LEFT_CHUNKS * CHUNK
BAND = HIST + CHUNK
ATT_HEADS = 8
ATT_DIM = 64
REL_MAX = 128
WIDTH = 512
MEM_TOKENS = 256
MEM_HEADS = 4
MEM_DIM = 128
N_EXPERTS = 32
TOP_K = 4
D_FF = 1024
SWIGLU_LIMIT = 7.0
SWIGLU_ALPHA = 1.702
DN_ALPHA = 2.0 ** 0.25
LN_EPS = 1e-5
NEG_INF = -1e30

V7X_LANES = 128
V7X_VMEM_BYTES = 64 * 1024 * 1024
VMEM_LIMIT = V7X_VMEM_BYTES - 8 * 1024 * 1024

ROW_TILE = 512
MERGE_TILE = 1024
MOE_TILE = 512
WINDOW = BAND + CHUNK
EXPERT_TILE = 1024
RUN_CHUNK = 8
LOCAL_BLOCK = 256


def _params(*sem):
    return pltpu.CompilerParams(dimension_semantics=sem, vmem_limit_bytes=VMEM_LIMIT)


def _resident(shape):
    nd = len(shape)
    return pl.BlockSpec(shape, lambda *_: (0,) * nd, pipeline_mode=pl.Buffered(1))


def _layer_norm(r, g, b):
    mu = jnp.mean(r, axis=-1, keepdims=True)
    d = r - mu
    var = jnp.mean(d * d, axis=-1, keepdims=True)
    return d * lax.rsqrt(var + LN_EPS) * g + b


def _memkv_kernel(x_ref, w_ref, kf_ref, vf_ref, kb_ref, vb_ref):
    y = jnp.dot(x_ref[...].astype(BF16), w_ref[...], preferred_element_type=F32)
    k = y[:, :WIDTH]
    v = y[:, WIDTH:]
    kf_ref[...] = k
    vf_ref[...] = v
    kb_ref[...] = k.astype(BF16)
    vb_ref[...] = v.astype(BF16)


def _memory_kv(mem2d, w_b):
    rows = mem2d.shape[0]
    tile = pl.BlockSpec((ROW_TILE, D_MODEL), lambda i: (i, 0))
    half = pl.BlockSpec((ROW_TILE, WIDTH), lambda i: (i, 0))
    return pl.pallas_call(
        _memkv_kernel,
        grid=(rows // ROW_TILE,),
        in_specs=[tile, _resident((D_MODEL, 2 * WIDTH))],
        out_specs=[half, half, half, half],
        out_shape=[jax.ShapeDtypeStruct((rows, WIDTH), F32)] * 2
        + [jax.ShapeDtypeStruct((rows, WIDTH), BF16)] * 2,
        compiler_params=_params("arbitrary"),
        name="memory_kv",
    )(mem2d, w_b)


def _proj_in_kernel(x_ref, w_ref, cw_ref, cinit_ref,
                    q_ref, k_ref, v_ref, c_ref, qm_ref, kt_ref, vt_ref, ut_ref,
                    carry_ref, *, pad_steps):
    s = pl.program_id(1)
    ns = pl.num_programs(1)
    rows = x_ref.shape[0]
    first = pad_steps
    last = ns - 1 - pad_steps

    if pad_steps:
        @pl.when((s < first) | (s > last))
        def _():
            k_ref[...] = jnp.zeros(k_ref.shape, k_ref.dtype)
            v_ref[...] = jnp.zeros(v_ref.shape, v_ref.dtype)

    @pl.when((s >= first) & (s <= last))
    def _():
        @pl.when(s == first)
        def _():
            carry_ref[...] = cinit_ref[0]

        xb = x_ref[...].astype(BF16)

        def proj(g):
            return jnp.dot(xb, w_ref[:, g * WIDTH:(g + 1) * WIDTH], preferred_element_type=F32)

        q_ref[...] = (proj(0) * (ATT_DIM ** -0.5)).astype(BF16)
        k = proj(1)
        v = proj(2)
        k_ref[0] = k.astype(BF16)
        v_ref[0] = v.astype(BF16)

        @pl.when(s == last)
        def _():
            kt_ref[0] = k
            vt_ref[0] = v

        bg = proj(3)
        u = proj(4) * proj(5)
        prev = carry_ref[...]
        row = lax.broadcasted_iota(I32, u.shape, 0)
        u1 = jnp.where(row == 0, prev[7:8], pltpu.roll(u, 1, 0))
        u2 = jnp.where(row == 0, prev[6:7], jnp.where(row == 1, prev[7:8], pltpu.roll(u, 2, 0)))
        cw = cw_ref[...]
        c_ref[...] = (bg * (cw[0:1] * u2 + cw[1:2] * u1 + cw[2:3] * u)).astype(BF16)
        tail = u[rows - 8:]
        carry_ref[...] = tail

        @pl.when(s == last)
        def _():
            ut_ref[0] = tail

        qm_ref[...] = proj(6).astype(BF16)


def _proj_in(x2d, w_b, conv_w8, conv_init, *, batch, seq, rows, pad_steps):
    n_data = seq // rows
    n_steps = n_data + 2 * pad_steps

    def data_idx(b, s):
        return (b * n_data + jnp.clip(s - pad_steps, 0, n_data - 1), 0)

    wide = pl.BlockSpec((rows, D_MODEL), data_idx)
    narrow = pl.BlockSpec((rows, WIDTH), data_idx)
    seq_blk = pl.BlockSpec((1, rows, WIDTH), lambda b, s: (b, s, 0))
    tail_blk = pl.BlockSpec((1, rows, WIDTH), lambda b, s: (b, 0, 0))
    tail8 = pl.BlockSpec((1, 8, WIDTH), lambda b, s: (b, 0, 0))
    tok = jax.ShapeDtypeStruct((batch * seq, WIDTH), BF16)
    kv = jax.ShapeDtypeStruct((batch, n_steps * rows, WIDTH), BF16)
    tail = jax.ShapeDtypeStruct((batch, rows, WIDTH), F32)
    return pl.pallas_call(
        functools.partial(_proj_in_kernel, pad_steps=pad_steps),
        grid=(batch, n_steps),
        in_specs=[wide, _resident(w_b.shape), _resident(conv_w8.shape), tail8],
        out_specs=[narrow, seq_blk, seq_blk, narrow, narrow, tail_blk, tail_blk, tail8],
        out_shape=[tok, kv, kv, tok, tok, tail, tail,
                   jax.ShapeDtypeStruct((batch, 8, WIDTH), F32)],
        scratch_shapes=[pltpu.VMEM((8, WIDTH), F32)],
        compiler_params=_params("arbitrary", "arbitrary"),
        name="proj_in",
    )(x2d, w_b, conv_w8, conv_init)


def _bias_kernel(tab_ref, out_ref):
    shape = (CHUNK, WINDOW)
    i = lax.broadcasted_iota(I32, shape, 0)
    j = lax.broadcasted_iota(I32, shape, 1)
    idx = jnp.clip(HIST + i - j, -REL_MAX, REL_MAX) + REL_MAX
    for h in range(ATT_HEADS):
        def body(d, acc, h=h):
            return jnp.where(idx == d, tab_ref[h, d], acc)
        out_ref[h] = lax.fori_loop(0, 2 * REL_MAX + 1, body, jnp.zeros(shape, F32))


def _band_bias(rel_table):
    return pl.pallas_call(
        _bias_kernel,
        in_specs=[pl.BlockSpec(memory_space=pltpu.SMEM)],
        out_shape=jax.ShapeDtypeStruct((ATT_HEADS, CHUNK, WINDOW), F32),
        name="band_bias",
    )(rel_table)


def _attention_kernel(q_ref, k_ref, v_ref, bias_ref, qm_ref, mk_ref, mv_ref,
                      a_ref, m_ref, *, chunk_rows, n_chunks, valid_keys, first_chunk, mem_rows):
    s = pl.program_id(1)
    nr = chunk_rows
    pair_w = 2 * ATT_DIM
    col = lax.broadcasted_iota(I32, (ATT_HEADS * nr, WINDOW), 1)
    lane2 = lax.broadcasted_iota(I32, (2 * nr, pair_w), 1)
    row2 = lax.broadcasted_iota(I32, (2 * nr, pair_w), 0)
    own = (lane2 >= ATT_DIM) == (row2 >= nr)
    low_lanes = lax.broadcasted_iota(I32, (nr, pair_w), 1) < ATT_DIM
    nt = (((1,), (1,)), ((), ()))

    def chunk(c, carry):
        g = first_chunk + s * n_chunks + c
        r0 = pl.multiple_of(c * nr, nr)
        w0 = pl.multiple_of((s * n_chunks + c) * CHUNK, CHUNK)
        qc = q_ref[pl.ds(r0, nr), :].astype(F32)
        kw = k_ref[0, pl.ds(w0, WINDOW), :]
        vw = v_ref[0, pl.ds(w0, WINDOW), :]
        scores = []
        for pair in range(ATT_HEADS // 2):
            lanes = slice(pair * pair_w, (pair + 1) * pair_w)
            qp = qc[:, lanes]
            q2 = jnp.where(own, jnp.concatenate([qp, qp], axis=0), 0.0).astype(BF16)
            scores.append(lax.dot_general(q2, kw[:, lanes], nt, preferred_element_type=F32))
        sc = jnp.concatenate(scores, axis=0) + bias_ref[...]
        sc = jnp.where((col >= HIST - CHUNK * g) & (col < valid_keys), sc, NEG_INF)
        m = jnp.max(sc, axis=-1, keepdims=True)
        p = jnp.exp(sc - m)
        inv = 1.0 / jnp.sum(p, axis=-1, keepdims=True)
        pb = p.astype(BF16)
        outs = []
        for pair in range(ATT_HEADS // 2):
            lanes = slice(pair * pair_w, (pair + 1) * pair_w)
            rows = slice(pair * 2 * nr, (pair + 1) * 2 * nr)
            o = jnp.dot(pb[rows], vw[:, lanes], preferred_element_type=F32) * inv[rows]
            outs.append(jnp.where(low_lanes, o[:nr], o[nr:]))
        a_ref[pl.ds(r0, nr), :] = jnp.concatenate(outs, axis=1).astype(BF16)
        return carry

    lax.fori_loop(0, n_chunks, chunk, 0, unroll=min(n_chunks, 4))

    rows = qm_ref.shape[0]
    for rb in range(rows // mem_rows):
        rs = slice(rb * mem_rows, (rb + 1) * mem_rows)
        scores = []
        for h in range(MEM_HEADS):
            lanes = slice(h * MEM_DIM, (h + 1) * MEM_DIM)
            scores.append(lax.dot_general(qm_ref[rs, lanes], mk_ref[0, :, lanes], nt,
                                          preferred_element_type=F32))
        sc = jnp.concatenate(scores, axis=0) * (MEM_DIM ** -0.5)
        m = jnp.max(sc, axis=-1, keepdims=True)
        p = jnp.exp(sc - m)
        inv = 1.0 / jnp.sum(p, axis=-1, keepdims=True)
        pb = p.astype(BF16)
        for h in range(MEM_HEADS):
            lanes = slice(h * MEM_DIM, (h + 1) * MEM_DIM)
            hs = slice(h * mem_rows, (h + 1) * mem_rows)
            o = jnp.dot(pb[hs], mv_ref[0, :, lanes], preferred_element_type=F32) * inv[hs]
            m_ref[rs, lanes] = o.astype(BF16)


def _attention(q, k_pad, v_pad, bias, qm, mk, mv, *, batch, seq, rows, chunk_rows,
               valid_keys, first_chunk, mem_rows):
    n_steps = seq // rows
    tok = pl.BlockSpec((rows, WIDTH), lambda b, s: (b * n_steps + s, 0))
    whole_seq = pl.BlockSpec((1, k_pad.shape[1], WIDTH), lambda b, s: (b, 0, 0))
    mem = pl.BlockSpec((1, MEM_TOKENS, WIDTH), lambda b, s: (b, 0, 0))
    kern = functools.partial(
        _attention_kernel, chunk_rows=chunk_rows, n_chunks=rows // chunk_rows,
        valid_keys=valid_keys, first_chunk=first_chunk, mem_rows=mem_rows)
    out = jax.ShapeDtypeStruct((batch * seq, WIDTH), BF16)
    return pl.pallas_call(
        kern,
        grid=(batch, n_steps),
        in_specs=[tok, whole_seq, whole_seq, _resident(bias.shape), tok, mem, mem],
        out_specs=[tok, tok],
        out_shape=[out, out],
        compiler_params=_params("arbitrary", "arbitrary"),
        name="attention",
    )(q, k_pad, v_pad, bias, qm, mk, mv)


def _merge_kernel(x_ref, a_ref, c_ref, m_ref, wg_ref, wa_ref, wc_ref, wm_ref, wo_ref,
                  g_ref, b_ref, wr2_ref, wrh_ref, br_ref,
                  y_ref, sel_ref, topi_ref, topg_ref):
    x = x_ref[...]
    xb = x.astype(BF16)
    branches = [(a_ref[...], wa_ref), (c_ref[...], wc_ref), (m_ref[...], wm_ref)]
    out = None
    n_col = 256
    for j in range(D_MODEL // n_col):
        cols = slice(j * n_col, (j + 1) * n_col)
        comb = None
        for i, (br, wbr) in enumerate(branches):
            gcols = slice(i * D_MODEL + j * n_col, i * D_MODEL + (j + 1) * n_col)
            gate = jax.nn.sigmoid(jnp.dot(xb, wg_ref[:, gcols], preferred_element_type=F32))
            term = gate * jnp.dot(br, wbr[:, cols], preferred_element_type=F32)
            comb = term if comb is None else comb + term
        part = jnp.dot(comb.astype(BF16), wo_ref[cols, :], preferred_element_type=F32)
        out = part if out is None else out + part
    y = _layer_norm(DN_ALPHA * x + out, g_ref[...], b_ref[...])
    y_ref[...] = y

    y_hi = y.astype(BF16)
    y_lo = (y - y_hi.astype(F32)).astype(BF16)
    both = jnp.dot(y_hi, wr2_ref[...], preferred_element_type=F32)
    logits = (both[:, :N_EXPERTS] + both[:, N_EXPERTS:]
              + jnp.dot(y_lo, wrh_ref[...], preferred_element_type=F32) + br_ref[...])

    rows = logits.shape[0]
    lane = lax.broadcasted_iota(I32, (rows, N_EXPERTS), 1).astype(F32)
    lane_out = lax.broadcasted_iota(I32, (rows, V7X_LANES), 1)
    work = logits
    sel = jnp.zeros((rows, N_EXPERTS), F32)
    topi = jnp.zeros((rows, V7X_LANES), I32)
    vals = []
    for k in range(TOP_K):
        top = jnp.max(work, axis=-1, keepdims=True)
        idx = jnp.min(jnp.where(work == top, lane, float(N_EXPERTS)), axis=-1, keepdims=True)
        hit = lane == idx
        sel = jnp.where(hit, 1.0, sel)
        work = jnp.where(hit, -jnp.inf, work)
        topi = jnp.where(lane_out == k, idx.astype(I32), topi)
        vals.append(top)
    exps = [jnp.exp(v - vals[0]) for v in vals]
    denom = exps[0] + exps[1] + exps[2] + exps[3]
    topg = jnp.zeros((rows, V7X_LANES), F32)
    for k in range(TOP_K):
        topg = jnp.where(lane_out == k, exps[k] / denom, topg)
    sel_ref[...] = sel.astype(BF16)
    topi_ref[...] = topi
    topg_ref[...] = topg


def _merge(x2d, a, c, m, weights, *, rows):
    n = x2d.shape[0]
    wide = pl.BlockSpec((rows, D_MODEL), lambda i: (i, 0))
    narrow = pl.BlockSpec((rows, WIDTH), lambda i: (i, 0))
    lanes = pl.BlockSpec((rows, V7X_LANES), lambda i: (i, 0))
    return pl.pallas_call(
        _merge_kernel,
        grid=(n // rows,),
        in_specs=[wide, narrow, narrow, narrow] + [_resident(w.shape) for w in weights],
        out_specs=[wide, pl.BlockSpec((rows, N_EXPERTS), lambda i: (i, 0)), lanes, lanes],
        out_shape=[jax.ShapeDtypeStruct((n, D_MODEL), F32),
                   jax.ShapeDtypeStruct((n, N_EXPERTS), BF16),
                   jax.ShapeDtypeStruct((n, V7X_LANES), I32),
                   jax.ShapeDtypeStruct((n, V7X_LANES), F32)],
        compiler_params=_params("arbitrary"),
        name="merge_router",
    )(x2d, a, c, m, *weights)


def _rank_kernel(sel_ref, topi_ref, init_ref, lpos_ref, before_ref, cnt_ref, carry_ref):
    @pl.when(pl.program_id(0) == 0)
    def _():
        carry_ref[...] = init_ref[...]

    before_ref[0] = carry_ref[...]
    sel = sel_ref[...]
    rows = sel.shape[0]
    r = lax.broadcasted_iota(I32, (rows, rows), 0)
    c = lax.broadcasted_iota(I32, (rows, rows), 1)
    earlier = jnp.where(c < r, 1.0, 0.0).astype(BF16)
    in_tile = jnp.dot(earlier, sel, preferred_element_type=F32)
    n_run = jnp.sum(sel.astype(F32), axis=0, keepdims=True)
    run_rows = jnp.ceil(n_run * (1.0 / RUN_CHUNK)) * RUN_CHUNK
    er = lax.broadcasted_iota(I32, (N_EXPERTS, N_EXPERTS), 0)
    ec = lax.broadcasted_iota(I32, (N_EXPERTS, N_EXPERTS), 1)
    lower_experts = jnp.where(er < ec, 1.0, 0.0).astype(BF16)
    local0 = jnp.dot(jnp.broadcast_to(run_rows, (8, N_EXPERTS)).astype(BF16), lower_experts,
                     preferred_element_type=F32)[0:1]
    pos = in_tile + local0
    lane = lax.broadcasted_iota(I32, (rows, N_EXPERTS), 1)
    lane_out = lax.broadcasted_iota(I32, (rows, V7X_LANES), 1)
    topi = topi_ref[...]
    lpos = jnp.full((rows, V7X_LANES), -1, I32)
    for k in range(TOP_K):
        mine = jnp.sum(jnp.where(lane == topi[:, k:k + 1], pos, 0.0), axis=-1, keepdims=True)
        lpos = jnp.where(lane_out == k, mine.astype(I32), lpos)
    lpos_ref[...] = lpos
    total = carry_ref[...] + n_run
    carry_ref[...] = total
    cnt_ref[...] = total


def _ranks(sel, topi, init, *, rows):
    n = sel.shape[0]
    small = pl.BlockSpec((8, N_EXPERTS), lambda i: (0, 0))
    lanes = pl.BlockSpec((rows, V7X_LANES), lambda i: (i, 0))
    return pl.pallas_call(
        _rank_kernel,
        grid=(n // rows,),
        in_specs=[pl.BlockSpec((rows, N_EXPERTS), lambda i: (i, 0)), lanes, small],
        out_specs=[lanes, pl.BlockSpec((1, 8, N_EXPERTS), lambda i: (i, 0, 0)), small],
        out_shape=[jax.ShapeDtypeStruct((n, V7X_LANES), I32),
                   jax.ShapeDtypeStruct((n // rows, 8, N_EXPERTS), F32),
                   jax.ShapeDtypeStruct((8, N_EXPERTS), F32)],
        scratch_shapes=[pltpu.VMEM((8, N_EXPERTS), F32)],
        compiler_params=_params("arbitrary"),
        name="expert_ranks",
    )(sel, topi, init)


SCAL_SORTED0 = 0
SCAL_CHUNKS = N_EXPERTS
SCAL_LOCAL0 = 2 * N_EXPERTS
SCAL_TOTAL = 3 * N_EXPERTS
COPY_CHUNKS = (8, 4, 2, 1)
COPY_ROWS = tuple(c * RUN_CHUNK for c in COPY_CHUNKS)
SCAL_PREV_TOTAL = SCAL_TOTAL + len(COPY_CHUNKS)


def _run_copy(local_ref, local_row, sorted_ref, sorted_row, sems, size, to_sorted):
    loc = local_ref.at[pl.ds(local_row, COPY_ROWS[size])]
    srt = sorted_ref.at[pl.ds(sorted_row, COPY_ROWS[size])]
    sem = sems[size]
    return pltpu.make_async_copy(loc, srt, sem) if to_sorted else pltpu.make_async_copy(srt, loc, sem)


def _start_runs(scal_ref, local_ref, sorted_ref, sems, *, to_sorted):
    def expert(e, carry):
        sorted0 = scal_ref[0, 0, SCAL_SORTED0 + e]
        local0 = scal_ref[0, 0, SCAL_LOCAL0 + e]
        n_chunk = scal_ref[0, 0, SCAL_CHUNKS + e]
        n_big = lax.shift_right_logical(n_chunk, COPY_CHUNKS[0].bit_length() - 1)

        def copy(size, offset):
            return _run_copy(local_ref, pl.multiple_of(local0 + offset, RUN_CHUNK), sorted_ref,
                             pl.multiple_of(sorted0 + offset, RUN_CHUNK), sems, size, to_sorted)

        def big(c, inner):
            copy(0, c * COPY_ROWS[0]).start()
            return inner

        lax.fori_loop(0, n_big, big, 0)
        offset = n_big * COPY_ROWS[0]
        for size in range(1, len(COPY_CHUNKS)):
            take = (n_chunk & COPY_CHUNKS[size]) != 0

            @pl.when(take)
            def _(size=size, offset=offset):
                copy(size, offset).start()

            offset = offset + jnp.where(take, COPY_ROWS[size], 0)
        return carry

    lax.fori_loop(0, N_EXPERTS, expert, 0)


def _wait_runs(scal_ref, total_lane, local_ref, sorted_ref, sems, *, to_sorted):
    for size in range(len(COPY_ROWS)):
        def drain(c, carry, size=size):
            _run_copy(local_ref, 0, sorted_ref, 0, sems, size, to_sorted).wait()
            return carry

        lax.fori_loop(0, scal_ref[0, 0, total_lane + size], drain, 0)


def _dispatch_kernel(scal_ref, lpos_ref, y_ref, *rest, local_rows):
    sorted_ref, stage = rest[-2 - len(COPY_CHUNKS):-len(COPY_CHUNKS)]
    sem_arrays = rest[-len(COPY_CHUNKS):]
    i = pl.program_id(0)
    slot = i % 2
    sems = [tuple(a.at[s] for a in sem_arrays) for s in (slot, 1 - slot)]
    rows = y_ref.shape[0]
    yb = y_ref[...].astype(BF16)
    lpos = lpos_ref[0]
    blk = LOCAL_BLOCK
    for rb in range(local_rows // blk):
        j = lax.broadcasted_iota(I32, (blk, rows), 0) + rb * blk
        onehot = jnp.zeros((blk, rows), F32)
        for k in range(TOP_K):
            onehot = jnp.where(j == lpos[k:k + 1], 1.0, onehot)
        onehot = onehot.astype(BF16)
        stage[slot, rb * blk:(rb + 1) * blk, :] = jnp.dot(onehot, yb, preferred_element_type=F32)
    _start_runs(scal_ref, stage.at[slot], sorted_ref, sems[0], to_sorted=True)

    @pl.when(i > 0)
    def _():
        _wait_runs(scal_ref, SCAL_PREV_TOTAL, stage.at[1 - slot], sorted_ref, sems[1],
                   to_sorted=True)

    @pl.when(i == pl.num_programs(0) - 1)
    def _():
        _wait_runs(scal_ref, SCAL_TOTAL, stage.at[slot], sorted_ref, sems[0], to_sorted=True)


def _local_rows(rows):
    worst = rows * TOP_K + N_EXPERTS * (RUN_CHUNK - 1)
    return (worst + LOCAL_BLOCK - 1) // LOCAL_BLOCK * LOCAL_BLOCK


def _dispatch(scal, lpos_t, y1, sorted_rows, *, rows, n_sorted):
    n = y1.shape[0]
    local_rows = _local_rows(rows)
    any_spec = pl.BlockSpec(memory_space=pl.ANY)
    in_specs = [pl.BlockSpec((1, 1, V7X_LANES), lambda i: (i, 0, 0), memory_space=pltpu.SMEM),
                pl.BlockSpec((1, 8, rows), lambda i: (i, 0, 0)),
                pl.BlockSpec((rows, D_MODEL), lambda i: (i, 0))]
    args = [scal, lpos_t, y1]
    aliases = {}
    if sorted_rows is not None:
        in_specs.append(any_spec)
        args.append(sorted_rows)
        aliases = {3: 0}
    return pl.pallas_call(
        functools.partial(_dispatch_kernel, local_rows=local_rows),
        grid=(n // rows,),
        in_specs=in_specs,
        out_specs=any_spec,
        out_shape=jax.ShapeDtypeStruct((n_sorted, D_MODEL), F32),
        scratch_shapes=[pltpu.VMEM((2, local_rows, D_MODEL), F32),
                        *[pltpu.SemaphoreType.DMA((2,))] * len(COPY_CHUNKS)],
        input_output_aliases=aliases,
        compiler_params=_params("arbitrary"),
        name="dispatch_rows",
    )(*args)


def _expert_kernel(te_ref, tr_ref, tb_ref, x_ref, wgu_ref, bgu_ref, wd_ref, bd_ref, o_ref,
                   wgu_b, wd_b):
    del tb_ref
    i = pl.program_id(0)
    e = te_ref[i]
    changed = (i == 0) | (e != te_ref[jnp.maximum(i - 1, 0)])

    @pl.when(changed)
    def _():
        wgu_b[...] = wgu_ref[0].astype(BF16)
        wd_b[...] = wd_ref[0].astype(BF16)

    def ffn(n_rows):
        row = lax.broadcasted_iota(I32, (n_rows, D_MODEL), 0)
        xb = jnp.where(row < tr_ref[i], x_ref[0:n_rows, :], 0.0).astype(BF16)
        bgu = bgu_ref[0]
        acc = None
        n_col = 512
        for j in range(D_FF // n_col):
            cg = slice(j * n_col, (j + 1) * n_col)
            cl = slice(D_FF + j * n_col, D_FF + (j + 1) * n_col)
            hg = jnp.dot(xb, wgu_b[:, cg], preferred_element_type=F32) + bgu[:, cg]
            hl = jnp.dot(xb, wgu_b[:, cl], preferred_element_type=F32) + bgu[:, cl]
            glu = jnp.minimum(hg, SWIGLU_LIMIT)
            lin = jnp.clip(hl, -SWIGLU_LIMIT, SWIGLU_LIMIT)
            act = glu * jax.nn.sigmoid(SWIGLU_ALPHA * glu) * (lin + 1.0)
            part = jnp.dot(act.astype(BF16), wd_b[cg, :], preferred_element_type=F32)
            acc = part if acc is None else acc + part
        o_ref[0:n_rows, :] = acc + bd_ref[0]

    half = EXPERT_TILE // 2

    @pl.when(tr_ref[i] > half)
    def _():
        ffn(EXPERT_TILE)

    @pl.when((tr_ref[i] > 0) & (tr_ref[i] <= half))
    def _():
        ffn(half)


def _experts(tile_expert, tile_rows, tile_block, xs, w_gate_up, b_gate_up, w_down, b_down):
    n_tiles = xs.shape[0] // EXPERT_TILE
    grid_spec = pltpu.PrefetchScalarGridSpec(
        num_scalar_prefetch=3,
        grid=(n_tiles,),
        in_specs=[
            pl.BlockSpec((EXPERT_TILE, D_MODEL), lambda i, te, tr, tb: (tb[i], 0)),
            pl.BlockSpec((1, D_MODEL, 2 * D_FF), lambda i, te, tr, tb: (te[i], 0, 0)),
            pl.BlockSpec((1, 1, 2 * D_FF), lambda i, te, tr, tb: (te[i], 0, 0)),
            pl.BlockSpec((1, D_FF, D_MODEL), lambda i, te, tr, tb: (te[i], 0, 0)),
            pl.BlockSpec((1, 1, D_MODEL), lambda i, te, tr, tb: (te[i], 0, 0)),
        ],
        out_specs=pl.BlockSpec((EXPERT_TILE, D_MODEL), lambda i, te, tr, tb: (tb[i], 0)),
        scratch_shapes=[pltpu.VMEM((D_MODEL, 2 * D_FF), BF16), pltpu.VMEM((D_FF, D_MODEL), BF16)],
    )
    return pl.pallas_call(
        _expert_kernel,
        grid_spec=grid_spec,
        out_shape=jax.ShapeDtypeStruct(xs.shape, F32),
        compiler_params=_params("arbitrary"),
        name="expert_ffn",
    )(tile_expert, tile_rows, tile_block, xs, w_gate_up, b_gate_up, w_down, b_down)


def _combine_kernel(scal_ref, next_scal_ref, lpos_ref, gate_ref, y_ref, g_ref, b_ref, eo_ref,
                    out_ref, buf, *sem_arrays, local_rows):
    i = pl.program_id(0)
    slot = i % 2
    sems = [tuple(a.at[s] for a in sem_arrays) for s in (slot, 1 - slot)]

    @pl.when(i == 0)
    def _():
        buf[...] = jnp.zeros(buf.shape, buf.dtype)
        _start_runs(scal_ref, buf.at[0], eo_ref, tuple(a.at[0] for a in sem_arrays), to_sorted=False)

    @pl.when(i + 1 < pl.num_programs(0))
    def _():
        _start_runs(next_scal_ref, buf.at[1 - slot], eo_ref, sems[1], to_sorted=False)

    _wait_runs(scal_ref, SCAL_TOTAL, buf.at[slot], eo_ref, sems[0], to_sorted=False)
    rows = y_ref.shape[0]
    lpos = lpos_ref[...]
    gate = gate_ref[...]
    blk = LOCAL_BLOCK
    lane = lax.broadcasted_iota(I32, (rows, V7X_LANES), 1)
    slot_row = [jnp.broadcast_to(lpos[:, k:k + 1], (rows, V7X_LANES)) for k in range(TOP_K)]
    slot_gate = [jnp.broadcast_to(gate[:, k:k + 1], (rows, V7X_LANES)) for k in range(TOP_K)]
    moe = None
    for kb in range(local_rows // blk):
        pieces = []
        for jb in range(blk // V7X_LANES):
            col = lane + (kb * blk + jb * V7X_LANES)
            w = jnp.zeros((rows, V7X_LANES), F32)
            for k in range(TOP_K):
                w = jnp.where(col == slot_row[k], slot_gate[k], w)
            pieces.append(w)
        weights = jnp.concatenate(pieces, axis=1).astype(BF16)
        part = jnp.dot(weights, buf[slot, kb * blk:(kb + 1) * blk, :].astype(BF16),
                       preferred_element_type=F32)
        moe = part if moe is None else moe + part
    out_ref[...] = _layer_norm(DN_ALPHA * y_ref[...] + moe, g_ref[...], b_ref[...])


def _combine(scal, lpos, gates, y1, ln_g, ln_b, expert_out, *, rows):
    n = y1.shape[0]
    local_rows = _local_rows(rows)
    wide = pl.BlockSpec((rows, D_MODEL), lambda i: (i, 0))
    lanes = pl.BlockSpec((rows, V7X_LANES), lambda i: (i, 0))
    n_steps = n // rows
    scalars = pl.BlockSpec((1, 1, V7X_LANES), lambda i: (i, 0, 0), memory_space=pltpu.SMEM)
    next_scalars = pl.BlockSpec((1, 1, V7X_LANES), lambda i: (jnp.minimum(i + 1, n_steps - 1), 0, 0),
                                memory_space=pltpu.SMEM)
    return pl.pallas_call(
        functools.partial(_combine_kernel, local_rows=local_rows),
        grid=(n_steps,),
        in_specs=[scalars, next_scalars, lanes, lanes, wide, _resident(ln_g.shape),
                  _resident(ln_b.shape), pl.BlockSpec(memory_space=pl.ANY)],
        out_specs=wide,
        out_shape=jax.ShapeDtypeStruct((n, D_MODEL), F32),
        scratch_shapes=[pltpu.VMEM((2, local_rows, D_MODEL), F32),
                        *[pltpu.SemaphoreType.DMA((2,))] * len(COPY_CHUNKS)],
        compiler_params=_params("arbitrary"),
        name="combine_norm",
    )(scal, scal, lpos, gates, y1, ln_g, ln_b, expert_out)


def _tile_tables(lpos, n_chunk, sorted0, rows):
    n_tiles = lpos.shape[0] // rows
    local0 = (jnp.cumsum(n_chunk, axis=1) - n_chunk) * RUN_CHUNK
    per_size = [n_chunk // COPY_CHUNKS[0]] + [(n_chunk // c) % 2 for c in COPY_CHUNKS[1:]]
    total = jnp.stack([jnp.sum(n, axis=1) for n in per_size], axis=1)
    prev_total = jnp.concatenate([jnp.zeros((1, len(COPY_CHUNKS)), I32), total[:-1]], axis=0)
    pad = jnp.zeros((n_tiles, V7X_LANES - SCAL_PREV_TOTAL - len(COPY_CHUNKS)), I32)
    scal = jnp.concatenate([sorted0, n_chunk, local0, total, prev_total, pad], axis=1)
    lpos_t = jnp.transpose(lpos[:, :8].reshape(n_tiles, rows, 8), (0, 2, 1))
    return scal.reshape(n_tiles, 1, V7X_LANES), lpos_t


def kernel(x_prompt, x_sample, mem_prompt, cache_attn_k, cache_attn_v, cache_conv, cache_mem_k,
           cache_mem_v, w_in, rel_bias, conv_w, w_mem_kv, w_gate, w_br_attn, w_br_conv, w_br_mem,
           w_out, ln1_g, ln1_b, w_router, b_router, w_gate_up, b_gate_up, w_down, b_down,
           ln2_g, ln2_b):
    depth = w_in.shape[0]
    assert depth == 1, "one layer only"
    batch, seq, _ = x_prompt.shape
    dec_batch, dec_seq, _ = x_sample.shape
    n_prompt = batch * seq
    n_sample = dec_batch * dec_seq
    assert seq % ROW_TILE == 0 and HIST == ROW_TILE and cache_attn_k.shape[2] == HIST
    assert dec_seq % 16 == 0 and dec_seq <= CHUNK and cache_conv.shape[2] == 2

    w_in_b = w_in[0].astype(BF16)
    w_mem_b = w_mem_kv[0].astype(BF16)
    wr = w_router[0]
    wr_hi = wr.astype(BF16)
    wr_lo = (wr - wr_hi.astype(F32)).astype(BF16)
    merge_w = (w_gate[0].astype(BF16), w_br_attn[0].astype(BF16), w_br_conv[0].astype(BF16),
               w_br_mem[0].astype(BF16), w_out[0].astype(BF16), ln1_g, ln1_b,
               jnp.concatenate([wr_hi, wr_lo], axis=1), wr_hi, b_router)
    conv_w8 = jnp.pad(conv_w[0], ((0, 5), (0, 0)))
    bias = _band_bias(rel_bias[0])

    xp = x_prompt.reshape(n_prompt, D_MODEL)
    mk_f, mv_f, mk_b, mv_b = _memory_kv(mem_prompt.reshape(batch * MEM_TOKENS, D_MODEL), w_mem_b)
    q, k_pad, v_pad, c, qm, k_tail, v_tail, u_tail = _proj_in(
        xp, w_in_b, conv_w8, jnp.zeros((batch, 8, WIDTH), F32),
        batch=batch, seq=seq, rows=ROW_TILE, pad_steps=1)
    a, m = _attention(
        q, k_pad, v_pad, bias.reshape(ATT_HEADS * CHUNK, WINDOW), qm,
        mk_b.reshape(batch, MEM_TOKENS, WIDTH),
        mv_b.reshape(batch, MEM_TOKENS, WIDTH), batch=batch, seq=seq, rows=ROW_TILE,
        chunk_rows=CHUNK, valid_keys=BAND, first_chunk=0, mem_rows=256)
    y1_p, sel_p, topi_p, topg_p = _merge(xp, a, c, m, merge_w, rows=MERGE_TILE)

    xs = x_sample.reshape(n_sample, D_MODEL)
    conv_init = jnp.pad(cache_conv[0], ((0, 0), (6, 0), (0, 0)))
    q_s, k_s, v_s, c_s, qm_s, k_new, v_new, u_tail_s = _proj_in(
        xs, w_in_b, conv_w8, conv_init, batch=dec_batch, seq=dec_seq, rows=dec_seq, pad_steps=0)
    pad_rows = WINDOW - HIST - dec_seq

    def window(cache, new):
        cache = cache[0].reshape(dec_batch, HIST, WIDTH).astype(BF16)
        return jnp.pad(jnp.concatenate([cache, new], axis=1), ((0, 0), (0, pad_rows), (0, 0)))

    a_s, m_s = _attention(
        q_s, window(cache_attn_k, k_s), window(cache_attn_v, v_s),
        bias[:, :dec_seq].reshape(ATT_HEADS * dec_seq, WINDOW), qm_s,
        cache_mem_k[0].reshape(dec_batch, MEM_TOKENS, WIDTH).astype(BF16),
        cache_mem_v[0].reshape(dec_batch, MEM_TOKENS, WIDTH).astype(BF16),
        batch=dec_batch, seq=dec_seq, rows=dec_seq, chunk_rows=dec_seq,
        valid_keys=HIST + dec_seq, first_chunk=LEFT_CHUNKS, mem_rows=dec_seq)
    y1_s, sel_s, topi_s, topg_s = _merge(xs, a_s, c_s, m_s, merge_w, rows=n_sample)

    lpos_p, before_p, cnt_p = _ranks(sel_p, topi_p, jnp.zeros((8, N_EXPERTS), F32), rows=MOE_TILE)
    lpos_s, before_s, cnt = _ranks(sel_s, topi_s, cnt_p, rows=n_sample)
    before = jnp.concatenate([before_p[:, 0], before_s[:, 0]], axis=0).astype(I32)
    n_run = jnp.concatenate([before[1:], cnt[:1].astype(I32)], axis=0) - before
    n_chunk = (n_run + RUN_CHUNK - 1) // RUN_CHUNK
    run_rows = n_chunk * RUN_CHUNK
    counts = jnp.sum(run_rows, axis=0)
    padded = (counts + EXPERT_TILE - 1) // EXPERT_TILE * EXPERT_TILE
    ends = jnp.cumsum(padded)
    starts = ends - padded
    sorted0 = starts[None, :] + jnp.cumsum(run_rows, axis=0) - run_rows
    n_token_tiles = n_prompt // MOE_TILE + 1
    n_tiles = ((n_prompt + n_sample) * TOP_K
               + n_token_tiles * N_EXPERTS * (RUN_CHUNK - 1)) // EXPERT_TILE + N_EXPERTS
    tile_start = jnp.arange(n_tiles, dtype=I32) * EXPERT_TILE
    tile_expert = jnp.minimum(
        jnp.sum((ends[None, :] <= tile_start[:, None]).astype(I32), axis=1), N_EXPERTS - 1)
    tile_rows = jnp.clip(starts[tile_expert] + counts[tile_expert] - tile_start, 0, EXPERT_TILE)
    last_used = jnp.maximum(ends[-1] // EXPERT_TILE - 1, 0)
    tile_block = jnp.minimum(jnp.arange(n_tiles, dtype=I32), last_used)
    tile_expert = jnp.where(jnp.arange(n_tiles) > last_used, tile_expert[last_used], tile_expert)
    scal_p, lpos_tp = _tile_tables(lpos_p, n_chunk[:-1], sorted0[:-1], MOE_TILE)
    scal_s, lpos_ts = _tile_tables(lpos_s, n_chunk[-1:], sorted0[-1:], n_sample)

    n_sorted = n_tiles * EXPERT_TILE
    sorted_rows = _dispatch(scal_p, lpos_tp, y1_p, None, rows=MOE_TILE, n_sorted=n_sorted)
    sorted_rows = _dispatch(scal_s, lpos_ts, y1_s, sorted_rows, rows=n_sample, n_sorted=n_sorted)
    expert_out = _experts(tile_expert, tile_rows, tile_block, sorted_rows, w_gate_up[0],
                          b_gate_up[0].reshape(N_EXPERTS, 1, 2 * D_FF), w_down[0],
                          b_down[0].reshape(N_EXPERTS, 1, D_MODEL))
    y_p = _combine(scal_p, lpos_p, topg_p, y1_p, ln2_g, ln2_b, expert_out, rows=MOE_TILE)
    y_s = _combine(scal_s, lpos_s, topg_s, y1_s, ln2_g, ln2_b, expert_out, rows=n_sample)

    def heads(t, n_b, n_rows, n_heads, dim):
        return t.reshape(1, n_b, n_rows, n_heads, dim)

    return (y_p.reshape(batch, seq, D_MODEL),
            y_s.reshape(dec_batch, dec_seq, D_MODEL),
            heads(k_tail, batch, HIST, ATT_HEADS, ATT_DIM),
            heads(v_tail, batch, HIST, ATT_HEADS, ATT_DIM),
            u_tail[:, 6:8].reshape(1, batch, 2, WIDTH),
            heads(mk_f, batch, MEM_TOKENS, MEM_HEADS, MEM_DIM),
            heads(mv_f, batch, MEM_TOKENS, MEM_HEADS, MEM_DIM),
            heads(k_new, dec_batch, dec_seq, ATT_HEADS, ATT_DIM),
            heads(v_new, dec_batch, dec_seq, ATT_HEADS, ATT_DIM),
            u_tail_s[:, 6:8].reshape(1, dec_batch, 2, WIDTH))
```

```python
import functools

import jax
import jax.numpy as jnp
from jax import lax
from jax.experimental import pallas as pl
from jax.experimental.pallas import tpu as pltpu

F32 = jnp.float32
BF16 = jnp.bfloat16
I32 = jnp.int32

D_MODEL = 1024
CHUNK = 64
LEFT_CHUNKS = 8
HIST = LEFT_CHUNKS * CHUNK
BAND = HIST + CHUNK
ATT_HEADS = 8
ATT_DIM = 64
REL_MAX = 128
WIDTH = 512
MEM_TOKENS = 256
MEM_HEADS = 4
MEM_DIM = 128
N_EXPERTS = 32
TOP_K = 4
D_FF = 1024
SWIGLU_LIMIT = 7.0
SWIGLU_ALPHA = 1.702
DN_ALPHA = 2.0 ** 0.25
LN_EPS = 1e-5
NEG_INF = -1e30

V7X_LANES = 128
V7X_VMEM_BYTES = 64 * 1024 * 1024
VMEM_LIMIT = V7X_VMEM_BYTES - 8 * 1024 * 1024

ROW_TILE = 512
MERGE_TILE = 1024
MOE_TILE = 512
WINDOW = BAND + CHUNK
EXPERT_TILE = 1024
RUN_CHUNK = 8
LOCAL_BLOCK = 256


def _params(*sem):
    return pltpu.CompilerParams(dimension_semantics=sem, vmem_limit_bytes=VMEM_LIMIT)


def _resident(shape):
    nd = len(shape)
    return pl.BlockSpec(shape, lambda *_: (0,) * nd, pipeline_mode=pl.Buffered(1))


def _layer_norm(r, g, b):
    mu = jnp.mean(r, axis=-1, keepdims=True)
    d = r - mu
    var = jnp.mean(d * d, axis=-1, keepdims=True)
    return d * lax.rsqrt(var + LN_EPS) * g + b


def _memkv_kernel(x_ref, w_ref, kf_ref, vf_ref, kb_ref, vb_ref):
    y = jnp.dot(x_ref[...].astype(BF16), w_ref[...], preferred_element_type=F32)
    k = y[:, :WIDTH]
    v = y[:, WIDTH:]
    kf_ref[...] = k
    vf_ref[...] = v
    kb_ref[...] = k.astype(BF16)
    vb_ref[...] = v.astype(BF16)


def _memory_kv(mem2d, w_b):
    rows = mem2d.shape[0]
    tile = pl.BlockSpec((ROW_TILE, D_MODEL), lambda i: (i, 0))
    half = pl.BlockSpec((ROW_TILE, WIDTH), lambda i: (i, 0))
    return pl.pallas_call(
        _memkv_kernel,
        grid=(rows // ROW_TILE,),
        in_specs=[tile, _resident((D_MODEL, 2 * WIDTH))],
        out_specs=[half, half, half, half],
        out_shape=[jax.ShapeDtypeStruct((rows, WIDTH), F32)] * 2
        + [jax.ShapeDtypeStruct((rows, WIDTH), BF16)] * 2,
        compiler_params=_params("arbitrary"),
        name="memory_kv",
    )(mem2d, w_b)


def _proj_in_kernel(x_ref, w_ref, cw_ref, cinit_ref,
                    q_ref, k_ref, v_ref, c_ref, qm_ref, kt_ref, vt_ref, ut_ref,
                    carry_ref, *, pad_steps):
    s = pl.program_id(1)
    ns = pl.num_programs(1)
    rows = x_ref.shape[0]
    first = pad_steps
    last = ns - 1 - pad_steps

    if pad_steps:
        @pl.when((s < first) | (s > last))
        def _():
            k_ref[...] = jnp.zeros(k_ref.shape, k_ref.dtype)
            v_ref[...] = jnp.zeros(v_ref.shape, v_ref.dtype)

    @pl.when((s >= first) & (s <= last))
    def _():
        @pl.when(s == first)
        def _():
            carry_ref[...] = cinit_ref[0]

        xb = x_ref[...].astype(BF16)

        def proj(g, n=1):
            y = jnp.dot(xb, w_ref[:, g * WIDTH:(g + n) * WIDTH], preferred_element_type=F32)
            return y if n == 1 else [y[:, j * WIDTH:(j + 1) * WIDTH] for j in range(n)]

        q, k = proj(0, 2)
        q_ref[...] = (q * (ATT_DIM ** -0.5)).astype(BF16)
        k_ref[0] = k.astype(BF16)
        v, bg = proj(2, 2)
        v_ref[0] = v.astype(BF16)

        @pl.when(s == last)
        def _():
            kt_ref[0] = k
            vt_ref[0] = v

        cg, hc = proj(4, 2)
        u = cg * hc
        prev = carry_ref[...]
        row = lax.broadcasted_iota(I32, u.shape, 0)
        u1 = jnp.where(row == 0, prev[7:8], pltpu.roll(u, 1, 0))
        u2 = jnp.where(row == 0, prev[6:7], jnp.where(row == 1, prev[7:8], pltpu.roll(u, 2, 0)))
        cw = cw_ref[...]
        c_ref[...] = (bg * (cw[0:1] * u2 + cw[1:2] * u1 + cw[2:3] * u)).astype(BF16)
        tail = u[rows - 8:]
        carry_ref[...] = tail

        @pl.when(s == last)
        def _():
            ut_ref[0] = tail

        qm_ref[...] = proj(6).astype(BF16)


def _proj_in(x2d, w_b, conv_w8, conv_init, *, batch, seq, rows, pad_steps):
    n_data = seq // rows
    n_steps = n_data + 2 * pad_steps

    def data_idx(b, s):
        return (b * n_data + jnp.clip(s - pad_steps, 0, n_data - 1), 0)

    wide = pl.BlockSpec((rows, D_MODEL), data_idx)
    narrow = pl.BlockSpec((rows, WIDTH), data_idx)
    seq_blk = pl.BlockSpec((1, rows, WIDTH), lambda b, s: (b, s, 0))
    tail_blk = pl.BlockSpec((1, rows, WIDTH), lambda b, s: (b, 0, 0))
    tail8 = pl.BlockSpec((1, 8, WIDTH), lambda b, s: (b, 0, 0))
    tok = jax.ShapeDtypeStruct((batch * seq, WIDTH), BF16)
    kv = jax.ShapeDtypeStruct((batch, n_steps * rows, WIDTH), BF16)
    tail = jax.ShapeDtypeStruct((batch, rows, WIDTH), F32)
    return pl.pallas_call(
        functools.partial(_proj_in_kernel, pad_steps=pad_steps),
        grid=(batch, n_steps),
        in_specs=[wide, _resident(w_b.shape), _resident(conv_w8.shape), tail8],
        out_specs=[narrow, seq_blk, seq_blk, narrow, narrow, tail_blk, tail_blk, tail8],
        out_shape=[tok, kv, kv, tok, tok, tail, tail,
                   jax.ShapeDtypeStruct((batch, 8, WIDTH), F32)],
        scratch_shapes=[pltpu.VMEM((8, WIDTH), F32)],
        compiler_params=_params("arbitrary", "arbitrary"),
        name="proj_in",
    )(x2d, w_b, conv_w8, conv_init)


def _bias_kernel(tab_ref, out_ref):
    shape = (CHUNK, WINDOW)
    i = lax.broadcasted_iota(I32, shape, 0)
    j = lax.broadcasted_iota(I32, shape, 1)
    idx = jnp.clip(HIST + i - j, -REL_MAX, REL_MAX) + REL_MAX
    for h in range(ATT_HEADS):
        def body(d, acc, h=h):
            return jnp.where(idx == d, tab_ref[h, d], acc)
        out_ref[h] = lax.fori_loop(0, 2 * REL_MAX + 1, body, jnp.zeros(shape, F32))


def _band_bias(rel_table):
    return pl.pallas_call(
        _bias_kernel,
        in_specs=[pl.BlockSpec(memory_space=pltpu.SMEM)],
        out_shape=jax.ShapeDtypeStruct((ATT_HEADS, CHUNK, WINDOW), F32),
        name="band_bias",
    )(rel_table)


def _attention_kernel(q_ref, k_ref, v_ref, bias_ref, qm_ref, mk_ref, mv_ref,
                      a_ref, m_ref, *, chunk_rows, n_chunks, valid_keys, first_chunk, mem_rows):
    s = pl.program_id(1)
    nr = chunk_rows
    pair_w = 2 * ATT_DIM
    col = lax.broadcasted_iota(I32, (ATT_HEADS * nr, WINDOW), 1)
    lane2 = lax.broadcasted_iota(I32, (2 * nr, pair_w), 1)
    row2 = lax.broadcasted_iota(I32, (2 * nr, pair_w), 0)
    own = (lane2 >= ATT_DIM) == (row2 >= nr)
    low_lanes = lax.broadcasted_iota(I32, (nr, pair_w), 1) < ATT_DIM
    nt = (((1,), (1,)), ((), ()))

    def chunk(c, carry):
        g = first_chunk + s * n_chunks + c
        r0 = pl.multiple_of(c * nr, nr)
        w0 = pl.multiple_of((s * n_chunks + c) * CHUNK, CHUNK)
        qc = q_ref[pl.ds(r0, nr), :].astype(F32)
        kw = k_ref[0, pl.ds(w0, WINDOW), :]
        vw = v_ref[0, pl.ds(w0, WINDOW), :]
        scores = []
        for pair in range(ATT_HEADS // 2):
            lanes = slice(pair * pair_w, (pair + 1) * pair_w)
            qp = qc[:, lanes]
            q2 = jnp.where(own, jnp.concatenate([qp, qp], axis=0), 0.0).astype(BF16)
            scores.append(lax.dot_general(q2, kw[:, lanes], nt, preferred_element_type=F32))
        sc = jnp.concatenate(scores, axis=0) + bias_ref[...]
        sc = jnp.where((col >= HIST - CHUNK * g) & (col < valid_keys), sc, NEG_INF)
        m = jnp.max(sc, axis=-1, keepdims=True)
        p = jnp.exp(sc - m)
        inv = 1.0 / jnp.sum(p, axis=-1, keepdims=True)
        pb = p.astype(BF16)
        outs = []
        for pair in range(ATT_HEADS // 2):
            lanes = slice(pair * pair_w, (pair + 1) * pair_w)
            rows = slice(pair * 2 * nr, (pair + 1) * 2 * nr)
            o = jnp.dot(pb[rows], vw[:, lanes], preferred_element_type=F32) * inv[rows]
            outs.append(jnp.where(low_lanes, o[:nr], o[nr:]))
        a_ref[pl.ds(r0, nr), :] = jnp.concatenate(outs, axis=1).astype(BF16)
        return carry

    lax.fori_loop(0, n_chunks, chunk, 0, unroll=min(n_chunks, 4))

    rows = qm_ref.shape[0]
    for rb in range(rows // mem_rows):
        rs = slice(rb * mem_rows, (rb + 1) * mem_rows)
        scores = []
        for h in range(MEM_HEADS):
            lanes = slice(h * MEM_DIM, (h + 1) * MEM_DIM)
            scores.append(lax.dot_general(qm_ref[rs, lanes], mk_ref[0, :, lanes], nt,
                                          preferred_element_type=F32))
        sc = jnp.concatenate(scores, axis=0) * (MEM_DIM ** -0.5)
        m = jnp.max(sc, axis=-1, keepdims=True)
        p = jnp.exp(sc - m)
        inv = 1.0 / jnp.sum(p, axis=-1, keepdims=True)
        pb = p.astype(BF16)
        for h in range(MEM_HEADS):
            lanes = slice(h * MEM_DIM, (h + 1) * MEM_DIM)
            hs = slice(h * mem_rows, (h + 1) * mem_rows)
            o = jnp.dot(pb[hs], mv_ref[0, :, lanes], preferred_element_type=F32) * inv[hs]
            m_ref[rs, lanes] = o.astype(BF16)


def _attention(q, k_pad, v_pad, bias, qm, mk, mv, *, batch, seq, rows, chunk_rows,
               valid_keys, first_chunk, mem_rows):
    n_steps = seq // rows
    tok = pl.BlockSpec((rows, WIDTH), lambda b, s: (b * n_steps + s, 0))
    whole_seq = pl.BlockSpec((1, k_pad.shape[1], WIDTH), lambda b, s: (b, 0, 0))
    mem = pl.BlockSpec((1, MEM_TOKENS, WIDTH), lambda b, s: (b, 0, 0))
    kern = functools.partial(
        _attention_kernel, chunk_rows=chunk_rows, n_chunks=rows // chunk_rows,
        valid_keys=valid_keys, first_chunk=first_chunk, mem_rows=mem_rows)
    out = jax.ShapeDtypeStruct((batch * seq, WIDTH), BF16)
    return pl.pallas_call(
        kern,
        grid=(batch, n_steps),
        in_specs=[tok, whole_seq, whole_seq, _resident(bias.shape), tok, mem, mem],
        out_specs=[tok, tok],
        out_shape=[out, out],
        compiler_params=_params("arbitrary", "arbitrary"),
        name="attention",
    )(q, k_pad, v_pad, bias, qm, mk, mv)


def _merge_kernel(x_ref, a_ref, c_ref, m_ref, wg_ref, wa_ref, wc_ref, wm_ref, wo_ref,
                  g_ref, b_ref, wr2_ref, wrh_ref, br_ref,
                  y_ref, sel_ref, topi_ref, topg_ref):
    x = x_ref[...]
    xb = x.astype(BF16)
    branches = [(a_ref[...], wa_ref), (c_ref[...], wc_ref), (m_ref[...], wm_ref)]
    out = None
    n_col = 512
    for j in range(D_MODEL // n_col):
        cols = slice(j * n_col, (j + 1) * n_col)
        comb = None
        for i, (br, wbr) in enumerate(branches):
            gcols = slice(i * D_MODEL + j * n_col, i * D_MODEL + (j + 1) * n_col)
            gate = jax.nn.sigmoid(jnp.dot(xb, wg_ref[:, gcols], preferred_element_type=F32))
            term = gate * jnp.dot(br, wbr[:, cols], preferred_element_type=F32)
            comb = term if comb is None else comb + term
        part = jnp.dot(comb.astype(BF16), wo_ref[cols, :], preferred_element_type=F32)
        out = part if out is None else out + part
    y = _layer_norm(DN_ALPHA * x + out, g_ref[...], b_ref[...])
    y_ref[...] = y

    y_hi = y.astype(BF16)
    y_lo = (y - y_hi.astype(F32)).astype(BF16)
    both = jnp.dot(y_hi, wr2_ref[...], preferred_element_type=F32)
    logits = (both[:, :N_EXPERTS] + both[:, N_EXPERTS:]
              + jnp.dot(y_lo, wrh_ref[...], preferred_element_type=F32) + br_ref[...])

    rows = logits.shape[0]
    lane = lax.broadcasted_iota(I32, (rows, N_EXPERTS), 1).astype(F32)
    lane_out = lax.broadcasted_iota(I32, (rows, V7X_LANES), 1)
    work = logits
    sel = jnp.zeros((rows, N_EXPERTS), F32)
    topi = jnp.zeros((rows, V7X_LANES), I32)
    vals = []
    for k in range(TOP_K):
        top = jnp.max(work, axis=-1, keepdims=True)
        idx = jnp.min(jnp.where(work == top, lane, float(N_EXPERTS)), axis=-1, keepdims=True)
        hit = lane == idx
        sel = jnp.where(hit, 1.0, sel)
        work = jnp.where(hit, -jnp.inf, work)
        topi = jnp.where(lane_out == k, idx.astype(I32), topi)
        vals.append(top)
    exps = [jnp.exp(v - vals[0]) for v in vals]
    denom = exps[0] + exps[1] + exps[2] + exps[3]
    topg = jnp.zeros((rows, V7X_LANES), F32)
    for k in range(TOP_K):
        topg = jnp.where(lane_out == k, exps[k] / denom, topg)
    sel_ref[...] = sel.astype(BF16)
    topi_ref[...] = topi
    topg_ref[...] = topg


def _merge(x2d, a, c, m, weights, *, rows):
    n = x2d.shape[0]
    wide = pl.BlockSpec((rows, D_MODEL), lambda i: (i, 0))
    narrow = pl.BlockSpec((rows, WIDTH), lambda i: (i, 0))
    lanes = pl.BlockSpec((rows, V7X_LANES), lambda i: (i, 0))
    return pl.pallas_call(
        _merge_kernel,
        grid=(n // rows,),
        in_specs=[wide, narrow, narrow, narrow] + [_resident(w.shape) for w in weights],
        out_specs=[wide, pl.BlockSpec((rows, N_EXPERTS), lambda i: (i, 0)), lanes, lanes],
        out_shape=[jax.ShapeDtypeStruct((n, D_MODEL), F32),
                   jax.ShapeDtypeStruct((n, N_EXPERTS), BF16),
                   jax.ShapeDtypeStruct((n, V7X_LANES), I32),
                   jax.ShapeDtypeStruct((n, V7X_LANES), F32)],
        compiler_params=_params("arbitrary"),
        name="merge_router",
    )(x2d, a, c, m, *weights)


def _rank_kernel(sel_ref, topi_ref, init_ref, lpos_ref, before_ref, cnt_ref, carry_ref):
    @pl.when(pl.program_id(0) == 0)
    def _():
        carry_ref[...] = init_ref[...]

    before_ref[0] = carry_ref[...]
    sel = sel_ref[...]
    rows = sel.shape[0]
    r = lax.broadcasted_iota(I32, (rows, rows), 0)
    c = lax.broadcasted_iota(I32, (rows, rows), 1)
    earlier = jnp.where(c < r, 1.0, 0.0).astype(BF16)
    in_tile = jnp.dot(earlier, sel, preferred_element_type=F32)
    n_run = jnp.sum(sel.astype(F32), axis=0, keepdims=True)
    run_rows = jnp.ceil(n_run * (1.0 / RUN_CHUNK)) * RUN_CHUNK
    er = lax.broadcasted_iota(I32, (N_EXPERTS, N_EXPERTS), 0)
    ec = lax.broadcasted_iota(I32, (N_EXPERTS, N_EXPERTS), 1)
    lower_experts = jnp.where(er < ec, 1.0, 0.0).astype(BF16)
    local0 = jnp.dot(jnp.broadcast_to(run_rows, (8, N_EXPERTS)).astype(BF16), lower_experts,
                     preferred_element_type=F32)[0:1]
    pos = in_tile + local0
    lane = lax.broadcasted_iota(I32, (rows, N_EXPERTS), 1)
    lane_out = lax.broadcasted_iota(I32, (rows, V7X_LANES), 1)
    topi = topi_ref[...]
    lpos = jnp.full((rows, V7X_LANES), -1, I32)
    for k in range(TOP_K):
        mine = jnp.sum(jnp.where(lane == topi[:, k:k + 1], pos, 0.0), axis=-1, keepdims=True)
        lpos = jnp.where(lane_out == k, mine.astype(I32), lpos)
    lpos_ref[...] = lpos
    total = carry_ref[...] + n_run
    carry_ref[...] = total
    cnt_ref[...] = total


def _ranks(sel, topi, init, *, rows):
    n = sel.shape[0]
    small = pl.BlockSpec((8, N_EXPERTS), lambda i: (0, 0))
    lanes = pl.BlockSpec((rows, V7X_LANES), lambda i: (i, 0))
    return pl.pallas_call(
        _rank_kernel,
        grid=(n // rows,),
        in_specs=[pl.BlockSpec((rows, N_EXPERTS), lambda i: (i, 0)), lanes, small],
        out_specs=[lanes, pl.BlockSpec((1, 8, N_EXPERTS), lambda i: (i, 0, 0)), small],
        out_shape=[jax.ShapeDtypeStruct((n, V7X_LANES), I32),
                   jax.ShapeDtypeStruct((n // rows, 8, N_EXPERTS), F32),
                   jax.ShapeDtypeStruct((8, N_EXPERTS), F32)],
        scratch_shapes=[pltpu.VMEM((8, N_EXPERTS), F32)],
        compiler_params=_params("arbitrary"),
        name="expert_ranks",
    )(sel, topi, init)


SCAL_SORTED0 = 0
SCAL_CHUNKS = N_EXPERTS
SCAL_LOCAL0 = 2 * N_EXPERTS
SCAL_TOTAL = 3 * N_EXPERTS
COPY_CHUNKS = (8, 4, 2, 1)
COPY_ROWS = tuple(c * RUN_CHUNK for c in COPY_CHUNKS)
SCAL_PREV_TOTAL = SCAL_TOTAL + len(COPY_CHUNKS)


def _run_copy(local_ref, local_row, sorted_ref, sorted_row, sems, size, to_sorted):
    loc = local_ref.at[pl.ds(local_row, COPY_ROWS[size])]
    srt = sorted_ref.at[pl.ds(sorted_row, COPY_ROWS[size])]
    sem = sems[size]
    return pltpu.make_async_copy(loc, srt, sem) if to_sorted else pltpu.make_async_copy(srt, loc, sem)


def _start_runs(scal_ref, local_ref, sorted_ref, sems, *, to_sorted):
    def expert(e, carry):
        sorted0 = scal_ref[0, 0, SCAL_SORTED0 + e]
        local0 = scal_ref[0, 0, SCAL_LOCAL0 + e]
        n_chunk = scal_ref[0, 0, SCAL_CHUNKS + e]
        n_big = lax.shift_right_logical(n_chunk, COPY_CHUNKS[0].bit_length() - 1)

        def copy(size, offset):
            return _run_copy(local_ref, pl.multiple_of(local0 + offset, RUN_CHUNK), sorted_ref,
                             pl.multiple_of(sorted0 + offset, RUN_CHUNK), sems, size, to_sorted)

        def big(c, inner):
            copy(0, c * COPY_ROWS[0]).start()
            return inner

        lax.fori_loop(0, n_big, big, 0)
        offset = n_big * COPY_ROWS[0]
        for size in range(1, len(COPY_CHUNKS)):
            take = (n_chunk & COPY_CHUNKS[size]) != 0

            @pl.when(take)
            def _(size=size, offset=offset):
                copy(size, offset).start()

            offset = offset + jnp.where(take, COPY_ROWS[size], 0)
        return carry

    lax.fori_loop(0, N_EXPERTS, expert, 0)


def _wait_runs(scal_ref, total_lane, local_ref, sorted_ref, sems, *, to_sorted):
    for size in range(len(COPY_ROWS)):
        def drain(c, carry, size=size):
            _run_copy(local_ref, 0, sorted_ref, 0, sems, size, to_sorted).wait()
            return carry

        lax.fori_loop(0, scal_ref[0, 0, total_lane + size], drain, 0)


def _dispatch_kernel(scal_ref, lpos_ref, y_ref, *rest, local_rows):
    sorted_ref, stage = rest[-2 - len(COPY_CHUNKS):-len(COPY_CHUNKS)]
    sem_arrays = rest[-len(COPY_CHUNKS):]
    i = pl.program_id(0)
    slot = i % 2
    sems = [tuple(a.at[s] for a in sem_arrays) for s in (slot, 1 - slot)]
    rows = y_ref.shape[0]
    yb = y_ref[...].astype(BF16)
    lpos = lpos_ref[0]
    blk = LOCAL_BLOCK
    for rb in range(local_rows // blk):
        j = lax.broadcasted_iota(I32, (blk, rows), 0) + rb * blk
        onehot = jnp.zeros((blk, rows), F32)
        for k in range(TOP_K):
            onehot = jnp.where(j == lpos[k:k + 1], 1.0, onehot)
        onehot = onehot.astype(BF16)
        stage[slot, rb * blk:(rb + 1) * blk, :] = jnp.dot(onehot, yb, preferred_element_type=F32)
    _start_runs(scal_ref, stage.at[slot], sorted_ref, sems[0], to_sorted=True)

    @pl.when(i > 0)
    def _():
        _wait_runs(scal_ref, SCAL_PREV_TOTAL, stage.at[1 - slot], sorted_ref, sems[1],
                   to_sorted=True)

    @pl.when(i == pl.num_programs(0) - 1)
    def _():
        _wait_runs(scal_ref, SCAL_TOTAL, stage.at[slot], sorted_ref, sems[0], to_sorted=True)


def _local_rows(rows):
    worst = rows * TOP_K + N_EXPERTS * (RUN_CHUNK - 1)
    return (worst + LOCAL_BLOCK - 1) // LOCAL_BLOCK * LOCAL_BLOCK


def _dispatch(scal, lpos_t, y1, sorted_rows, *, rows, n_sorted):
    n = y1.shape[0]
    local_rows = _local_rows(rows)
    any_spec = pl.BlockSpec(memory_space=pl.ANY)
    in_specs = [pl.BlockSpec((1, 1, V7X_LANES), lambda i: (i, 0, 0), memory_space=pltpu.SMEM),
                pl.BlockSpec((1, 8, rows), lambda i: (i, 0, 0)),
                pl.BlockSpec((rows, D_MODEL), lambda i: (i, 0))]
    args = [scal, lpos_t, y1]
    aliases = {}
    if sorted_rows is not None:
        in_specs.append(any_spec)
        args.append(sorted_rows)
        aliases = {3: 0}
    return pl.pallas_call(
        functools.partial(_dispatch_kernel, local_rows=local_rows),
        grid=(n // rows,),
        in_specs=in_specs,
        out_specs=any_spec,
        out_shape=jax.ShapeDtypeStruct((n_sorted, D_MODEL), F32),
        scratch_shapes=[pltpu.VMEM((2, local_rows, D_MODEL), F32),
                        *[pltpu.SemaphoreType.DMA((2,))] * len(COPY_CHUNKS)],
        input_output_aliases=aliases,
        compiler_params=_params("arbitrary"),
        name="dispatch_rows",
    )(*args)


def _expert_kernel(te_ref, tr_ref, tb_ref, x_ref, wgu_ref, bgu_ref, wd_ref, bd_ref, o_ref,
                   wgu_b, wd_b):
    del tb_ref
    i = pl.program_id(0)
    e = te_ref[i]
    changed = (i == 0) | (e != te_ref[jnp.maximum(i - 1, 0)])

    @pl.when(changed)
    def _():
        wgu_b[...] = wgu_ref[0].astype(BF16)
        wd_b[...] = wd_ref[0].astype(BF16)

    def ffn(n_rows):
        row = lax.broadcasted_iota(I32, (n_rows, D_MODEL), 0)
        xb = jnp.where(row < tr_ref[i], x_ref[0:n_rows, :], 0.0).astype(BF16)
        bgu = bgu_ref[0]
        acc = None
        n_col = 1024
        for j in range(D_FF // n_col):
            cg = slice(j * n_col, (j + 1) * n_col)
            cl = slice(D_FF + j * n_col, D_FF + (j + 1) * n_col)
            hg = jnp.dot(xb, wgu_b[:, cg], preferred_element_type=F32) + bgu[:, cg]
            hl = jnp.dot(xb, wgu_b[:, cl], preferred_element_type=F32) + bgu[:, cl]
            glu = jnp.minimum(hg, SWIGLU_LIMIT)
            lin = jnp.clip(hl, -SWIGLU_LIMIT, SWIGLU_LIMIT)
            act = glu * jax.nn.sigmoid(SWIGLU_ALPHA * glu) * (lin + 1.0)
            part = jnp.dot(act.astype(BF16), wd_b[cg, :], preferred_element_type=F32)
            acc = part if acc is None else acc + part
        o_ref[0:n_rows, :] = acc + bd_ref[0]

    half = EXPERT_TILE // 2

    @pl.when(tr_ref[i] > half)
    def _():
        ffn(EXPERT_TILE)

    @pl.when((tr_ref[i] > 0) & (tr_ref[i] <= half))
    def _():
        ffn(half)


def _experts(tile_expert, tile_rows, tile_block, xs, w_gate_up, b_gate_up, w_down, b_down):
    n_tiles = xs.shape[0] // EXPERT_TILE
    grid_spec = pltpu.PrefetchScalarGridSpec(
        num_scalar_prefetch=3,
        grid=(n_tiles,),
        in_specs=[
            pl.BlockSpec((EXPERT_TILE, D_MODEL), lambda i, te, tr, tb: (tb[i], 0)),
            pl.BlockSpec((1, D_MODEL, 2 * D_FF), lambda i, te, tr, tb: (te[i], 0, 0)),
            pl.BlockSpec((1, 1, 2 * D_FF), lambda i, te, tr, tb: (te[i], 0, 0)),
            pl.BlockSpec((1, D_FF, D_MODEL), lambda i, te, tr, tb: (te[i], 0, 0)),
            pl.BlockSpec((1, 1, D_MODEL), lambda i, te, tr, tb: (te[i], 0, 0)),
        ],
        out_specs=pl.BlockSpec((EXPERT_TILE, D_MODEL), lambda i, te, tr, tb: (tb[i], 0)),
        scratch_shapes=[pltpu.VMEM((D_MODEL, 2 * D_FF), BF16), pltpu.VMEM((D_FF, D_MODEL), BF16)],
    )
    return pl.pallas_call(
        _expert_kernel,
        grid_spec=grid_spec,
        out_shape=jax.ShapeDtypeStruct(xs.shape, F32),
        compiler_params=_params("arbitrary"),
        name="expert_ffn",
    )(tile_expert, tile_rows, tile_block, xs, w_gate_up, b_gate_up, w_down, b_down)


def _combine_kernel(scal_ref, next_scal_ref, lpos_ref, gate_ref, y_ref, g_ref, b_ref, eo_ref,
                    out_ref, buf, *sem_arrays, local_rows):
    i = pl.program_id(0)
    slot = i % 2
    sems = [tuple(a.at[s] for a in sem_arrays) for s in (slot, 1 - slot)]

    @pl.when(i == 0)
    def _():
        buf[...] = jnp.zeros(buf.shape, buf.dtype)
        _start_runs(scal_ref, buf.at[0], eo_ref, tuple(a.at[0] for a in sem_arrays), to_sorted=False)

    @pl.when(i + 1 < pl.num_programs(0))
    def _():
        _start_runs(next_scal_ref, buf.at[1 - slot], eo_ref, sems[1], to_sorted=False)

    _wait_runs(scal_ref, SCAL_TOTAL, buf.at[slot], eo_ref, sems[0], to_sorted=False)
    rows = y_ref.shape[0]
    lpos = lpos_ref[...]
    gate = gate_ref[...]
    blk = LOCAL_BLOCK
    lane = lax.broadcasted_iota(I32, (rows, V7X_LANES), 1)
    slot_row = [jnp.broadcast_to(lpos[:, k:k + 1], (rows, V7X_LANES)) for k in range(TOP_K)]
    slot_gate = [jnp.broadcast_to(gate[:, k:k + 1], (rows, V7X_LANES)) for k in range(TOP_K)]
    moe = None
    for kb in range(local_rows // blk):
        pieces = []
        for jb in range(blk // V7X_LANES):
            col = lane + (kb * blk + jb * V7X_LANES)
            w = jnp.zeros((rows, V7X_LANES), F32)
            for k in range(TOP_K):
                w = jnp.where(col == slot_row[k], slot_gate[k], w)
            pieces.append(w)
        weights = jnp.concatenate(pieces, axis=1).astype(BF16)
        part = jnp.dot(weights, buf[slot, kb * blk:(kb + 1) * blk, :].astype(BF16),
                       preferred_element_type=F32)
        moe = part if moe is None else moe + part
    out_ref[...] = _layer_norm(DN_ALPHA * y_ref[...] + moe, g_ref[...], b_ref[...])


def _combine(scal, lpos, gates, y1, ln_g, ln_b, expert_out, *, rows):
    n = y1.shape[0]
    local_rows = _local_rows(rows)
    wide = pl.BlockSpec((rows, D_MODEL), lambda i: (i, 0))
    lanes = pl.BlockSpec((rows, V7X_LANES), lambda i: (i, 0))
    n_steps = n // rows
    scalars = pl.BlockSpec((1, 1, V7X_LANES), lambda i: (i, 0, 0), memory_space=pltpu.SMEM)
    next_scalars = pl.BlockSpec((1, 1, V7X_LANES), lambda i: (jnp.minimum(i + 1, n_steps - 1), 0, 0),
                                memory_space=pltpu.SMEM)
    return pl.pallas_call(
        functools.partial(_combine_kernel, local_rows=local_rows),
        grid=(n_steps,),
        in_specs=[scalars, next_scalars, lanes, lanes, wide, _resident(ln_g.shape),
                  _resident(ln_b.shape), pl.BlockSpec(memory_space=pl.ANY)],
        out_specs=wide,
        out_shape=jax.ShapeDtypeStruct((n, D_MODEL), F32),
        scratch_shapes=[pltpu.VMEM((2, local_rows, D_MODEL), F32),
                        *[pltpu.SemaphoreType.DMA((2,))] * len(COPY_CHUNKS)],
        compiler_params=_params("arbitrary"),
        name="combine_norm",
    )(scal, scal, lpos, gates, y1, ln_g, ln_b, expert_out)


def _tile_tables(lpos, n_chunk, sorted0, rows):
    n_tiles = lpos.shape[0] // rows
    local0 = (jnp.cumsum(n_chunk, axis=1) - n_chunk) * RUN_CHUNK
    per_size = [n_chunk // COPY_CHUNKS[0]] + [(n_chunk // c) % 2 for c in COPY_CHUNKS[1:]]
    total = jnp.stack([jnp.sum(n, axis=1) for n in per_size], axis=1)
    prev_total = jnp.concatenate([jnp.zeros((1, len(COPY_CHUNKS)), I32), total[:-1]], axis=0)
    pad = jnp.zeros((n_tiles, V7X_LANES - SCAL_PREV_TOTAL - len(COPY_CHUNKS)), I32)
    scal = jnp.concatenate([sorted0, n_chunk, local0, total, prev_total, pad], axis=1)
    lpos_t = jnp.transpose(lpos[:, :8].reshape(n_tiles, rows, 8), (0, 2, 1))
    return scal.reshape(n_tiles, 1, V7X_LANES), lpos_t


def kernel(x_prompt, x_sample, mem_prompt, cache_attn_k, cache_attn_v, cache_conv, cache_mem_k,
           cache_mem_v, w_in, rel_bias, conv_w, w_mem_kv, w_gate, w_br_attn, w_br_conv, w_br_mem,
           w_out, ln1_g, ln1_b, w_router, b_router, w_gate_up, b_gate_up, w_down, b_down,
           ln2_g, ln2_b):
    depth = w_in.shape[0]
    assert depth == 1, "one layer only"
    batch, seq, _ = x_prompt.shape
    dec_batch, dec_seq, _ = x_sample.shape
    n_prompt = batch * seq
    n_sample = dec_batch * dec_seq
    assert seq % ROW_TILE == 0 and HIST == ROW_TILE and cache_attn_k.shape[2] == HIST
    assert dec_seq % 16 == 0 and dec_seq <= CHUNK and cache_conv.shape[2] == 2

    w_in_b = w_in[0].astype(BF16)
    w_mem_b = w_mem_kv[0].astype(BF16)
    wr = w_router[0]
    wr_hi = wr.astype(BF16)
    wr_lo = (wr - wr_hi.astype(F32)).astype(BF16)
    merge_w = (w_gate[0].astype(BF16), w_br_attn[0].astype(BF16), w_br_conv[0].astype(BF16),
               w_br_mem[0].astype(BF16), w_out[0].astype(BF16), ln1_g, ln1_b,
               jnp.concatenate([wr_hi, wr_lo], axis=1), wr_hi, b_router)
    conv_w8 = jnp.pad(conv_w[0], ((0, 5), (0, 0)))
    bias = _band_bias(rel_bias[0])

    xp = x_prompt.reshape(n_prompt, D_MODEL)
    mk_f, mv_f, mk_b, mv_b = _memory_kv(mem_prompt.reshape(batch * MEM_TOKENS, D_MODEL), w_mem_b)
    q, k_pad, v_pad, c, qm, k_tail, v_tail, u_tail = _proj_in(
        xp, w_in_b, conv_w8, jnp.zeros((batch, 8, WIDTH), F32),
        batch=batch, seq=seq, rows=ROW_TILE, pad_steps=1)
    a, m = _attention(
        q, k_pad, v_pad, bias.reshape(ATT_HEADS * CHUNK, WINDOW), qm,
        mk_b.reshape(batch, MEM_TOKENS, WIDTH),
        mv_b.reshape(batch, MEM_TOKENS, WIDTH), batch=batch, seq=seq, rows=ROW_TILE,
        chunk_rows=CHUNK, valid_keys=BAND, first_chunk=0, mem_rows=256)
    y1_p, sel_p, topi_p, topg_p = _merge(xp, a, c, m, merge_w, rows=MERGE_TILE)

    xs = x_sample.reshape(n_sample, D_MODEL)
    conv_init = jnp.pad(cache_conv[0], ((0, 0), (6, 0), (0, 0)))
    q_s, k_s, v_s, c_s, qm_s, k_new, v_new, u_tail_s = _proj_in(
        xs, w_in_b, conv_w8, conv_init, batch=dec_batch, seq=dec_seq, rows=dec_seq, pad_steps=0)
    pad_rows = WINDOW - HIST - dec_seq

    def window(cache, new):
        cache = cache[0].reshape(dec_batch, HIST, WIDTH).astype(BF16)
        return jnp.pad(jnp.concatenate([cache, new], axis=1), ((0, 0), (0, pad_rows), (0, 0)))

    a_s, m_s = _attention(
        q_s, window(cache_attn_k, k_s), window(cache_attn_v, v_s),
        bias[:, :dec_seq].reshape(ATT_HEADS * dec_seq, WINDOW), qm_s,
        cache_mem_k[0].reshape(dec_batch, MEM_TOKENS, WIDTH).astype(BF16),
        cache_mem_v[0].reshape(dec_batch, MEM_TOKENS, WIDTH).astype(BF16),
        batch=dec_batch, seq=dec_seq, rows=dec_seq, chunk_rows=dec_seq,
        valid_keys=HIST + dec_seq, first_chunk=LEFT_CHUNKS, mem_rows=dec_seq)
    y1_s, sel_s, topi_s, topg_s = _merge(xs, a_s, c_s, m_s, merge_w, rows=n_sample)

    lpos_p, before_p, cnt_p = _ranks(sel_p, topi_p, jnp.zeros((8, N_EXPERTS), F32), rows=MOE_TILE)
    lpos_s, before_s, cnt = _ranks(sel_s, topi_s, cnt_p, rows=n_sample)
    before = jnp.concatenate([before_p[:, 0], before_s[:, 0]], axis=0).astype(I32)
    n_run = jnp.concatenate([before[1:], cnt[:1].astype(I32)], axis=0) - before
    n_chunk = (n_run + RUN_CHUNK - 1) // RUN_CHUNK
    run_rows = n_chunk * RUN_CHUNK
    counts = jnp.sum(run_rows, axis=0)
    padded = (counts + EXPERT_TILE - 1) // EXPERT_TILE * EXPERT_TILE
    ends = jnp.cumsum(padded)
    starts = ends - padded
    sorted0 = starts[None, :] + jnp.cumsum(run_rows, axis=0) - run_rows
    n_token_tiles = n_prompt // MOE_TILE + 1
    n_tiles = ((n_prompt + n_sample) * TOP_K
               + n_token_tiles * N_EXPERTS * (RUN_CHUNK - 1)) // EXPERT_TILE + N_EXPERTS
    tile_start = jnp.arange(n_tiles, dtype=I32) * EXPERT_TILE
    tile_expert = jnp.minimum(
        jnp.sum((ends[None, :] <= tile_start[:, None]).astype(I32), axis=1), N_EXPERTS - 1)
    tile_rows = jnp.clip(starts[tile_expert] + counts[tile_expert] - tile_start, 0, EXPERT_TILE)
    last_used = jnp.maximum(ends[-1] // EXPERT_TILE - 1, 0)
    tile_block = jnp.minimum(jnp.arange(n_tiles, dtype=I32), last_used)
    tile_expert = jnp.where(jnp.arange(n_tiles) > last_used, tile_expert[last_used], tile_expert)
    scal_p, lpos_tp = _tile_tables(lpos_p, n_chunk[:-1], sorted0[:-1], MOE_TILE)
    scal_s, lpos_ts = _tile_tables(lpos_s, n_chunk[-1:], sorted0[-1:], n_sample)

    n_sorted = n_tiles * EXPERT_TILE
    sorted_rows = _dispatch(scal_p, lpos_tp, y1_p, None, rows=MOE_TILE, n_sorted=n_sorted)
    sorted_rows = _dispatch(scal_s, lpos_ts, y1_s, sorted_rows, rows=n_sample, n_sorted=n_sorted)
    expert_out = _experts(tile_expert, tile_rows, tile_block, sorted_rows, w_gate_up[0],
                          b_gate_up[0].reshape(N_EXPERTS, 1, 2 * D_FF), w_down[0],
                          b_down[0].reshape(N_EXPERTS, 1, D_MODEL))
    y_p = _combine(scal_p, lpos_p, topg_p, y1_p, ln2_g, ln2_b, expert_out, rows=MOE_TILE)
    y_s = _combine(scal_s, lpos_s, topg_s, y1_s, ln2_g, ln2_b, expert_out, rows=n_sample)

    def heads(t, n_b, n_rows, n_heads, dim):
        return t.reshape(1, n_b, n_rows, n_heads, dim)

    return (y_p.reshape(batch, seq, D_MODEL),
            y_s.reshape(dec_batch, dec_seq, D_MODEL),
            heads(k_tail, batch, HIST, ATT_HEADS, ATT_DIM),
            heads(v_tail, batch, HIST, ATT_HEADS, ATT_DIM),
            u_tail[:, 6:8].reshape(1, batch, 2, WIDTH),
            heads(mk_f, batch, MEM_TOKENS, MEM_HEADS, MEM_DIM),
            heads(mv_f, batch, MEM_TOKENS, MEM_HEADS, MEM_DIM),
            heads(k_new, dec_batch, dec_seq, ATT_HEADS, ATT_DIM),
            heads(v_new, dec_batch, dec_seq, ATT_HEADS, ATT_DIM),
            u_tail_s[:, 6:8].reshape(1, dec_batch, 2, WIDTH))
```

```python
import functools

import jax
import jax.numpy as jnp
from jax import lax
from jax.experimental import pallas as pl
from jax.experimental.pallas import tpu as pltpu

F32 = jnp.float32
BF16 = jnp.bfloat16
I32 = jnp.int32

D_MODEL = 1024
CHUNK = 64
LEFT_CHUNKS = 8
HIST = LEFT_CHUNKS * CHUNK
BAND = HIST + CHUNK
ATT_HEADS = 8
ATT_DIM = 64
REL_MAX = 128
WIDTH = 512
MEM_TOKENS = 256
MEM_HEADS = 4
MEM_DIM = 128
N_EXPERTS = 32
TOP_K = 4
D_FF = 1024
SWIGLU_LIMIT = 7.0
SWIGLU_ALPHA = 1.702
DN_ALPHA = 2.0 ** 0.25
LN_EPS = 1e-5
NEG_INF = -1e30

V7X_LANES = 128
V7X_SUBLANES = 8
V7X_VMEM_BYTES = 64 * 1024 * 1024
VMEM_LIMIT = V7X_VMEM_BYTES * 7 // 8

ROW_TILE = 512
MERGE_TILE = 1024
MOE_TILE = 512
WINDOW = BAND + CHUNK
EXPERT_TILE = 1024
RUN_CHUNK = V7X_SUBLANES
LOCAL_BLOCK = 256


def _params(*sem):
    return pltpu.CompilerParams(dimension_semantics=sem, vmem_limit_bytes=VMEM_LIMIT)


def _resident(shape):
    nd = len(shape)
    return pl.BlockSpec(shape, lambda *_: (0,) * nd, pipeline_mode=pl.Buffered(1))


def _layer_norm(r, g, b):
    mu = jnp.mean(r, axis=-1, keepdims=True)
    d = r - mu
    var = jnp.mean(d * d, axis=-1, keepdims=True)
    return d * lax.rsqrt(var + LN_EPS) * g + b


def _memkv_kernel(x_ref, w_ref, kf_ref, vf_ref, kb_ref, vb_ref):
    y = jnp.dot(x_ref[...].astype(BF16), w_ref[...], preferred_element_type=F32)
    k = y[:, :WIDTH]
    v = y[:, WIDTH:]
    kf_ref[...] = k
    vf_ref[...] = v
    kb_ref[...] = k.astype(BF16)
    vb_ref[...] = v.astype(BF16)


def _memory_kv(mem2d, w_b):
    rows = mem2d.shape[0]
    tile = pl.BlockSpec((ROW_TILE, D_MODEL), lambda i: (i, 0))
    half = pl.BlockSpec((ROW_TILE, WIDTH), lambda i: (i, 0))
    return pl.pallas_call(
        _memkv_kernel,
        grid=(rows // ROW_TILE,),
        in_specs=[tile, _resident((D_MODEL, 2 * WIDTH))],
        out_specs=[half, half, half, half],
        out_shape=[jax.ShapeDtypeStruct((rows, WIDTH), F32)] * 2
        + [jax.ShapeDtypeStruct((rows, WIDTH), BF16)] * 2,
        compiler_params=_params("arbitrary"),
        name="memory_kv",
    )(mem2d, w_b)


def _proj_in_kernel(x_ref, w_ref, cw_ref, cinit_ref,
                    q_ref, k_ref, v_ref, c_ref, qm_ref, kt_ref, vt_ref, ut_ref,
                    carry_ref, *, pad_steps):
    s = pl.program_id(1)
    ns = pl.num_programs(1)
    rows = x_ref.shape[0]
    first = pad_steps
    last = ns - 1 - pad_steps

    if pad_steps:
        @pl.when((s < first) | (s > last))
        def _():
            k_ref[...] = jnp.zeros(k_ref.shape, k_ref.dtype)
            v_ref[...] = jnp.zeros(v_ref.shape, v_ref.dtype)

    @pl.when((s >= first) & (s <= last))
    def _():
        @pl.when(s == first)
        def _():
            carry_ref[...] = cinit_ref[0]

        xb = x_ref[...].astype(BF16)

        def proj(g, n=1):
            y = jnp.dot(xb, w_ref[:, g * WIDTH:(g + n) * WIDTH], preferred_element_type=F32)
            return y if n == 1 else [y[:, j * WIDTH:(j + 1) * WIDTH] for j in range(n)]

        q, k = proj(0, 2)
        q_ref[...] = (q * (ATT_DIM ** -0.5)).astype(BF16)
        k_ref[0] = k.astype(BF16)
        v, bg = proj(2, 2)
        v_ref[0] = v.astype(BF16)

        @pl.when(s == last)
        def _():
            kt_ref[0] = k
            vt_ref[0] = v

        cg, hc = proj(4, 2)
        u = cg * hc
        prev = carry_ref[...]
        last1 = prev[V7X_SUBLANES - 1:V7X_SUBLANES]
        last2 = prev[V7X_SUBLANES - 2:V7X_SUBLANES - 1]
        row = lax.broadcasted_iota(I32, u.shape, 0)
        u1 = jnp.where(row == 0, last1, pltpu.roll(u, 1, 0))
        u2 = jnp.where(row == 0, last2, jnp.where(row == 1, last1, pltpu.roll(u, 2, 0)))
        cw = cw_ref[...]
        c_ref[...] = (bg * (cw[0:1] * u2 + cw[1:2] * u1 + cw[2:3] * u)).astype(BF16)
        tail = u[rows - V7X_SUBLANES:]
        carry_ref[...] = tail

        @pl.when(s == last)
        def _():
            ut_ref[0] = tail

        qm_ref[...] = proj(6).astype(BF16)


def _proj_in(x2d, w_b, conv_w8, conv_init, *, batch, seq, rows, pad_steps):
    n_data = seq // rows
    n_steps = n_data + 2 * pad_steps

    def data_idx(b, s):
        return (b * n_data + jnp.clip(s - pad_steps, 0, n_data - 1), 0)

    wide = pl.BlockSpec((rows, D_MODEL), data_idx)
    narrow = pl.BlockSpec((rows, WIDTH), data_idx)
    seq_blk = pl.BlockSpec((1, rows, WIDTH), lambda b, s: (b, s, 0))
    tail_blk = pl.BlockSpec((1, rows, WIDTH), lambda b, s: (b, 0, 0))
    tail8 = pl.BlockSpec((1, V7X_SUBLANES, WIDTH), lambda b, s: (b, 0, 0))
    tok = jax.ShapeDtypeStruct((batch * seq, WIDTH), BF16)
    kv = jax.ShapeDtypeStruct((batch, n_steps * rows, WIDTH), BF16)
    tail = jax.ShapeDtypeStruct((batch, rows, WIDTH), F32)
    return pl.pallas_call(
        functools.partial(_proj_in_kernel, pad_steps=pad_steps),
        grid=(batch, n_steps),
        in_specs=[wide, _resident(w_b.shape), _resident(conv_w8.shape), tail8],
        out_specs=[narrow, seq_blk, seq_blk, narrow, narrow, tail_blk, tail_blk, tail8],
        out_shape=[tok, kv, kv, tok, tok, tail, tail,
                   jax.ShapeDtypeStruct((batch, V7X_SUBLANES, WIDTH), F32)],
        scratch_shapes=[pltpu.VMEM((V7X_SUBLANES, WIDTH), F32)],
        compiler_params=_params("arbitrary", "arbitrary"),
        name="proj_in",
    )(x2d, w_b, conv_w8, conv_init)


def _bias_kernel(tab_ref, out_ref):
    shape = (CHUNK, WINDOW)
    i = lax.broadcasted_iota(I32, shape, 0)
    j = lax.broadcasted_iota(I32, shape, 1)
    idx = jnp.clip(HIST + i - j, -REL_MAX, REL_MAX) + REL_MAX
    for h in range(ATT_HEADS):
        def body(d, acc, h=h):
            return jnp.where(idx == d, tab_ref[h, d], acc)
        out_ref[h] = lax.fori_loop(0, 2 * REL_MAX + 1, body, jnp.zeros(shape, F32))


def _band_bias(rel_table):
    return pl.pallas_call(
        _bias_kernel,
        in_specs=[pl.BlockSpec(memory_space=pltpu.SMEM)],
        out_shape=jax.ShapeDtypeStruct((ATT_HEADS, CHUNK, WINDOW), F32),
        name="band_bias",
    )(rel_table)


def _attention_kernel(q_ref, k_ref, v_ref, bias_ref, qm_ref, mk_ref, mv_ref,
                      a_ref, m_ref, *, chunk_rows, n_chunks, valid_keys, first_chunk, mem_rows):
    s = pl.program_id(1)
    nr = chunk_rows
    pair_w = 2 * ATT_DIM
    col = lax.broadcasted_iota(I32, (ATT_HEADS * nr, WINDOW), 1)
    lane2 = lax.broadcasted_iota(I32, (2 * nr, pair_w), 1)
    row2 = lax.broadcasted_iota(I32, (2 * nr, pair_w), 0)
    own = (lane2 >= ATT_DIM) == (row2 >= nr)
    low_lanes = lax.broadcasted_iota(I32, (nr, pair_w), 1) < ATT_DIM
    nt = (((1,), (1,)), ((), ()))

    def chunk(c, carry):
        g = first_chunk + s * n_chunks + c
        r0 = pl.multiple_of(c * nr, nr)
        w0 = pl.multiple_of((s * n_chunks + c) * CHUNK, CHUNK)
        qc = q_ref[pl.ds(r0, nr), :].astype(F32)
        kw = k_ref[0, pl.ds(w0, WINDOW), :]
        vw = v_ref[0, pl.ds(w0, WINDOW), :]
        scores = []
        for pair in range(ATT_HEADS // 2):
            lanes = slice(pair * pair_w, (pair + 1) * pair_w)
            qp = qc[:, lanes]
            q2 = jnp.where(own, jnp.concatenate([qp, qp], axis=0), 0.0).astype(BF16)
            scores.append(lax.dot_general(q2, kw[:, lanes], nt, preferred_element_type=F32))
        sc = jnp.concatenate(scores, axis=0) + bias_ref[...]
        sc = jnp.where((col >= HIST - CHUNK * g) & (col < valid_keys), sc, NEG_INF)
        m = jnp.max(sc, axis=-1, keepdims=True)
        p = jnp.exp(sc - m)
        inv = 1.0 / jnp.sum(p, axis=-1, keepdims=True)
        pb = p.astype(BF16)
        outs = []
        for pair in range(ATT_HEADS // 2):
            lanes = slice(pair * pair_w, (pair + 1) * pair_w)
            rows = slice(pair * 2 * nr, (pair + 1) * 2 * nr)
            o = jnp.dot(pb[rows], vw[:, lanes], preferred_element_type=F32) * inv[rows]
            outs.append(jnp.where(low_lanes, o[:nr], o[nr:]))
        a_ref[pl.ds(r0, nr), :] = jnp.concatenate(outs, axis=1).astype(BF16)
        return carry

    lax.fori_loop(0, n_chunks, chunk, 0, unroll=min(n_chunks, 4))

    rows = qm_ref.shape[0]
    for rb in range(rows // mem_rows):
        rs = slice(rb * mem_rows, (rb + 1) * mem_rows)
        scores = []
        for h in range(MEM_HEADS):
            lanes = slice(h * MEM_DIM, (h + 1) * MEM_DIM)
            scores.append(lax.dot_general(qm_ref[rs, lanes], mk_ref[0, :, lanes], nt,
                                          preferred_element_type=F32))
        sc = jnp.concatenate(scores, axis=0) * (MEM_DIM ** -0.5)
        m = jnp.max(sc, axis=-1, keepdims=True)
        p = jnp.exp(sc - m)
        inv = 1.0 / jnp.sum(p, axis=-1, keepdims=True)
        pb = p.astype(BF16)
        for h in range(MEM_HEADS):
            lanes = slice(h * MEM_DIM, (h + 1) * MEM_DIM)
            hs = slice(h * mem_rows, (h + 1) * mem_rows)
            o = jnp.dot(pb[hs], mv_ref[0, :, lanes], preferred_element_type=F32) * inv[hs]
            m_ref[rs, lanes] = o.astype(BF16)


def _attention(q, k_pad, v_pad, bias, qm, mk, mv, *, batch, seq, rows, chunk_rows,
               valid_keys, first_chunk, mem_rows):
    n_steps = seq // rows
    tok = pl.BlockSpec((rows, WIDTH), lambda b, s: (b * n_steps + s, 0))
    whole_seq = pl.BlockSpec((1, k_pad.shape[1], WIDTH), lambda b, s: (b, 0, 0))
    mem = pl.BlockSpec((1, MEM_TOKENS, WIDTH), lambda b, s: (b, 0, 0))
    kern = functools.partial(
        _attention_kernel, chunk_rows=chunk_rows, n_chunks=rows // chunk_rows,
        valid_keys=valid_keys, first_chunk=first_chunk, mem_rows=mem_rows)
    out = jax.ShapeDtypeStruct((batch * seq, WIDTH), BF16)
    return pl.pallas_call(
        kern,
        grid=(batch, n_steps),
        in_specs=[tok, whole_seq, whole_seq, _resident(bias.shape), tok, mem, mem],
        out_specs=[tok, tok],
        out_shape=[out, out],
        compiler_params=_params("arbitrary", "arbitrary"),
        name="attention",
    )(q, k_pad, v_pad, bias, qm, mk, mv)


def _merge_kernel(x_ref, a_ref, c_ref, m_ref, wg_ref, wa_ref, wc_ref, wm_ref, wo_ref,
                  g_ref, b_ref, wr2_ref, wrh_ref, br_ref,
                  y_ref, sel_ref, topi_ref, topg_ref):
    x = x_ref[...]
    xb = x.astype(BF16)
    comb = None
    for i, (br, wbr) in enumerate(((a_ref, wa_ref), (c_ref, wc_ref), (m_ref, wm_ref))):
        gate = jax.nn.sigmoid(jnp.dot(xb, wg_ref[:, i * D_MODEL:(i + 1) * D_MODEL],
                                      preferred_element_type=F32))
        term = gate * jnp.dot(br[...], wbr[...], preferred_element_type=F32)
        comb = term if comb is None else comb + term
    out = jnp.dot(comb.astype(BF16), wo_ref[...], preferred_element_type=F32)
    y = _layer_norm(DN_ALPHA * x + out, g_ref[...], b_ref[...])
    y_ref[...] = y

    y_hi = y.astype(BF16)
    y_lo = (y - y_hi.astype(F32)).astype(BF16)
    both = jnp.dot(y_hi, wr2_ref[...], preferred_element_type=F32)
    logits = (both[:, :N_EXPERTS] + both[:, N_EXPERTS:]
              + jnp.dot(y_lo, wrh_ref[...], preferred_element_type=F32) + br_ref[...])

    rows = logits.shape[0]
    lane = lax.broadcasted_iota(I32, (rows, N_EXPERTS), 1).astype(F32)
    lane_out = lax.broadcasted_iota(I32, (rows, V7X_LANES), 1)
    work = logits
    sel = jnp.zeros((rows, N_EXPERTS), F32)
    topi = jnp.zeros((rows, V7X_LANES), I32)
    vals = []
    for k in range(TOP_K):
        top = jnp.max(work, axis=-1, keepdims=True)
        idx = jnp.min(jnp.where(work == top, lane, float(N_EXPERTS)), axis=-1, keepdims=True)
        hit = lane == idx
        sel = jnp.where(hit, 1.0, sel)
        work = jnp.where(hit, -jnp.inf, work)
        topi = jnp.where(lane_out == k, idx.astype(I32), topi)
        vals.append(top)
    exps = [jnp.exp(v - vals[0]) for v in vals]
    denom = exps[0] + exps[1] + exps[2] + exps[3]
    topg = jnp.zeros((rows, V7X_LANES), F32)
    for k in range(TOP_K):
        topg = jnp.where(lane_out == k, exps[k] / denom, topg)
    sel_ref[...] = sel.astype(BF16)
    topi_ref[...] = topi
    topg_ref[...] = topg


def _merge(x2d, a, c, m, weights, *, rows):
    n = x2d.shape[0]
    wide = pl.BlockSpec((rows, D_MODEL), lambda i: (i, 0))
    narrow = pl.BlockSpec((rows, WIDTH), lambda i: (i, 0))
    lanes = pl.BlockSpec((rows, V7X_LANES), lambda i: (i, 0))
    return pl.pallas_call(
        _merge_kernel,
        grid=(n // rows,),
        in_specs=[wide, narrow, narrow, narrow] + [_resident(w.shape) for w in weights],
        out_specs=[wide, pl.BlockSpec((rows, N_EXPERTS), lambda i: (i, 0)), lanes, lanes],
        out_shape=[jax.ShapeDtypeStruct((n, D_MODEL), F32),
                   jax.ShapeDtypeStruct((n, N_EXPERTS), BF16),
                   jax.ShapeDtypeStruct((n, V7X_LANES), I32),
                   jax.ShapeDtypeStruct((n, V7X_LANES), F32)],
        compiler_params=_params("arbitrary"),
        name="merge_router",
    )(x2d, a, c, m, *weights)


def _rank_kernel(sel_ref, topi_ref, init_ref, lpos_ref, before_ref, cnt_ref, carry_ref):
    @pl.when(pl.program_id(0) == 0)
    def _():
        carry_ref[...] = init_ref[...]

    before_ref[0] = carry_ref[...]
    sel = sel_ref[...]
    rows = sel.shape[0]
    r = lax.broadcasted_iota(I32, (rows, rows), 0)
    c = lax.broadcasted_iota(I32, (rows, rows), 1)
    earlier = jnp.where(c < r, 1.0, 0.0).astype(BF16)
    in_tile = jnp.dot(earlier, sel, preferred_element_type=F32)
    n_run = jnp.sum(sel.astype(F32), axis=0, keepdims=True)
    run_rows = jnp.ceil(n_run * (1.0 / RUN_CHUNK)) * RUN_CHUNK
    er = lax.broadcasted_iota(I32, (N_EXPERTS, N_EXPERTS), 0)
    ec = lax.broadcasted_iota(I32, (N_EXPERTS, N_EXPERTS), 1)
    lower_experts = jnp.where(er < ec, 1.0, 0.0).astype(BF16)
    local0 = jnp.dot(jnp.broadcast_to(run_rows, (8, N_EXPERTS)).astype(BF16), lower_experts,
                     preferred_element_type=F32)[0:1]
    pos = in_tile + local0
    lane = lax.broadcasted_iota(I32, (rows, N_EXPERTS), 1)
    lane_out = lax.broadcasted_iota(I32, (rows, V7X_LANES), 1)
    topi = topi_ref[...]
    lpos = jnp.full((rows, V7X_LANES), -1, I32)
    for k in range(TOP_K):
        mine = jnp.sum(jnp.where(lane == topi[:, k:k + 1], pos, 0.0), axis=-1, keepdims=True)
        lpos = jnp.where(lane_out == k, mine.astype(I32), lpos)
    lpos_ref[...] = lpos
    total = carry_ref[...] + n_run
    carry_ref[...] = total
    cnt_ref[...] = total


def _ranks(sel, topi, init, *, rows):
    n = sel.shape[0]
    small = pl.BlockSpec((8, N_EXPERTS), lambda i: (0, 0))
    lanes = pl.BlockSpec((rows, V7X_LANES), lambda i: (i, 0))
    return pl.pallas_call(
        _rank_kernel,
        grid=(n // rows,),
        in_specs=[pl.BlockSpec((rows, N_EXPERTS), lambda i: (i, 0)), lanes, small],
        out_specs=[lanes, pl.BlockSpec((1, 8, N_EXPERTS), lambda i: (i, 0, 0)), small],
        out_shape=[jax.ShapeDtypeStruct((n, V7X_LANES), I32),
                   jax.ShapeDtypeStruct((n // rows, 8, N_EXPERTS), F32),
                   jax.ShapeDtypeStruct((8, N_EXPERTS), F32)],
        scratch_shapes=[pltpu.VMEM((8, N_EXPERTS), F32)],
        compiler_params=_params("arbitrary"),
        name="expert_ranks",
    )(sel, topi, init)


SCAL_SORTED0 = 0
SCAL_CHUNKS = N_EXPERTS
SCAL_LOCAL0 = 2 * N_EXPERTS
SCAL_TOTAL = 3 * N_EXPERTS
COPY_CHUNKS = (8, 4, 2, 1)
COPY_ROWS = tuple(c * RUN_CHUNK for c in COPY_CHUNKS)
SCAL_PREV_TOTAL = SCAL_TOTAL + len(COPY_CHUNKS)


def _run_copy(local_ref, local_row, sorted_ref, sorted_row, sems, size, to_sorted):
    loc = local_ref.at[pl.ds(local_row, COPY_ROWS[size])]
    srt = sorted_ref.at[pl.ds(sorted_row, COPY_ROWS[size])]
    sem = sems[size]
    return pltpu.make_async_copy(loc, srt, sem) if to_sorted else pltpu.make_async_copy(srt, loc, sem)


def _start_runs(scal_ref, local_ref, sorted_ref, sems, *, to_sorted):
    def expert(e, carry):
        sorted0 = scal_ref[0, 0, SCAL_SORTED0 + e]
        local0 = scal_ref[0, 0, SCAL_LOCAL0 + e]
        n_chunk = scal_ref[0, 0, SCAL_CHUNKS + e]
        n_big = lax.shift_right_logical(n_chunk, COPY_CHUNKS[0].bit_length() - 1)

        def copy(size, offset):
            return _run_copy(local_ref, pl.multiple_of(local0 + offset, RUN_CHUNK), sorted_ref,
                             pl.multiple_of(sorted0 + offset, RUN_CHUNK), sems, size, to_sorted)

        def big(c, inner):
            copy(0, c * COPY_ROWS[0]).start()
            return inner

        lax.fori_loop(0, n_big, big, 0)
        offset = n_big * COPY_ROWS[0]
        for size in range(1, len(COPY_CHUNKS)):
            take = (n_chunk & COPY_CHUNKS[size]) != 0

            @pl.when(take)
            def _(size=size, offset=offset):
                copy(size, offset).start()

            offset = offset + jnp.where(take, COPY_ROWS[size], 0)
        return carry

    lax.fori_loop(0, N_EXPERTS, expert, 0)


def _wait_runs(scal_ref, total_lane, local_ref, sorted_ref, sems, *, to_sorted):
    for size in range(len(COPY_ROWS)):
        def drain(c, carry, size=size):
            _run_copy(local_ref, 0, sorted_ref, 0, sems, size, to_sorted).wait()
            return carry

        lax.fori_loop(0, scal_ref[0, 0, total_lane + size], drain, 0)


def _dispatch_kernel(scal_ref, lpos_ref, y_ref, *rest, local_rows):
    sorted_ref, stage = rest[-2 - len(COPY_CHUNKS):-len(COPY_CHUNKS)]
    sem_arrays = rest[-len(COPY_CHUNKS):]
    i = pl.program_id(0)
    slot = i % 2
    sems = [tuple(a.at[s] for a in sem_arrays) for s in (slot, 1 - slot)]
    rows = y_ref.shape[0]
    yb = y_ref[...].astype(BF16)
    lpos = lpos_ref[0]
    blk = LOCAL_BLOCK
    for rb in range(local_rows // blk):
        j = lax.broadcasted_iota(I32, (blk, rows), 0) + rb * blk
        onehot = jnp.zeros((blk, rows), F32)
        for k in range(TOP_K):
            onehot = jnp.where(j == lpos[k:k + 1], 1.0, onehot)
        onehot = onehot.astype(BF16)
        stage[slot, rb * blk:(rb + 1) * blk, :] = jnp.dot(onehot, yb, preferred_element_type=F32)
    _start_runs(scal_ref, stage.at[slot], sorted_ref, sems[0], to_sorted=True)

    @pl.when(i > 0)
    def _():
        _wait_runs(scal_ref, SCAL_PREV_TOTAL, stage.at[1 - slot], sorted_ref, sems[1],
                   to_sorted=True)

    @pl.when(i == pl.num_programs(0) - 1)
    def _():
        _wait_runs(scal_ref, SCAL_TOTAL, stage.at[slot], sorted_ref, sems[0], to_sorted=True)


def _local_rows(rows):
    worst = rows * TOP_K + N_EXPERTS * (RUN_CHUNK - 1)
    return (worst + LOCAL_BLOCK - 1) // LOCAL_BLOCK * LOCAL_BLOCK


def _dispatch(scal, lpos_t, y1, sorted_rows, *, rows, n_sorted):
    n = y1.shape[0]
    local_rows = _local_rows(rows)
    any_spec = pl.BlockSpec(memory_space=pl.ANY)
    in_specs = [pl.BlockSpec((1, 1, V7X_LANES), lambda i: (i, 0, 0), memory_space=pltpu.SMEM),
                pl.BlockSpec((1, 8, rows), lambda i: (i, 0, 0)),
                pl.BlockSpec((rows, D_MODEL), lambda i: (i, 0))]
    args = [scal, lpos_t, y1]
    aliases = {}
    if sorted_rows is not None:
        in_specs.append(any_spec)
        args.append(sorted_rows)
        aliases = {3: 0}
    return pl.pallas_call(
        functools.partial(_dispatch_kernel, local_rows=local_rows),
        grid=(n // rows,),
        in_specs=in_specs,
        out_specs=any_spec,
        out_shape=jax.ShapeDtypeStruct((n_sorted, D_MODEL), F32),
        scratch_shapes=[pltpu.VMEM((2, local_rows, D_MODEL), F32),
                        *[pltpu.SemaphoreType.DMA((2,))] * len(COPY_CHUNKS)],
        input_output_aliases=aliases,
        compiler_params=_params("arbitrary"),
        name="dispatch_rows",
    )(*args)


def _expert_kernel(te_ref, tr_ref, tb_ref, x_ref, wgu_ref, bgu_ref, wd_ref, bd_ref, o_ref,
                   wgu_b, wd_b):
    del tb_ref
    i = pl.program_id(0)
    e = te_ref[i]
    changed = (i == 0) | (e != te_ref[jnp.maximum(i - 1, 0)])

    @pl.when(changed)
    def _():
        wgu_b[...] = wgu_ref[0].astype(BF16)
        wd_b[...] = wd_ref[0].astype(BF16)

    def ffn(n_rows):
        row = lax.broadcasted_iota(I32, (n_rows, D_MODEL), 0)
        xb = jnp.where(row < tr_ref[i], x_ref[0:n_rows, :], 0.0).astype(BF16)
        bgu = bgu_ref[0]
        hg = jnp.dot(xb, wgu_b[:, :D_FF], preferred_element_type=F32) + bgu[:, :D_FF]
        hl = jnp.dot(xb, wgu_b[:, D_FF:], preferred_element_type=F32) + bgu[:, D_FF:]
        glu = jnp.minimum(hg, SWIGLU_LIMIT)
        lin = jnp.clip(hl, -SWIGLU_LIMIT, SWIGLU_LIMIT)
        act = glu * jax.nn.sigmoid(SWIGLU_ALPHA * glu) * (lin + 1.0)
        o_ref[0:n_rows, :] = (jnp.dot(act.astype(BF16), wd_b[...], preferred_element_type=F32)
                              + bd_ref[0])

    half = EXPERT_TILE // 2

    @pl.when(tr_ref[i] > half)
    def _():
        ffn(EXPERT_TILE)

    @pl.when((tr_ref[i] > 0) & (tr_ref[i] <= half))
    def _():
        ffn(half)


def _experts(tile_expert, tile_rows, tile_block, xs, w_gate_up, b_gate_up, w_down, b_down):
    n_tiles = xs.shape[0] // EXPERT_TILE
    grid_spec = pltpu.PrefetchScalarGridSpec(
        num_scalar_prefetch=3,
        grid=(n_tiles,),
        in_specs=[
            pl.BlockSpec((EXPERT_TILE, D_MODEL), lambda i, te, tr, tb: (tb[i], 0)),
            pl.BlockSpec((1, D_MODEL, 2 * D_FF), lambda i, te, tr, tb: (te[i], 0, 0)),
            pl.BlockSpec((1, 1, 2 * D_FF), lambda i, te, tr, tb: (te[i], 0, 0)),
            pl.BlockSpec((1, D_FF, D_MODEL), lambda i, te, tr, tb: (te[i], 0, 0)),
            pl.BlockSpec((1, 1, D_MODEL), lambda i, te, tr, tb: (te[i], 0, 0)),
        ],
        out_specs=pl.BlockSpec((EXPERT_TILE, D_MODEL), lambda i, te, tr, tb: (tb[i], 0)),
        scratch_shapes=[pltpu.VMEM((D_MODEL, 2 * D_FF), BF16), pltpu.VMEM((D_FF, D_MODEL), BF16)],
    )
    return pl.pallas_call(
        _expert_kernel,
        grid_spec=grid_spec,
        out_shape=jax.ShapeDtypeStruct(xs.shape, F32),
        compiler_params=_params("arbitrary"),
        name="expert_ffn",
    )(tile_expert, tile_rows, tile_block, xs, w_gate_up, b_gate_up, w_down, b_down)


def _combine_kernel(scal_ref, next_scal_ref, lpos_ref, gate_ref, y_ref, g_ref, b_ref, eo_ref,
                    out_ref, buf, *sem_arrays, local_rows):
    i = pl.program_id(0)
    slot = i % 2
    sems = [tuple(a.at[s] for a in sem_arrays) for s in (slot, 1 - slot)]

    @pl.when(i == 0)
    def _():
        buf[...] = jnp.zeros(buf.shape, buf.dtype)
        _start_runs(scal_ref, buf.at[0], eo_ref, tuple(a.at[0] for a in sem_arrays), to_sorted=False)

    @pl.when(i + 1 < pl.num_programs(0))
    def _():
        _start_runs(next_scal_ref, buf.at[1 - slot], eo_ref, sems[1], to_sorted=False)

    _wait_runs(scal_ref, SCAL_TOTAL, buf.at[slot], eo_ref, sems[0], to_sorted=False)
    rows = y_ref.shape[0]
    lpos = lpos_ref[...]
    gate = gate_ref[...]
    blk = LOCAL_BLOCK
    lane = lax.broadcasted_iota(I32, (rows, V7X_LANES), 1)
    slot_row = [jnp.broadcast_to(lpos[:, k:k + 1], (rows, V7X_LANES)) for k in range(TOP_K)]
    slot_gate = [jnp.broadcast_to(gate[:, k:k + 1], (rows, V7X_LANES)) for k in range(TOP_K)]
    moe = None
    for kb in range(local_rows // blk):
        pieces = []
        for jb in range(blk // V7X_LANES):
            col = lane + (kb * blk + jb * V7X_LANES)
            w = jnp.zeros((rows, V7X_LANES), F32)
            for k in range(TOP_K):
                w = jnp.where(col == slot_row[k], slot_gate[k], w)
            pieces.append(w)
        weights = jnp.concatenate(pieces, axis=1).astype(BF16)
        part = jnp.dot(weights, buf[slot, kb * blk:(kb + 1) * blk, :].astype(BF16),
                       preferred_element_type=F32)
        moe = part if moe is None else moe + part
    out_ref[...] = _layer_norm(DN_ALPHA * y_ref[...] + moe, g_ref[...], b_ref[...])


def _combine(scal, lpos, gates, y1, ln_g, ln_b, expert_out, *, rows):
    n = y1.shape[0]
    local_rows = _local_rows(rows)
    wide = pl.BlockSpec((rows, D_MODEL), lambda i: (i, 0))
    lanes = pl.BlockSpec((rows, V7X_LANES), lambda i: (i, 0))
    n_steps = n // rows
    scalars = pl.BlockSpec((1, 1, V7X_LANES), lambda i: (i, 0, 0), memory_space=pltpu.SMEM)
    next_scalars = pl.BlockSpec((1, 1, V7X_LANES), lambda i: (jnp.minimum(i + 1, n_steps - 1), 0, 0),
                                memory_space=pltpu.SMEM)
    return pl.pallas_call(
        functools.partial(_combine_kernel, local_rows=local_rows),
        grid=(n_steps,),
        in_specs=[scalars, next_scalars, lanes, lanes, wide, _resident(ln_g.shape),
                  _resident(ln_b.shape), pl.BlockSpec(memory_space=pl.ANY)],
        out_specs=wide,
        out_shape=jax.ShapeDtypeStruct((n, D_MODEL), F32),
        scratch_shapes=[pltpu.VMEM((2, local_rows, D_MODEL), F32),
                        *[pltpu.SemaphoreType.DMA((2,))] * len(COPY_CHUNKS)],
        compiler_params=_params("arbitrary"),
        name="combine_norm",
    )(scal, scal, lpos, gates, y1, ln_g, ln_b, expert_out)


def _tile_tables(lpos, n_chunk, sorted0, rows):
    n_tiles = lpos.shape[0] // rows
    local0 = (jnp.cumsum(n_chunk, axis=1) - n_chunk) * RUN_CHUNK
    per_size = [n_chunk // COPY_CHUNKS[0]] + [(n_chunk // c) % 2 for c in COPY_CHUNKS[1:]]
    total = jnp.stack([jnp.sum(n, axis=1) for n in per_size], axis=1)
    prev_total = jnp.concatenate([jnp.zeros((1, len(COPY_CHUNKS)), I32), total[:-1]], axis=0)
    pad = jnp.zeros((n_tiles, V7X_LANES - SCAL_PREV_TOTAL - len(COPY_CHUNKS)), I32)
    scal = jnp.concatenate([sorted0, n_chunk, local0, total, prev_total, pad], axis=1)
    lpos_t = jnp.transpose(lpos[:, :8].reshape(n_tiles, rows, 8), (0, 2, 1))
    return scal.reshape(n_tiles, 1, V7X_LANES), lpos_t


def kernel(x_prompt, x_sample, mem_prompt, cache_attn_k, cache_attn_v, cache_conv, cache_mem_k,
           cache_mem_v, w_in, rel_bias, conv_w, w_mem_kv, w_gate, w_br_attn, w_br_conv, w_br_mem,
           w_out, ln1_g, ln1_b, w_router, b_router, w_gate_up, b_gate_up, w_down, b_down,
           ln2_g, ln2_b):
    depth = w_in.shape[0]
    assert depth == 1, "one layer only"
    batch, seq, _ = x_prompt.shape
    dec_batch, dec_seq, _ = x_sample.shape
    n_prompt = batch * seq
    n_sample = dec_batch * dec_seq
    assert seq % ROW_TILE == 0 and HIST == ROW_TILE and cache_attn_k.shape[2] == HIST
    assert dec_seq % 16 == 0 and dec_seq <= CHUNK and cache_conv.shape[2] == 2

    w_in_b = w_in[0].astype(BF16)
    w_mem_b = w_mem_kv[0].astype(BF16)
    wr = w_router[0]
    wr_hi = wr.astype(BF16)
    wr_lo = (wr - wr_hi.astype(F32)).astype(BF16)
    merge_w = (w_gate[0].astype(BF16), w_br_attn[0].astype(BF16), w_br_conv[0].astype(BF16),
               w_br_mem[0].astype(BF16), w_out[0].astype(BF16), ln1_g, ln1_b,
               jnp.concatenate([wr_hi, wr_lo], axis=1), wr_hi, b_router)
    conv_w8 = jnp.pad(conv_w[0], ((0, 5), (0, 0)))
    bias = _band_bias(rel_bias[0])

    xp = x_prompt.reshape(n_prompt, D_MODEL)
    mk_f, mv_f, mk_b, mv_b = _memory_kv(mem_prompt.reshape(batch * MEM_TOKENS, D_MODEL), w_mem_b)
    q, k_pad, v_pad, c, qm, k_tail, v_tail, u_tail = _proj_in(
        xp, w_in_b, conv_w8, jnp.zeros((batch, 8, WIDTH), F32),
        batch=batch, seq=seq, rows=ROW_TILE, pad_steps=1)
    a, m = _attention(
        q, k_pad, v_pad, bias.reshape(ATT_HEADS * CHUNK, WINDOW), qm,
        mk_b.reshape(batch, MEM_TOKENS, WIDTH),
        mv_b.reshape(batch, MEM_TOKENS, WIDTH), batch=batch, seq=seq, rows=ROW_TILE,
        chunk_rows=CHUNK, valid_keys=BAND, first_chunk=0, mem_rows=256)
    y1_p, sel_p, topi_p, topg_p = _merge(xp, a, c, m, merge_w, rows=MERGE_TILE)

    xs = x_sample.reshape(n_sample, D_MODEL)
    conv_init = jnp.pad(cache_conv[0], ((0, 0), (6, 0), (0, 0)))
    q_s, k_s, v_s, c_s, qm_s, k_new, v_new, u_tail_s = _proj_in(
        xs, w_in_b, conv_w8, conv_init, batch=dec_batch, seq=dec_seq, rows=dec_seq, pad_steps=0)
    pad_rows = WINDOW - HIST - dec_seq

    def window(cache, new):
        cache = cache[0].reshape(dec_batch, HIST, WIDTH).astype(BF16)
        return jnp.pad(jnp.concatenate([cache, new], axis=1), ((0, 0), (0, pad_rows), (0, 0)))

    a_s, m_s = _attention(
        q_s, window(cache_attn_k, k_s), window(cache_attn_v, v_s),
        bias[:, :dec_seq].reshape(ATT_HEADS * dec_seq, WINDOW), qm_s,
        cache_mem_k[0].reshape(dec_batch, MEM_TOKENS, WIDTH).astype(BF16),
        cache_mem_v[0].reshape(dec_batch, MEM_TOKENS, WIDTH).astype(BF16),
        batch=dec_batch, seq=dec_seq, rows=dec_seq, chunk_rows=dec_seq,
        valid_keys=HIST + dec_seq, first_chunk=LEFT_CHUNKS, mem_rows=dec_seq)
    y1_s, sel_s, topi_s, topg_s = _merge(xs, a_s, c_s, m_s, merge_w, rows=n_sample)

    lpos_p, before_p, cnt_p = _ranks(sel_p, topi_p, jnp.zeros((8, N_EXPERTS), F32), rows=MOE_TILE)
    lpos_s, before_s, cnt = _ranks(sel_s, topi_s, cnt_p, rows=n_sample)
    before = jnp.concatenate([before_p[:, 0], before_s[:, 0]], axis=0).astype(I32)
    n_run = jnp.concatenate([before[1:], cnt[:1].astype(I32)], axis=0) - before
    n_chunk = (n_run + RUN_CHUNK - 1) // RUN_CHUNK
    run_rows = n_chunk * RUN_CHUNK
    counts = jnp.sum(run_rows, axis=0)
    padded = (counts + EXPERT_TILE - 1) // EXPERT_TILE * EXPERT_TILE
    ends = jnp.cumsum(padded)
    starts = ends - padded
    sorted0 = starts[None, :] + jnp.cumsum(run_rows, axis=0) - run_rows
    n_token_tiles = n_prompt // MOE_TILE + 1
    n_tiles = ((n_prompt + n_sample) * TOP_K
               + n_token_tiles * N_EXPERTS * (RUN_CHUNK - 1)) // EXPERT_TILE + N_EXPERTS
    tile_start = jnp.arange(n_tiles, dtype=I32) * EXPERT_TILE
    tile_expert = jnp.minimum(
        jnp.sum((ends[None, :] <= tile_start[:, None]).astype(I32), axis=1), N_EXPERTS - 1)
    tile_rows = jnp.clip(starts[tile_expert] + counts[tile_expert] - tile_start, 0, EXPERT_TILE)
    last_used = jnp.maximum(ends[-1] // EXPERT_TILE - 1, 0)
    tile_block = jnp.minimum(jnp.arange(n_tiles, dtype=I32), last_used)
    tile_expert = jnp.where(jnp.arange(n_tiles) > last_used, tile_expert[last_used], tile_expert)
    scal_p, lpos_tp = _tile_tables(lpos_p, n_chunk[:-1], sorted0[:-1], MOE_TILE)
    scal_s, lpos_ts = _tile_tables(lpos_s, n_chunk[-1:], sorted0[-1:], n_sample)

    n_sorted = n_tiles * EXPERT_TILE
    sorted_rows = _dispatch(scal_p, lpos_tp, y1_p, None, rows=MOE_TILE, n_sorted=n_sorted)
    sorted_rows = _dispatch(scal_s, lpos_ts, y1_s, sorted_rows, rows=n_sample, n_sorted=n_sorted)
    expert_out = _experts(tile_expert, tile_rows, tile_block, sorted_rows, w_gate_up[0],
                          b_gate_up[0].reshape(N_EXPERTS, 1, 2 * D_FF), w_down[0],
                          b_down[0].reshape(N_EXPERTS, 1, D_MODEL))
    y_p = _combine(scal_p, lpos_p, topg_p, y1_p, ln2_g, ln2_b, expert_out, rows=MOE_TILE)
    y_s = _combine(scal_s, lpos_s, topg_s, y1_s, ln2_g, ln2_b, expert_out, rows=n_sample)

    def heads(t, n_b, n_rows, n_heads, dim):
        return t.reshape(1, n_b, n_rows, n_heads, dim)

    return (y_p.reshape(batch, seq, D_MODEL),
            y_s.reshape(dec_batch, dec_seq, D_MODEL),
            heads(k_tail, batch, HIST, ATT_HEADS, ATT_DIM),
            heads(v_tail, batch, HIST, ATT_HEADS, ATT_DIM),
            u_tail[:, 6:8].reshape(1, batch, 2, WIDTH),
            heads(mk_f, batch, MEM_TOKENS, MEM_HEADS, MEM_DIM),
            heads(mv_f, batch, MEM_TOKENS, MEM_HEADS, MEM_DIM),
            heads(k_new, dec_batch, dec_seq, ATT_HEADS, ATT_DIM),
            heads(v_new, dec_batch, dec_seq, ATT_HEADS, ATT_DIM),
            u_tail_s[:, 6:8].reshape(1, dec_batch, 2, WIDTH))
```

```python
import functools

import jax
import jax.numpy as jnp
from jax import lax
from jax.experimental import pallas as pl
from jax.experimental.pallas import tpu as pltpu

F32 = jnp.float32
BF16 = jnp.bfloat16
I32 = jnp.int32

D_MODEL = 1024
CHUNK = 64
LEFT_CHUNKS = 8
HIST = LEFT_CHUNKS * CHUNK
BAND = HIST + CHUNK
ATT_HEADS = 8
ATT_DIM = 64
REL_MAX = 128
WIDTH = 512
MEM_TOKENS = 256
MEM_HEADS = 4
MEM_DIM = 128
N_EXPERTS = 32
TOP_K = 4
D_FF = 1024
SWIGLU_LIMIT = 7.0
SWIGLU_ALPHA = 1.702
DN_ALPHA = 2.0 ** 0.25
LN_EPS = 1e-5
NEG_INF = -1e30

V7X_LANES = 128
V7X_SUBLANES = 8
V7X_VMEM_BYTES = 64 * 1024 * 1024
VMEM_LIMIT = V7X_VMEM_BYTES * 7 // 8

ROW_TILE = 512
MERGE_TILE = 1024
MOE_TILE = 512
WINDOW = BAND + CHUNK
EXPERT_TILE = 1024
RUN_CHUNK = V7X_SUBLANES
LOCAL_BLOCK = 256


def _params(*sem):
    return pltpu.CompilerParams(dimension_semantics=sem, vmem_limit_bytes=VMEM_LIMIT)


def _resident(shape):
    nd = len(shape)
    return pl.BlockSpec(shape, lambda *_: (0,) * nd, pipeline_mode=pl.Buffered(1))


def _layer_norm(r, g, b):
    mu = jnp.mean(r, axis=-1, keepdims=True)
    d = r - mu
    var = jnp.mean(d * d, axis=-1, keepdims=True)
    return d * lax.rsqrt(var + LN_EPS) * g + b


def _memkv_kernel(x_ref, w_ref, kf_ref, vf_ref, kb_ref, vb_ref):
    y = jnp.dot(x_ref[...].astype(BF16), w_ref[...], preferred_element_type=F32)
    k = y[:, :WIDTH]
    v = y[:, WIDTH:]
    kf_ref[...] = k
    vf_ref[...] = v
    kb_ref[...] = k.astype(BF16)
    vb_ref[...] = v.astype(BF16)


def _memory_kv(mem2d, w_b):
    rows = mem2d.shape[0]
    tile = pl.BlockSpec((ROW_TILE, D_MODEL), lambda i: (i, 0))
    half = pl.BlockSpec((ROW_TILE, WIDTH), lambda i: (i, 0))
    return pl.pallas_call(
        _memkv_kernel,
        grid=(rows // ROW_TILE,),
        in_specs=[tile, _resident((D_MODEL, 2 * WIDTH))],
        out_specs=[half, half, half, half],
        out_shape=[jax.ShapeDtypeStruct((rows, WIDTH), F32)] * 2
        + [jax.ShapeDtypeStruct((rows, WIDTH), BF16)] * 2,
        compiler_params=_params("arbitrary"),
        name="memory_kv",
    )(mem2d, w_b)


def _proj_in_kernel(x_ref, w_ref, cw_ref, cinit_ref,
                    q_ref, k_ref, v_ref, c_ref, qm_ref, kt_ref, vt_ref, ut_ref,
                    carry_ref, *, pad_steps):
    s = pl.program_id(1)
    ns = pl.num_programs(1)
    rows = x_ref.shape[0]
    first = pad_steps
    last = ns - 1 - pad_steps

    if pad_steps:
        @pl.when((s < first) | (s > last))
        def _():
            k_ref[...] = jnp.zeros(k_ref.shape, k_ref.dtype)
            v_ref[...] = jnp.zeros(v_ref.shape, v_ref.dtype)

    @pl.when((s >= first) & (s <= last))
    def _():
        @pl.when(s == first)
        def _():
            carry_ref[...] = cinit_ref[0]

        xb = x_ref[...].astype(BF16)

        def proj(g, n=1):
            y = jnp.dot(xb, w_ref[:, g * WIDTH:(g + n) * WIDTH], preferred_element_type=F32)
            return y if n == 1 else [y[:, j * WIDTH:(j + 1) * WIDTH] for j in range(n)]

        q, k = proj(0, 2)
        q_ref[...] = (q * (ATT_DIM ** -0.5)).astype(BF16)
        k_ref[0] = k.astype(BF16)
        v, bg = proj(2, 2)
        v_ref[0] = v.astype(BF16)

        @pl.when(s == last)
        def _():
            kt_ref[0] = k
            vt_ref[0] = v

        cg, hc = proj(4, 2)
        u = cg * hc
        prev = carry_ref[...]
        last1 = prev[V7X_SUBLANES - 1:V7X_SUBLANES]
        last2 = prev[V7X_SUBLANES - 2:V7X_SUBLANES - 1]
        row = lax.broadcasted_iota(I32, u.shape, 0)
        u1 = jnp.where(row == 0, last1, pltpu.roll(u, 1, 0))
        u2 = jnp.where(row == 0, last2, jnp.where(row == 1, last1, pltpu.roll(u, 2, 0)))
        cw = cw_ref[...]
        c_ref[...] = (bg * (cw[0:1] * u2 + cw[1:2] * u1 + cw[2:3] * u)).astype(BF16)
        tail = u[rows - V7X_SUBLANES:]
        carry_ref[...] = tail

        @pl.when(s == last)
        def _():
            ut_ref[0] = tail

        qm_ref[...] = proj(6).astype(BF16)


def _proj_in(x2d, w_b, conv_w8, conv_init, *, batch, seq, rows, pad_steps):
    n_data = seq // rows
    n_steps = n_data + 2 * pad_steps

    def data_idx(b, s):
        return (b * n_data + jnp.clip(s - pad_steps, 0, n_data - 1), 0)

    wide = pl.BlockSpec((rows, D_MODEL), data_idx)
    narrow = pl.BlockSpec((rows, WIDTH), data_idx)
    seq_blk = pl.BlockSpec((1, rows, WIDTH), lambda b, s: (b, s, 0))
    tail_blk = pl.BlockSpec((1, rows, WIDTH), lambda b, s: (b, 0, 0))
    tail8 = pl.BlockSpec((1, V7X_SUBLANES, WIDTH), lambda b, s: (b, 0, 0))
    tok = jax.ShapeDtypeStruct((batch * seq, WIDTH), BF16)
    kv = jax.ShapeDtypeStruct((batch, n_steps * rows, WIDTH), BF16)
    tail = jax.ShapeDtypeStruct((batch, rows, WIDTH), F32)
    return pl.pallas_call(
        functools.partial(_proj_in_kernel, pad_steps=pad_steps),
        grid=(batch, n_steps),
        in_specs=[wide, _resident(w_b.shape), _resident(conv_w8.shape), tail8],
        out_specs=[narrow, seq_blk, seq_blk, narrow, narrow, tail_blk, tail_blk, tail8],
        out_shape=[tok, kv, kv, tok, tok, tail, tail,
                   jax.ShapeDtypeStruct((batch, V7X_SUBLANES, WIDTH), F32)],
        scratch_shapes=[pltpu.VMEM((V7X_SUBLANES, WIDTH), F32)],
        compiler_params=_params("arbitrary", "arbitrary"),
        name="proj_in",
    )(x2d, w_b, conv_w8, conv_init)


def _bias_kernel(tab_ref, out_ref):
    shape = (CHUNK, WINDOW)
    i = lax.broadcasted_iota(I32, shape, 0)
    j = lax.broadcasted_iota(I32, shape, 1)
    idx = jnp.clip(HIST + i - j, -REL_MAX, REL_MAX) + REL_MAX
    for h in range(ATT_HEADS):
        def body(d, acc, h=h):
            return jnp.where(idx == d, tab_ref[h, d], acc)
        out_ref[h] = lax.fori_loop(0, 2 * REL_MAX + 1, body, jnp.zeros(shape, F32))


def _band_bias(rel_table):
    return pl.pallas_call(
        _bias_kernel,
        in_specs=[pl.BlockSpec(memory_space=pltpu.SMEM)],
        out_shape=jax.ShapeDtypeStruct((ATT_HEADS, CHUNK, WINDOW), F32),
        name="band_bias",
    )(rel_table)


def _attention_kernel(q_ref, k_ref, v_ref, bias_ref, qm_ref, mk_ref, mv_ref,
                      a_ref, m_ref, *, chunk_rows, n_chunks, valid_keys, first_chunk, mem_rows):
    s = pl.program_id(1)
    nr = chunk_rows
    pair_w = 2 * ATT_DIM
    col = lax.broadcasted_iota(I32, (ATT_HEADS * nr, WINDOW), 1)
    lane2 = lax.broadcasted_iota(I32, (2 * nr, pair_w), 1)
    row2 = lax.broadcasted_iota(I32, (2 * nr, pair_w), 0)
    own = (lane2 >= ATT_DIM) == (row2 >= nr)
    low_lanes = lax.broadcasted_iota(I32, (nr, pair_w), 1) < ATT_DIM
    nt = (((1,), (1,)), ((), ()))

    def chunk(c, carry):
        g = first_chunk + s * n_chunks + c
        r0 = pl.multiple_of(c * nr, nr)
        w0 = pl.multiple_of((s * n_chunks + c) * CHUNK, CHUNK)
        qc = q_ref[pl.ds(r0, nr), :].astype(F32)
        kw = k_ref[0, pl.ds(w0, WINDOW), :]
        vw = v_ref[0, pl.ds(w0, WINDOW), :]
        scores = []
        for pair in range(ATT_HEADS // 2):
            lanes = slice(pair * pair_w, (pair + 1) * pair_w)
            qp = qc[:, lanes]
            q2 = jnp.where(own, jnp.concatenate([qp, qp], axis=0), 0.0).astype(BF16)
            scores.append(lax.dot_general(q2, kw[:, lanes], nt, preferred_element_type=F32))
        sc = jnp.concatenate(scores, axis=0) + bias_ref[...]
        sc = jnp.where((col >= HIST - CHUNK * g) & (col < valid_keys), sc, NEG_INF)
        m = jnp.max(sc, axis=-1, keepdims=True)
        p = jnp.exp(sc - m)
        inv = 1.0 / jnp.sum(p, axis=-1, keepdims=True)
        pb = p.astype(BF16)
        outs = []
        for pair in range(ATT_HEADS // 2):
            lanes = slice(pair * pair_w, (pair + 1) * pair_w)
            rows = slice(pair * 2 * nr, (pair + 1) * 2 * nr)
            o = jnp.dot(pb[rows], vw[:, lanes], preferred_element_type=F32) * inv[rows]
            outs.append(jnp.where(low_lanes, o[:nr], o[nr:]))
        a_ref[pl.ds(r0, nr), :] = jnp.concatenate(outs, axis=1).astype(BF16)
        return carry

    lax.fori_loop(0, n_chunks, chunk, 0, unroll=min(n_chunks, 8))

    rows = qm_ref.shape[0]
    for rb in range(rows // mem_rows):
        rs = slice(rb * mem_rows, (rb + 1) * mem_rows)
        scores = []
        for h in range(MEM_HEADS):
            lanes = slice(h * MEM_DIM, (h + 1) * MEM_DIM)
            scores.append(lax.dot_general(qm_ref[rs, lanes], mk_ref[0, :, lanes], nt,
                                          preferred_element_type=F32))
        sc = jnp.concatenate(scores, axis=0) * (MEM_DIM ** -0.5)
        m = jnp.max(sc, axis=-1, keepdims=True)
        p = jnp.exp(sc - m)
        inv = 1.0 / jnp.sum(p, axis=-1, keepdims=True)
        pb = p.astype(BF16)
        for h in range(MEM_HEADS):
            lanes = slice(h * MEM_DIM, (h + 1) * MEM_DIM)
            hs = slice(h * mem_rows, (h + 1) * mem_rows)
            o = jnp.dot(pb[hs], mv_ref[0, :, lanes], preferred_element_type=F32) * inv[hs]
            m_ref[rs, lanes] = o.astype(BF16)


def _attention(q, k_pad, v_pad, bias, qm, mk, mv, *, batch, seq, rows, chunk_rows,
               valid_keys, first_chunk, mem_rows):
    n_steps = seq // rows
    tok = pl.BlockSpec((rows, WIDTH), lambda b, s: (b * n_steps + s, 0))
    whole_seq = pl.BlockSpec((1, k_pad.shape[1], WIDTH), lambda b, s: (b, 0, 0))
    mem = pl.BlockSpec((1, MEM_TOKENS, WIDTH), lambda b, s: (b, 0, 0))
    kern = functools.partial(
        _attention_kernel, chunk_rows=chunk_rows, n_chunks=rows // chunk_rows,
        valid_keys=valid_keys, first_chunk=first_chunk, mem_rows=mem_rows)
    out = jax.ShapeDtypeStruct((batch * seq, WIDTH), BF16)
    return pl.pallas_call(
        kern,
        grid=(batch, n_steps),
        in_specs=[tok, whole_seq, whole_seq, _resident(bias.shape), tok, mem, mem],
        out_specs=[tok, tok],
        out_shape=[out, out],
        compiler_params=_params("arbitrary", "arbitrary"),
        name="attention",
    )(q, k_pad, v_pad, bias, qm, mk, mv)


def _merge_kernel(x_ref, a_ref, c_ref, m_ref, wg_ref, wa_ref, wc_ref, wm_ref, wo_ref,
                  g_ref, b_ref, wr2_ref, wrh_ref, br_ref,
                  y_ref, sel_ref, topi_ref, topg_ref):
    x = x_ref[...]
    xb = x.astype(BF16)
    comb = None
    for i, (br, wbr) in enumerate(((a_ref, wa_ref), (c_ref, wc_ref), (m_ref, wm_ref))):
        gate = jax.nn.sigmoid(jnp.dot(xb, wg_ref[:, i * D_MODEL:(i + 1) * D_MODEL],
                                      preferred_element_type=F32))
        term = gate * jnp.dot(br[...], wbr[...], preferred_element_type=F32)
        comb = term if comb is None else comb + term
    out = jnp.dot(comb.astype(BF16), wo_ref[...], preferred_element_type=F32)
    y = _layer_norm(DN_ALPHA * x + out, g_ref[...], b_ref[...])
    y_ref[...] = y

    y_hi = y.astype(BF16)
    y_lo = (y - y_hi.astype(F32)).astype(BF16)
    both = jnp.dot(y_hi, wr2_ref[...], preferred_element_type=F32)
    logits = (both[:, :N_EXPERTS] + both[:, N_EXPERTS:]
              + jnp.dot(y_lo, wrh_ref[...], preferred_element_type=F32) + br_ref[...])

    rows = logits.shape[0]
    lane = lax.broadcasted_iota(I32, (rows, N_EXPERTS), 1).astype(F32)
    lane_out = lax.broadcasted_iota(I32, (rows, V7X_LANES), 1)
    work = logits
    sel = jnp.zeros((rows, N_EXPERTS), F32)
    topi = jnp.zeros((rows, V7X_LANES), I32)
    vals = []
    for k in range(TOP_K):
        top = jnp.max(work, axis=-1, keepdims=True)
        idx = jnp.min(jnp.where(work == top, lane, float(N_EXPERTS)), axis=-1, keepdims=True)
        hit = lane == idx
        sel = jnp.where(hit, 1.0, sel)
        work = jnp.where(hit, -jnp.inf, work)
        topi = jnp.where(lane_out == k, idx.astype(I32), topi)
        vals.append(top)
    exps = [jnp.exp(v - vals[0]) for v in vals]
    denom = exps[0] + exps[1] + exps[2] + exps[3]
    topg = jnp.zeros((rows, V7X_LANES), F32)
    for k in range(TOP_K):
        topg = jnp.where(lane_out == k, exps[k] / denom, topg)
    sel_ref[...] = sel.astype(BF16)
    topi_ref[...] = topi
    topg_ref[...] = topg


def _merge(x2d, a, c, m, weights, *, rows):
    n = x2d.shape[0]
    wide = pl.BlockSpec((rows, D_MODEL), lambda i: (i, 0))
    narrow = pl.BlockSpec((rows, WIDTH), lambda i: (i, 0))
    lanes = pl.BlockSpec((rows, V7X_LANES), lambda i: (i, 0))
    return pl.pallas_call(
        _merge_kernel,
        grid=(n // rows,),
        in_specs=[wide, narrow, narrow, narrow] + [_resident(w.shape) for w in weights],
        out_specs=[wide, pl.BlockSpec((rows, N_EXPERTS), lambda i: (i, 0)), lanes, lanes],
        out_shape=[jax.ShapeDtypeStruct((n, D_MODEL), F32),
                   jax.ShapeDtypeStruct((n, N_EXPERTS), BF16),
                   jax.ShapeDtypeStruct((n, V7X_LANES), I32),
                   jax.ShapeDtypeStruct((n, V7X_LANES), F32)],
        compiler_params=_params("arbitrary"),
        name="merge_router",
    )(x2d, a, c, m, *weights)


def _rank_kernel(sel_ref, topi_ref, init_ref, lpos_ref, before_ref, cnt_ref, carry_ref):
    @pl.when(pl.program_id(0) == 0)
    def _():
        carry_ref[...] = init_ref[...]

    before_ref[0] = carry_ref[...]
    sel = sel_ref[...]
    rows = sel.shape[0]
    r = lax.broadcasted_iota(I32, (rows, rows), 0)
    c = lax.broadcasted_iota(I32, (rows, rows), 1)
    earlier = jnp.where(c < r, 1.0, 0.0).astype(BF16)
    in_tile = jnp.dot(earlier, sel, preferred_element_type=F32)
    n_run = jnp.sum(sel.astype(F32), axis=0, keepdims=True)
    run_rows = jnp.ceil(n_run * (1.0 / RUN_CHUNK)) * RUN_CHUNK
    er = lax.broadcasted_iota(I32, (N_EXPERTS, N_EXPERTS), 0)
    ec = lax.broadcasted_iota(I32, (N_EXPERTS, N_EXPERTS), 1)
    lower_experts = jnp.where(er < ec, 1.0, 0.0).astype(BF16)
    local0 = jnp.dot(jnp.broadcast_to(run_rows, (8, N_EXPERTS)).astype(BF16), lower_experts,
                     preferred_element_type=F32)[0:1]
    pos = in_tile + local0
    lane = lax.broadcasted_iota(I32, (rows, N_EXPERTS), 1)
    lane_out = lax.broadcasted_iota(I32, (rows, V7X_LANES), 1)
    topi = topi_ref[...]
    lpos = jnp.full((rows, V7X_LANES), -1, I32)
    for k in range(TOP_K):
        mine = jnp.sum(jnp.where(lane == topi[:, k:k + 1], pos, 0.0), axis=-1, keepdims=True)
        lpos = jnp.where(lane_out == k, mine.astype(I32), lpos)
    lpos_ref[...] = lpos
    total = carry_ref[...] + n_run
    carry_ref[...] = total
    cnt_ref[...] = total


def _ranks(sel, topi, init, *, rows):
    n = sel.shape[0]
    small = pl.BlockSpec((8, N_EXPERTS), lambda i: (0, 0))
    lanes = pl.BlockSpec((rows, V7X_LANES), lambda i: (i, 0))
    return pl.pallas_call(
        _rank_kernel,
        grid=(n // rows,),
        in_specs=[pl.BlockSpec((rows, N_EXPERTS), lambda i: (i, 0)), lanes, small],
        out_specs=[lanes, pl.BlockSpec((1, 8, N_EXPERTS), lambda i: (i, 0, 0)), small],
        out_shape=[jax.ShapeDtypeStruct((n, V7X_LANES), I32),
                   jax.ShapeDtypeStruct((n // rows, 8, N_EXPERTS), F32),
                   jax.ShapeDtypeStruct((8, N_EXPERTS), F32)],
        scratch_shapes=[pltpu.VMEM((8, N_EXPERTS), F32)],
        compiler_params=_params("arbitrary"),
        name="expert_ranks",
    )(sel, topi, init)


SCAL_SORTED0 = 0
SCAL_CHUNKS = N_EXPERTS
SCAL_LOCAL0 = 2 * N_EXPERTS
SCAL_TOTAL = 3 * N_EXPERTS
COPY_CHUNKS = (8, 4, 2, 1)
COPY_ROWS = tuple(c * RUN_CHUNK for c in COPY_CHUNKS)
SCAL_PREV_TOTAL = SCAL_TOTAL + len(COPY_CHUNKS)


def _run_copy(local_ref, local_row, sorted_ref, sorted_row, sems, size, to_sorted):
    loc = local_ref.at[pl.ds(local_row, COPY_ROWS[size])]
    srt = sorted_ref.at[pl.ds(sorted_row, COPY_ROWS[size])]
    sem = sems[size]
    return pltpu.make_async_copy(loc, srt, sem) if to_sorted else pltpu.make_async_copy(srt, loc, sem)


def _start_runs(scal_ref, local_ref, sorted_ref, sems, *, to_sorted):
    def expert(e, carry):
        sorted0 = scal_ref[0, 0, SCAL_SORTED0 + e]
        local0 = scal_ref[0, 0, SCAL_LOCAL0 + e]
        n_chunk = scal_ref[0, 0, SCAL_CHUNKS + e]
        n_big = lax.shift_right_logical(n_chunk, COPY_CHUNKS[0].bit_length() - 1)

        def copy(size, offset):
            return _run_copy(local_ref, pl.multiple_of(local0 + offset, RUN_CHUNK), sorted_ref,
                             pl.multiple_of(sorted0 + offset, RUN_CHUNK), sems, size, to_sorted)

        def big(c, inner):
            copy(0, c * COPY_ROWS[0]).start()
            return inner

        lax.fori_loop(0, n_big, big, 0)
        offset = n_big * COPY_ROWS[0]
        for size in range(1, len(COPY_CHUNKS)):
            take = (n_chunk & COPY_CHUNKS[size]) != 0

            @pl.when(take)
            def _(size=size, offset=offset):
                copy(size, offset).start()

            offset = offset + jnp.where(take, COPY_ROWS[size], 0)
        return carry

    lax.fori_loop(0, N_EXPERTS, expert, 0)


def _wait_runs(scal_ref, total_lane, local_ref, sorted_ref, sems, *, to_sorted):
    for size in range(len(COPY_ROWS)):
        def drain(c, carry, size=size):
            _run_copy(local_ref, 0, sorted_ref, 0, sems, size, to_sorted).wait()
            return carry

        lax.fori_loop(0, scal_ref[0, 0, total_lane + size], drain, 0)


def _dispatch_kernel(scal_ref, lpos_ref, y_ref, *rest, local_rows):
    sorted_ref, stage = rest[-2 - len(COPY_CHUNKS):-len(COPY_CHUNKS)]
    sem_arrays = rest[-len(COPY_CHUNKS):]
    i = pl.program_id(0)
    slot = i % 2
    sems = [tuple(a.at[s] for a in sem_arrays) for s in (slot, 1 - slot)]
    rows = y_ref.shape[0]
    yb = y_ref[...].astype(BF16)
    lpos = lpos_ref[0]
    blk = LOCAL_BLOCK
    for rb in range(local_rows // blk):
        j = lax.broadcasted_iota(I32, (blk, rows), 0) + rb * blk
        onehot = jnp.zeros((blk, rows), F32)
        for k in range(TOP_K):
            onehot = jnp.where(j == lpos[k:k + 1], 1.0, onehot)
        onehot = onehot.astype(BF16)
        stage[slot, rb * blk:(rb + 1) * blk, :] = jnp.dot(onehot, yb, preferred_element_type=F32)
    _start_runs(scal_ref, stage.at[slot], sorted_ref, sems[0], to_sorted=True)

    @pl.when(i > 0)
    def _():
        _wait_runs(scal_ref, SCAL_PREV_TOTAL, stage.at[1 - slot], sorted_ref, sems[1],
                   to_sorted=True)

    @pl.when(i == pl.num_programs(0) - 1)
    def _():
        _wait_runs(scal_ref, SCAL_TOTAL, stage.at[slot], sorted_ref, sems[0], to_sorted=True)


def _local_rows(rows):
    worst = rows * TOP_K + N_EXPERTS * (RUN_CHUNK - 1)
    return (worst + LOCAL_BLOCK - 1) // LOCAL_BLOCK * LOCAL_BLOCK


def _dispatch(scal, lpos_t, y1, sorted_rows, *, rows, n_sorted):
    n = y1.shape[0]
    local_rows = _local_rows(rows)
    any_spec = pl.BlockSpec(memory_space=pl.ANY)
    in_specs = [pl.BlockSpec((1, 1, V7X_LANES), lambda i: (i, 0, 0), memory_space=pltpu.SMEM),
                pl.BlockSpec((1, 8, rows), lambda i: (i, 0, 0)),
                pl.BlockSpec((rows, D_MODEL), lambda i: (i, 0))]
    args = [scal, lpos_t, y1]
    aliases = {}
    if sorted_rows is not None:
        in_specs.append(any_spec)
        args.append(sorted_rows)
        aliases = {3: 0}
    return pl.pallas_call(
        functools.partial(_dispatch_kernel, local_rows=local_rows),
        grid=(n // rows,),
        in_specs=in_specs,
        out_specs=any_spec,
        out_shape=jax.ShapeDtypeStruct((n_sorted, D_MODEL), F32),
        scratch_shapes=[pltpu.VMEM((2, local_rows, D_MODEL), F32),
                        *[pltpu.SemaphoreType.DMA((2,))] * len(COPY_CHUNKS)],
        input_output_aliases=aliases,
        compiler_params=_params("arbitrary"),
        name="dispatch_rows",
    )(*args)


def _expert_kernel(te_ref, tr_ref, tb_ref, x_ref, wgu_ref, bgu_ref, wd_ref, bd_ref, o_ref,
                   wgu_b, wd_b):
    del tb_ref
    i = pl.program_id(0)
    e = te_ref[i]
    changed = (i == 0) | (e != te_ref[jnp.maximum(i - 1, 0)])

    @pl.when(changed)
    def _():
        wgu_b[...] = wgu_ref[0].astype(BF16)
        wd_b[...] = wd_ref[0].astype(BF16)

    def ffn(n_rows):
        row = lax.broadcasted_iota(I32, (n_rows, D_MODEL), 0)
        xb = jnp.where(row < tr_ref[i], x_ref[0:n_rows, :], 0.0).astype(BF16)
        bgu = bgu_ref[0]
        hg = jnp.dot(xb, wgu_b[:, :D_FF], preferred_element_type=F32) + bgu[:, :D_FF]
        hl = jnp.dot(xb, wgu_b[:, D_FF:], preferred_element_type=F32) + bgu[:, D_FF:]
        glu = jnp.minimum(hg, SWIGLU_LIMIT)
        lin = jnp.clip(hl, -SWIGLU_LIMIT, SWIGLU_LIMIT)
        act = glu * jax.nn.sigmoid(SWIGLU_ALPHA * glu) * (lin + 1.0)
        o_ref[0:n_rows, :] = (jnp.dot(act.astype(BF16), wd_b[...], preferred_element_type=F32)
                              + bd_ref[0])

    quarter = EXPERT_TILE // 4
    for n_quarters in range(1, 5):
        @pl.when((tr_ref[i] > (n_quarters - 1) * quarter) & (tr_ref[i] <= n_quarters * quarter))
        def _(n_quarters=n_quarters):
            ffn(n_quarters * quarter)


def _experts(tile_expert, tile_rows, tile_block, xs, w_gate_up, b_gate_up, w_down, b_down):
    n_tiles = xs.shape[0] // EXPERT_TILE
    grid_spec = pltpu.PrefetchScalarGridSpec(
        num_scalar_prefetch=3,
        grid=(n_tiles,),
        in_specs=[
            pl.BlockSpec((EXPERT_TILE, D_MODEL), lambda i, te, tr, tb: (tb[i], 0)),
            pl.BlockSpec((1, D_MODEL, 2 * D_FF), lambda i, te, tr, tb: (te[i], 0, 0)),
            pl.BlockSpec((1, 1, 2 * D_FF), lambda i, te, tr, tb: (te[i], 0, 0)),
            pl.BlockSpec((1, D_FF, D_MODEL), lambda i, te, tr, tb: (te[i], 0, 0)),
            pl.BlockSpec((1, 1, D_MODEL), lambda i, te, tr, tb: (te[i], 0, 0)),
        ],
        out_specs=pl.BlockSpec((EXPERT_TILE, D_MODEL), lambda i, te, tr, tb: (tb[i], 0)),
        scratch_shapes=[pltpu.VMEM((D_MODEL, 2 * D_FF), BF16), pltpu.VMEM((D_FF, D_MODEL), BF16)],
    )
    return pl.pallas_call(
        _expert_kernel,
        grid_spec=grid_spec,
        out_shape=jax.ShapeDtypeStruct(xs.shape, F32),
        compiler_params=_params("arbitrary"),
        name="expert_ffn",
    )(tile_expert, tile_rows, tile_block, xs, w_gate_up, b_gate_up, w_down, b_down)


def _combine_kernel(scal_ref, next_scal_ref, lpos_ref, gate_ref, y_ref, g_ref, b_ref, eo_ref,
                    out_ref, buf, *sem_arrays, local_rows):
    i = pl.program_id(0)
    slot = i % 2
    sems = [tuple(a.at[s] for a in sem_arrays) for s in (slot, 1 - slot)]

    @pl.when(i == 0)
    def _():
        buf[...] = jnp.zeros(buf.shape, buf.dtype)
        _start_runs(scal_ref, buf.at[0], eo_ref, tuple(a.at[0] for a in sem_arrays), to_sorted=False)

    @pl.when(i + 1 < pl.num_programs(0))
    def _():
        _start_runs(next_scal_ref, buf.at[1 - slot], eo_ref, sems[1], to_sorted=False)

    _wait_runs(scal_ref, SCAL_TOTAL, buf.at[slot], eo_ref, sems[0], to_sorted=False)
    rows = y_ref.shape[0]
    lpos = lpos_ref[...]
    gate = gate_ref[...]
    blk = LOCAL_BLOCK
    lane = lax.broadcasted_iota(I32, (rows, V7X_LANES), 1)
    slot_row = [jnp.broadcast_to(lpos[:, k:k + 1], (rows, V7X_LANES)) for k in range(TOP_K)]
    slot_gate = [jnp.broadcast_to(gate[:, k:k + 1], (rows, V7X_LANES)) for k in range(TOP_K)]
    moe = None
    for kb in range(local_rows // blk):
        pieces = []
        for jb in range(blk // V7X_LANES):
            col = lane + (kb * blk + jb * V7X_LANES)
            w = jnp.zeros((rows, V7X_LANES), F32)
            for k in range(TOP_K):
                w = jnp.where(col == slot_row[k], slot_gate[k], w)
            pieces.append(w)
        weights = jnp.concatenate(pieces, axis=1).astype(BF16)
        part = jnp.dot(weights, buf[slot, kb * blk:(kb + 1) * blk, :].astype(BF16),
                       preferred_element_type=F32)
        moe = part if moe is None else moe + part
    out_ref[...] = _layer_norm(DN_ALPHA * y_ref[...] + moe, g_ref[...], b_ref[...])


def _combine(scal, lpos, gates, y1, ln_g, ln_b, expert_out, *, rows):
    n = y1.shape[0]
    local_rows = _local_rows(rows)
    wide = pl.BlockSpec((rows, D_MODEL), lambda i: (i, 0))
    lanes = pl.BlockSpec((rows, V7X_LANES), lambda i: (i, 0))
    n_steps = n // rows
    scalars = pl.BlockSpec((1, 1, V7X_LANES), lambda i: (i, 0, 0), memory_space=pltpu.SMEM)
    next_scalars = pl.BlockSpec((1, 1, V7X_LANES), lambda i: (jnp.minimum(i + 1, n_steps - 1), 0, 0),
                                memory_space=pltpu.SMEM)
    return pl.pallas_call(
        functools.partial(_combine_kernel, local_rows=local_rows),
        grid=(n_steps,),
        in_specs=[scalars, next_scalars, lanes, lanes, wide, _resident(ln_g.shape),
                  _resident(ln_b.shape), pl.BlockSpec(memory_space=pl.ANY)],
        out_specs=wide,
        out_shape=jax.ShapeDtypeStruct((n, D_MODEL), F32),
        scratch_shapes=[pltpu.VMEM((2, local_rows, D_MODEL), F32),
                        *[pltpu.SemaphoreType.DMA((2,))] * len(COPY_CHUNKS)],
        compiler_params=_params("arbitrary"),
        name="combine_norm",
    )(scal, scal, lpos, gates, y1, ln_g, ln_b, expert_out)


def _tile_tables(lpos, n_chunk, sorted0, rows):
    n_tiles = lpos.shape[0] // rows
    local0 = (jnp.cumsum(n_chunk, axis=1) - n_chunk) * RUN_CHUNK
    per_size = [n_chunk // COPY_CHUNKS[0]] + [(n_chunk // c) % 2 for c in COPY_CHUNKS[1:]]
    total = jnp.stack([jnp.sum(n, axis=1) for n in per_size], axis=1)
    prev_total = jnp.concatenate([jnp.zeros((1, len(COPY_CHUNKS)), I32), total[:-1]], axis=0)
    pad = jnp.zeros((n_tiles, V7X_LANES - SCAL_PREV_TOTAL - len(COPY_CHUNKS)), I32)
    scal = jnp.concatenate([sorted0, n_chunk, local0, total, prev_total, pad], axis=1)
    lpos_t = jnp.transpose(lpos[:, :8].reshape(n_tiles, rows, 8), (0, 2, 1))
    return scal.reshape(n_tiles, 1, V7X_LANES), lpos_t


def kernel(x_prompt, x_sample, mem_prompt, cache_attn_k, cache_attn_v, cache_conv, cache_mem_k,
           cache_mem_v, w_in, rel_bias, conv_w, w_mem_kv, w_gate, w_br_attn, w_br_conv, w_br_mem,
           w_out, ln1_g, ln1_b, w_router, b_router, w_gate_up, b_gate_up, w_down, b_down,
           ln2_g, ln2_b):
    depth = w_in.shape[0]
    assert depth == 1, "one layer only"
    batch, seq, _ = x_prompt.shape
    dec_batch, dec_seq, _ = x_sample.shape
    n_prompt = batch * seq
    n_sample = dec_batch * dec_seq
    assert seq % ROW_TILE == 0 and HIST == ROW_TILE and cache_attn_k.shape[2] == HIST
    assert dec_seq % 16 == 0 and dec_seq <= CHUNK and cache_conv.shape[2] == 2

    w_in_b = w_in[0].astype(BF16)
    w_mem_b = w_mem_kv[0].astype(BF16)
    wr = w_router[0]
    wr_hi = wr.astype(BF16)
    wr_lo = (wr - wr_hi.astype(F32)).astype(BF16)
    merge_w = (w_gate[0].astype(BF16), w_br_attn[0].astype(BF16), w_br_conv[0].astype(BF16),
               w_br_mem[0].astype(BF16), w_out[0].astype(BF16), ln1_g, ln1_b,
               jnp.concatenate([wr_hi, wr_lo], axis=1), wr_hi, b_router)
    conv_w8 = jnp.pad(conv_w[0], ((0, 5), (0, 0)))
    bias = _band_bias(rel_bias[0])

    xp = x_prompt.reshape(n_prompt, D_MODEL)
    mk_f, mv_f, mk_b, mv_b = _memory_kv(mem_prompt.reshape(batch * MEM_TOKENS, D_MODEL), w_mem_b)
    q, k_pad, v_pad, c, qm, k_tail, v_tail, u_tail = _proj_in(
        xp, w_in_b, conv_w8, jnp.zeros((batch, 8, WIDTH), F32),
        batch=batch, seq=seq, rows=ROW_TILE, pad_steps=1)
    a, m = _attention(
        q, k_pad, v_pad, bias.reshape(ATT_HEADS * CHUNK, WINDOW), qm,
        mk_b.reshape(batch, MEM_TOKENS, WIDTH),
        mv_b.reshape(batch, MEM_TOKENS, WIDTH), batch=batch, seq=seq, rows=ROW_TILE,
        chunk_rows=CHUNK, valid_keys=BAND, first_chunk=0, mem_rows=256)
    y1_p, sel_p, topi_p, topg_p = _merge(xp, a, c, m, merge_w, rows=MERGE_TILE)

    xs = x_sample.reshape(n_sample, D_MODEL)
    conv_init = jnp.pad(cache_conv[0], ((0, 0), (6, 0), (0, 0)))
    q_s, k_s, v_s, c_s, qm_s, k_new, v_new, u_tail_s = _proj_in(
        xs, w_in_b, conv_w8, conv_init, batch=dec_batch, seq=dec_seq, rows=dec_seq, pad_steps=0)
    pad_rows = WINDOW - HIST - dec_seq

    def window(cache, new):
        cache = cache[0].reshape(dec_batch, HIST, WIDTH).astype(BF16)
        return jnp.pad(jnp.concatenate([cache, new], axis=1), ((0, 0), (0, pad_rows), (0, 0)))

    a_s, m_s = _attention(
        q_s, window(cache_attn_k, k_s), window(cache_attn_v, v_s),
        bias[:, :dec_seq].reshape(ATT_HEADS * dec_seq, WINDOW), qm_s,
        cache_mem_k[0].reshape(dec_batch, MEM_TOKENS, WIDTH).astype(BF16),
        cache_mem_v[0].reshape(dec_batch, MEM_TOKENS, WIDTH).astype(BF16),
        batch=dec_batch, seq=dec_seq, rows=dec_seq, chunk_rows=dec_seq,
        valid_keys=HIST + dec_seq, first_chunk=LEFT_CHUNKS, mem_rows=dec_seq)
    y1_s, sel_s, topi_s, topg_s = _merge(xs, a_s, c_s, m_s, merge_w, rows=n_sample)

    lpos_p, before_p, cnt_p = _ranks(sel_p, topi_p, jnp.zeros((8, N_EXPERTS), F32), rows=MOE_TILE)
    lpos_s, before_s, cnt = _ranks(sel_s, topi_s, cnt_p, rows=n_sample)
    before = jnp.concatenate([before_p[:, 0], before_s[:, 0]], axis=0).astype(I32)
    n_run = jnp.concatenate([before[1:], cnt[:1].astype(I32)], axis=0) - before
    n_chunk = (n_run + RUN_CHUNK - 1) // RUN_CHUNK
    run_rows = n_chunk * RUN_CHUNK
    counts = jnp.sum(run_rows, axis=0)
    padded = (counts + EXPERT_TILE - 1) // EXPERT_TILE * EXPERT_TILE
    ends = jnp.cumsum(padded)
    starts = ends - padded
    sorted0 = starts[None, :] + jnp.cumsum(run_rows, axis=0) - run_rows
    n_token_tiles = n_prompt // MOE_TILE + 1
    n_tiles = ((n_prompt + n_sample) * TOP_K
               + n_token_tiles * N_EXPERTS * (RUN_CHUNK - 1)) // EXPERT_TILE + N_EXPERTS
    tile_start = jnp.arange(n_tiles, dtype=I32) * EXPERT_TILE
    tile_expert = jnp.minimum(
        jnp.sum((ends[None, :] <= tile_start[:, None]).astype(I32), axis=1), N_EXPERTS - 1)
    tile_rows = jnp.clip(starts[tile_expert] + counts[tile_expert] - tile_start, 0, EXPERT_TILE)
    last_used = jnp.maximum(ends[-1] // EXPERT_TILE - 1, 0)
    tile_block = jnp.minimum(jnp.arange(n_tiles, dtype=I32), last_used)
    tile_expert = jnp.where(jnp.arange(n_tiles) > last_used, tile_expert[last_used], tile_expert)
    scal_p, lpos_tp = _tile_tables(lpos_p, n_chunk[:-1], sorted0[:-1], MOE_TILE)
    scal_s, lpos_ts = _tile_tables(lpos_s, n_chunk[-1:], sorted0[-1:], n_sample)

    n_sorted = n_tiles * EXPERT_TILE
    sorted_rows = _dispatch(scal_p, lpos_tp, y1_p, None, rows=MOE_TILE, n_sorted=n_sorted)
    sorted_rows = _dispatch(scal_s, lpos_ts, y1_s, sorted_rows, rows=n_sample, n_sorted=n_sorted)
    expert_out = _experts(tile_expert, tile_rows, tile_block, sorted_rows, w_gate_up[0],
                          b_gate_up[0].reshape(N_EXPERTS, 1, 2 * D_FF), w_down[0],
                          b_down[0].reshape(N_EXPERTS, 1, D_MODEL))
    y_p = _combine(scal_p, lpos_p, topg_p, y1_p, ln2_g, ln2_b, expert_out, rows=MOE_TILE)
    y_s = _combine(scal_s, lpos_s, topg_s, y1_s, ln2_g, ln2_b, expert_out, rows=n_sample)

    def heads(t, n_b, n_rows, n_heads, dim):
        return t.reshape(1, n_b, n_rows, n_heads, dim)

    return (y_p.reshape(batch, seq, D_MODEL),
            y_s.reshape(dec_batch, dec_seq, D_MODEL),
            heads(k_tail, batch, HIST, ATT_HEADS, ATT_DIM),
            heads(v_tail, batch, HIST, ATT_HEADS, ATT_DIM),
            u_tail[:, 6:8].reshape(1, batch, 2, WIDTH),
            heads(mk_f, batch, MEM_TOKENS, MEM_HEADS, MEM_DIM),
            heads(mv_f, batch, MEM_TOKENS, MEM_HEADS, MEM_DIM),
            heads(k_new, dec_batch, dec_seq, ATT_HEADS, ATT_DIM),
            heads(v_new, dec_batch, dec_seq, ATT_HEADS, ATT_DIM),
            u_tail_s[:, 6:8].reshape(1, dec_batch, 2, WIDTH))
```
